```python
import math
import jax, jax.numpy as jnp
from jax import lax
import numpy as np

D_MODEL = 2048
BATCH = 4
SEQ = 8192
DEPTH = 2

HEAD_DIM = 128
N_HGRN_HEADS = 8
HGRN_KEY_DIM = 128
HGRN_VAL_DIM = HEAD_DIM
HGRN_WIDTH = N_HGRN_HEADS * HGRN_VAL_DIM
HGRN_CHUNK = 64
N_ATTN_HEADS = 8
ATTN_WIDTH = N_ATTN_HEADS * HEAD_DIM
MIX_WIDTH = HGRN_WIDTH + ATTN_WIDTH
ATTN_BLOCK = 128
DILATED_PAIRS = ((128, 1), (512, 4), (2048, 16))
ROT_DIM = HEAD_DIM // 4
ROPE_THETA = 500000.0
D_FF = 4 * D_MODEL
PLE_DIM = 256
NORM_EPS = 1e-6
IN_COLS = 2 * N_HGRN_HEADS * HGRN_KEY_DIM + 2 * HGRN_WIDTH + 3 * ATTN_WIDTH

kernel_name = "hymba_hgrn2_dilated_attn_trunk"


def rmsnorm(x, gain):
    xf = x.astype(jnp.float32)
    y = xf * lax.rsqrt(jnp.mean(xf * xf, axis=-1, keepdims=True) + NORM_EPS)
    return (y * gain.astype(jnp.float32)).astype(x.dtype)


def rotary_tables(positions):
    inv_freq = 1.0 / (ROPE_THETA ** (jnp.arange(0, ROT_DIM, 2, dtype=jnp.float32) / ROT_DIM))
    ang = positions.astype(jnp.float32)[..., None] * inv_freq
    return jnp.cos(ang)[:, None], jnp.sin(ang)[:, None]


def apply_partial_rotary(x, cos, sin):
    xf = x.astype(jnp.float32)
    half = ROT_DIM // 2
    x1, x2, xp = xf[..., :half], xf[..., half:ROT_DIM], xf[..., ROT_DIM:]
    return jnp.concatenate([x1 * cos - x2 * sin, x2 * cos + x1 * sin, xp], axis=-1)


def hgrn2_mixer(zq, zf, zi, zg, lb, g_norm):
    B, T, _ = zq.shape
    H, K, V, C = N_HGRN_HEADS, HGRN_KEY_DIM, HGRN_VAL_DIM, HGRN_CHUNK

    def heads(a, dim):
        return a.reshape(B, T, H, dim).transpose(0, 2, 1, 3).astype(jnp.float32)

    q = jax.nn.silu(heads(zq, K))
    xf = heads(zf, K)
    v = heads(zi, V)
    lbh = lb.reshape(H, 1, K).astype(jnp.float32)
    log_f = jnp.logaddexp(jnp.log(lbh), jnp.log1p(-lbh) + jax.nn.log_sigmoid(xf))
    k = (1.0 - lbh) * jax.nn.sigmoid(-xf)

    NC = T // C

    def chunks(a):
        return jnp.moveaxis(a.reshape(B, H, NC, C, a.shape[-1]), 2, 0)

    tril = jnp.tril(jnp.ones((C, C), dtype=bool))[:, :, None]

    def step(S, inp):
        qc, kc, vc, lfc = inp
        b = jnp.cumsum(lfc, axis=-2)
        diff = b[:, :, :, None, :] - b[:, :, None, :, :]
        decay = jnp.exp(jnp.where(tril, diff, -jnp.inf))
        A = jnp.einsum('bhtk,bhtsk,bhsk->bhts', qc, decay, kc)
        o = jnp.einsum('bhts,bhsv->bhtv', A, vc) + jnp.einsum('bhtk,bhkv->bhtv', qc * jnp.exp(b), S)
        bC = b[:, :, -1:, :]
        S = jnp.exp(bC[:, :, 0, :])[..., None] * S + jnp.einsum('bhsk,bhsv->bhkv', kc * jnp.exp(bC - b), vc)
        return S, o

    S0 = jnp.zeros((B, H, K, V), jnp.float32)
    _, o = lax.scan(step, S0, (chunks(q), chunks(k), chunks(v), chunks(log_f)))
    o = jnp.moveaxis(o, 0, 2).reshape(B, H, T, V)
    o = rmsnorm(o, g_norm)
    o = o.transpose(0, 2, 1, 3).reshape(B, T, H * V)
    return (o * jax.nn.silu(zg.astype(jnp.float32))).astype(zq.dtype)


def dilated_branch(q, k, v, window, dilation):
    B, H, T, E = q.shape
    Q = ATTN_BLOCK
    W = window // dilation
    span = dilation * Q
    T_pad = -(-T // span) * span
    N = T_pad // dilation
    NB = N // Q

    def to_blocks(a):
        a = jnp.pad(a, ((0, 0), (0, 0), (0, T_pad - T), (0, 0)))
        a = a.reshape(B, H, N, dilation, E).transpose(0, 1, 3, 2, 4)
        return a.reshape(B, H, dilation, NB, Q, E)

    def with_prev(a):
        prev = jnp.pad(a[:, :, :, :-1], ((0, 0), (0, 0), (0, 0), (1, 0), (0, 0), (0, 0)))
        return jnp.concatenate([prev, a], axis=-2)

    qb = to_blocks(q)
    kc = with_prev(to_blocks(k))
    vc = with_prev(to_blocks(v))
    s = jnp.einsum('bhrnqe,bhrnke->bhrnqk', qb, kc)
    qi = jnp.arange(Q)[:, None]
    kj = jnp.arange(2 * Q)[None, :]
    dist = qi + Q - kj
    band = (dist >= 0) & (dist <= W)
    first = (jnp.arange(NB) == 0)[:, None, None]
    valid = band[None] & ~(first & (kj < Q)[None])
    s = jnp.where(valid, s, -jnp.inf)
    m = jnp.max(s, axis=-1, keepdims=True)
    e = jnp.exp(s - m)
    den = jnp.sum(e, axis=-1)
    o = jnp.einsum('bhrnqk,bhrnke->bhrnqe', e, vc) / den[..., None]

    def from_blocks(a):
        tail = a.shape[5:]
        a = a.reshape((B, H, dilation, N) + tail)
        a = jnp.moveaxis(a, 2, 3).reshape((B, H, T_pad) + tail)
        return a[:, :, :T]

    return from_blocks(o), from_blocks(m[..., 0]), from_blocks(den)


def dilated_attention_mixer(zq, zk, zv, cos, sin, attn_norm):
    B, T, _ = zq.shape
    H = N_ATTN_HEADS

    def heads(a):
        return a.reshape(B, T, H, HEAD_DIM).transpose(0, 2, 1, 3)

    q = apply_partial_rotary(heads(zq), cos, sin) * (HEAD_DIM ** -0.5)
    k = apply_partial_rotary(heads(zk), cos, sin)
    v = heads(zv).astype(jnp.float32)
    outs = [dilated_branch(q, k, v, w, d) for (w, d) in DILATED_PAIRS]
    m_all = jnp.stack([m for (_, m, _) in outs], axis=0)
    m_max = jnp.max(m_all, axis=0)
    wts = jnp.stack([den * jnp.exp(m - m_max) for (_, m, den) in outs], axis=0)
    o_all = jnp.stack([o for (o, _, _) in outs], axis=0)
    o = jnp.sum(wts[..., None] * o_all, axis=0) / jnp.sum(wts, axis=0)[..., None]
    o = rmsnorm(o, attn_norm)
    return o.transpose(0, 2, 1, 3).reshape(B, T, H * HEAD_DIM).astype(zq.dtype)


def setup_inputs(seed: int = 0) -> dict:
    key = jax.random.key(seed)
    ks = jax.random.split(key, 18)
    f32 = jnp.float32

    def nrm(k, shape, scale):
        return jax.random.normal(k, shape, f32) * scale

    def gain(k, shape):
        return 1.0 + 0.02 * jax.random.normal(k, shape, f32)

    x = jax.random.normal(ks[0], (BATCH, SEQ, D_MODEL), f32)
    p = jax.random.normal(ks[1], (DEPTH, BATCH, SEQ, PLE_DIM), f32)
    offset = jax.random.randint(ks[2], (BATCH, 1), 0, 4096, dtype=jnp.int32)
    positions = (offset + jnp.arange(SEQ, dtype=jnp.int32)[None, :]).astype(jnp.int32)
    return {
        "x": x,
        "p": p,
        "positions": positions,
        "norm1": gain(ks[3], (DEPTH, D_MODEL)),
        "w_in": nrm(ks[4], (DEPTH, D_MODEL, IN_COLS), D_MODEL ** -0.5),
        "lb_param": nrm(ks[5], (DEPTH, N_HGRN_HEADS * HGRN_KEY_DIM), 0.5),
        "hgrn_norm": gain(ks[6], (DEPTH, HGRN_VAL_DIM)),
        "attn_norm": gain(ks[7], (DEPTH, HEAD_DIM)),
        "w_out": nrm(ks[8], (DEPTH, MIX_WIDTH, D_MODEL), MIX_WIDTH ** -0.5),
        "norm2": gain(ks[9], (DEPTH, D_MODEL)),
        "w1": nrm(ks[10], (DEPTH, D_MODEL, D_FF), D_MODEL ** -0.5),
        "w2": nrm(ks[11], (DEPTH, D_FF, D_MODEL), D_FF ** -0.5),
        "ple_norm": gain(ks[12], (DEPTH, D_MODEL)),
        "w_pg": nrm(ks[13], (DEPTH, D_MODEL, D_MODEL), D_MODEL ** -0.5),
        "w_pp": nrm(ks[14], (DEPTH, PLE_DIM, D_MODEL), PLE_DIM ** -0.5),
        "final_norm": gain(ks[15], (D_MODEL,)),
    }


def reference(x, p, positions, norm1, w_in, lb_param, hgrn_norm, attn_norm, w_out,
              norm2, w1, w2, ple_norm, w_pg, w_pp, final_norm):
    cos, sin = rotary_tables(positions)
    lbs = jnp.cumsum(jax.nn.softmax(lb_param.astype(jnp.float32), axis=0), axis=0)
    lbs = lbs - lbs[0:1]
    FK = N_HGRN_HEADS * HGRN_KEY_DIM
    splits = np.cumsum([FK, FK, HGRN_WIDTH, HGRN_WIDTH, ATTN_WIDTH, ATTN_WIDTH]).tolist()
    h = x
    for i in range(DEPTH):
        u = rmsnorm(h, norm1[i])
        z = jnp.einsum('btd,dc->btc', u, w_in[i])
        hq, hf, hi, hg, aq, ak, av = jnp.split(z, splits, axis=-1)
        o_hgrn = hgrn2_mixer(hq, hf, hi, hg, lbs[i], hgrn_norm[i])
        o_attn = dilated_attention_mixer(aq, ak, av, cos, sin, attn_norm[i])
        mix = jnp.concatenate([o_hgrn, o_attn], axis=-1)
        h = h + jnp.einsum('btc,cd->btd', mix, w_out[i])
        u2 = rmsnorm(h, norm2[i])
        a = jax.nn.relu(jnp.einsum('btd,df->btf', u2, w1[i]))
        h = h + jnp.einsum('btf,fd->btd', a * a, w2[i])
        gate = jax.nn.sigmoid(jnp.einsum('btd,de->bte', rmsnorm(h, ple_norm[i]), w_pg[i]))
        h = h + jnp.einsum('btk,kd->btd', p[i], w_pp[i]) * gate
    return rmsnorm(h, final_norm)
```

```python
import functools
import math

import jax
import jax.numpy as jnp
from jax import lax
from jax.experimental import pallas as pl
from jax.experimental.pallas import tpu as pltpu

F32 = jnp.float32
BF16 = jnp.bfloat16

HEAD_DIM = 128
N_HEADS = 8
GROUP_WIDTH = N_HEADS * HEAD_DIM
ROT_DIM = HEAD_DIM // 4
ROPE_THETA = 500000.0
NORM_EPS = 1e-6
HGRN_CHUNK = 64
HGRN_SUB = 16
ATTN_BLOCK = 128
ATTN_TILE = 2048
DILATIONS = (1, 4, 16)
VMEM_LIMIT = 56 * 1024 * 1024


def _params(semantics):
    return pltpu.CompilerParams(dimension_semantics=semantics, vmem_limit_bytes=VMEM_LIMIT)


def _rms(x, gain):
    ms = jnp.mean(x * x, axis=-1, keepdims=True)
    return x * lax.rsqrt(ms + NORM_EPS) * gain


def _norm_kernel(x_ref, g_ref, o_ref):
    o_ref[...] = _rms(x_ref[...], g_ref[...]).astype(o_ref.dtype)


def _norm(x, gain, out_dtype, tm=512):
    m, d = x.shape
    return pl.pallas_call(
        _norm_kernel,
        grid=(m // tm,),
        in_specs=[pl.BlockSpec((tm, d), lambda i: (i, 0)),
                  pl.BlockSpec((1, d), lambda i: (0, 0))],
        out_specs=pl.BlockSpec((tm, d), lambda i: (i, 0)),
        out_shape=jax.ShapeDtypeStruct((m, d), out_dtype),
        compiler_params=_params(("parallel",)),
        name="rmsnorm",
    )(x, gain)


def _rope_kernel(pos_ref, invf_ref, cos_ref, sin_ref):
    ang = pos_ref[...] * invf_ref[...]
    lane = lax.broadcasted_iota(jnp.int32, ang.shape, 1)
    s = jnp.sin(ang)
    cos_ref[...] = jnp.cos(ang)
    sin_ref[...] = jnp.where(lane < ROT_DIM // 2, -s, s)


def _rope_tables(positions, tr=1024):
    n = positions.size
    pos = positions.astype(F32).reshape(n, 1)
    half = ROT_DIM // 2
    inv = 1.0 / (ROPE_THETA ** (jnp.arange(0, ROT_DIM, 2, dtype=F32) / ROT_DIM))
    invf = jnp.concatenate([inv, inv, jnp.zeros((HEAD_DIM - 2 * half,), F32)]).reshape(1, HEAD_DIM)
    return pl.pallas_call(
        _rope_kernel,
        grid=(n // tr,),
        in_specs=[pl.BlockSpec((tr, 1), lambda i: (i, 0)),
                  pl.BlockSpec((1, HEAD_DIM), lambda i: (0, 0))],
        out_specs=[pl.BlockSpec((tr, HEAD_DIM), lambda i: (i, 0))] * 2,
        out_shape=[jax.ShapeDtypeStruct((n, HEAD_DIM), F32)] * 2,
        compiler_params=_params(("parallel",)),
        name="rope_tables",
    )(pos, invf)


def _matmul_kernel(x_ref, w_ref, o_ref):
    o_ref[...] = jnp.dot(x_ref[...], w_ref[...], preferred_element_type=F32).astype(o_ref.dtype)


def _in_proj(u, w, tm=1024, tn=1024):
    m, k = u.shape
    n = w.shape[1]
    return pl.pallas_call(
        _matmul_kernel,
        grid=(m // tm, n // tn),
        in_specs=[pl.BlockSpec((tm, k), lambda i, j: (i, 0)),
                  pl.BlockSpec((k, tn), lambda i, j: (0, j))],
        out_specs=pl.BlockSpec((tm, tn), lambda i, j: (i, j)),
        out_shape=jax.ShapeDtypeStruct((m, n), F32),
        compiler_params=_params(("parallel", "parallel")),
        name="in_proj",
    )(u, w)


def _split3(x):
    hi = x.astype(BF16)
    r1 = x - hi.astype(F32)
    mid = r1.astype(BF16)
    lo = (r1 - mid.astype(F32)).astype(BF16)
    return hi, mid, lo


def _hgrn_kernel(layer, zq_ref, zf_ref, zi_ref, zg_ref, lbp_ref, gn_ref, o_ref,
                 st_ref, b_ref, k_ref):
    C, SB = HGRN_CHUNK, HGRN_SUB
    tb = zq_ref.shape[0]

    @pl.when(pl.program_id(2) == 0)
    def _():
        st_ref[...] = jnp.zeros_like(st_ref)

    lbp = lbp_ref[...]
    e = jnp.exp(lbp - jnp.max(lbp, axis=0, keepdims=True))
    sm = e / jnp.sum(e, axis=0, keepdims=True)
    lb = jnp.zeros((1, HEAD_DIM), F32)
    for j in range(1, layer + 1):
        lb = lb + sm[j:j + 1, :]
    lb_pos = lb > 0.0
    log_lb = jnp.log(jnp.where(lb_pos, lb, 1.0))
    log1m_lb = jnp.log1p(-lb)
    one_m_lb = 1.0 - lb
    gn = gn_ref[...]

    row = lax.broadcasted_iota(jnp.int32, (C, C), 0)
    col = lax.broadcasted_iota(jnp.int32, (C, C), 1)
    tril = jnp.where(row >= col, 1.0, 0.0).astype(BF16)
    ones = jnp.ones((HEAD_DIM, HEAD_DIM), BF16)
    sub_t = lax.broadcasted_iota(jnp.int32, (SB, HEAD_DIM), 0)

    def chunk(c, carry):
        r0 = pl.multiple_of(c * C, C)
        zq = zq_ref[pl.ds(r0, C), :]
        zf = zf_ref[pl.ds(r0, C), :]
        v = zi_ref[pl.ds(r0, C), :]
        zg = zg_ref[pl.ds(r0, C), :]

        q = zq / (1.0 + jnp.exp(-zq))
        en = jnp.exp(-jnp.abs(zf))
        rcp = 1.0 / (1.0 + en)
        sig_neg = jnp.where(zf >= 0.0, en * rcp, rcp)
        log_sig = jnp.minimum(zf, 0.0) - jnp.log1p(en)
        t2 = log1m_lb + log_sig
        mx = jnp.maximum(log_lb, t2)
        lae = mx + jnp.log1p(jnp.exp(-jnp.abs(log_lb - t2)))
        lf = jnp.where(lb_pos, lae, t2)
        kk = one_m_lb * sig_neg

        hi, mid, lo = _split3(lf)
        b = (jnp.dot(tril, hi, preferred_element_type=F32)
             + jnp.dot(tril, mid, preferred_element_type=F32)
             + jnp.dot(tril, lo, preferred_element_type=F32))
        b_ref[...] = b
        k_ref[...] = kk
        vb = v.astype(BF16)
        st = st_ref[...]

        o_inter = lax.dot_general((q * jnp.exp(b)).astype(BF16), st.astype(BF16),
                                  (((1,), (1,)), ((), ())), preferred_element_type=F32)
        outs = []
        for i in range(C // SB):
            lo_r = i * SB
            q_i = q[lo_r:lo_r + SB, :]
            b_i = b[lo_r:lo_r + SB, :]
            o_i = o_inter[lo_r:lo_r + SB, :]
            if i > 0:
                bref = b[lo_r - 1:lo_r, :]
                qs = (q_i * jnp.exp(b_i - bref)).astype(BF16)
                ks = (kk[:lo_r, :] * jnp.exp(bref - b[:lo_r, :])).astype(BF16)
                a = lax.dot_general(qs, ks, (((1,), (1,)), ((), ())), preferred_element_type=F32)
                o_i = o_i + jnp.dot(a.astype(BF16), vb[:lo_r, :], preferred_element_type=F32)
            slabs = []
            for s in range(SB):
                b_s = b_ref[pl.ds(lo_r + s, 1), :]
                k_s = k_ref[pl.ds(lo_r + s, 1), :]
                w = q_i * k_s * jnp.exp(jnp.minimum(b_i - b_s, 0.0))
                slabs.append(jnp.where(sub_t >= s, w, 0.0).astype(BF16))
            red = jnp.dot(jnp.concatenate(slabs, axis=0), ones, preferred_element_type=F32)
            for s in range(SB):
                o_i = o_i + red[s * SB:(s + 1) * SB, :] * v[lo_r + s:lo_r + s + 1, :]
            outs.append(o_i)
        o = jnp.concatenate(outs, axis=0)

        b_end = b[C - 1:C, :]
        kd = (kk * jnp.exp(b_end - b)).astype(BF16)
        upd = lax.dot_general(vb, kd, (((0,), (0,)), ((), ())), preferred_element_type=F32)
        st_ref[...] = st * jnp.exp(b_end) + upd

        o = _rms(o, gn)
        o = o * (zg / (1.0 + jnp.exp(-zg)))
        o_ref[pl.ds(r0, C), :] = o.astype(o_ref.dtype)
        return carry

    lax.fori_loop(0, tb // C, chunk, 0)


def _hgrn(z, lb_param, g_norm, layer, batch, seq, tb=512):
    nt = seq // tb
    nh = N_HEADS

    def col(off):
        return pl.BlockSpec((tb, HEAD_DIM), lambda b, h, t, off=off: (b * nt + t, off + h))

    return pl.pallas_call(
        functools.partial(_hgrn_kernel, layer),
        grid=(batch, nh, nt),
        in_specs=[col(0), col(nh), col(2 * nh), col(3 * nh),
                  pl.BlockSpec((lb_param.shape[0], HEAD_DIM), lambda b, h, t: (0, h)),
                  pl.BlockSpec((1, HEAD_DIM), lambda b, h, t: (0, 0))],
        out_specs=pl.BlockSpec((tb, HEAD_DIM), lambda b, h, t: (b * nt + t, h)),
        out_shape=jax.ShapeDtypeStruct((batch * seq, GROUP_WIDTH), BF16),
        scratch_shapes=[pltpu.VMEM((HEAD_DIM, HEAD_DIM), F32),
                        pltpu.VMEM((HGRN_CHUNK, HEAD_DIM), F32),
                        pltpu.VMEM((HGRN_CHUNK, HEAD_DIM), F32)],
        compiler_params=_params(("parallel", "parallel", "arbitrary")),
        name="hgrn2",
    )(z, z, z, z, lb_param, g_norm)


def _rotary(x, cos, sin_signed, lane):
    partner = jnp.where(lane < ROT_DIM // 2,
                        pltpu.roll(x, HEAD_DIM - ROT_DIM // 2, axis=1),
                        pltpu.roll(x, ROT_DIM // 2, axis=1))
    return x * cos + partner * sin_signed


def _attn_kernel(zq_ref, zk_ref, zv_ref, cos_ref, sin_ref, gn_ref, o_ref,
                 q_buf, k_buf, v_buf, acc_buf, m_buf, l_buf):
    TQ, Q = ATTN_TILE, ATTN_BLOCK
    t = pl.program_id(2)
    RB = 256
    lane = lax.broadcasted_iota(jnp.int32, (RB, HEAD_DIM), 1)

    @pl.when(t == 0)
    def _():
        k_buf[pl.ds(0, TQ), :] = jnp.zeros((TQ, HEAD_DIM), F32)
        v_buf[pl.ds(0, TQ), :] = jnp.zeros((TQ, HEAD_DIM), F32)

    scale = HEAD_DIM ** -0.5
    for r in range(TQ // RB):
        rows = pl.ds(r * RB, RB)
        c, s = cos_ref[rows, :], sin_ref[rows, :]
        q_buf[rows, :] = _rotary(zq_ref[rows, :], c, s, lane) * scale
        k_buf[pl.ds(TQ + r * RB, RB), :] = _rotary(zk_ref[rows, :], c, s, lane)
        v_buf[pl.ds(TQ + r * RB, RB), :] = zv_ref[rows, :]

    qi = lax.broadcasted_iota(jnp.int32, (Q, Q), 0)
    kj = lax.broadcasted_iota(jnp.int32, (Q, Q), 1)
    band_prev = kj >= qi
    band_cur = kj <= qi
    has_prev_tile = t > 0
    neg = -jnp.inf
    nt_dims = (((1,), (1,)), ((), ()))

    for g, d in enumerate(DILATIONS):
        for res in range(d):
            for nb in range(TQ // (Q * d)):
                cur = TQ + res + d * Q * nb
                prev = cur - d * Q

                def rows_of(start):
                    return pl.ds(start, Q, stride=d) if d > 1 else pl.ds(start, Q)

                qb = q_buf[rows_of(cur - TQ), :].astype(BF16)
                kp = k_buf[rows_of(prev), :].astype(BF16)
                kc = k_buf[rows_of(cur), :].astype(BF16)
                vp = v_buf[rows_of(prev), :].astype(BF16)
                vc = v_buf[rows_of(cur), :].astype(BF16)
                sp = lax.dot_general(qb, kp, nt_dims, preferred_element_type=F32)
                sc = lax.dot_general(qb, kc, nt_dims, preferred_element_type=F32)
                ok_prev = band_prev if nb > 0 else jnp.logical_and(band_prev, has_prev_tile)
                sp = jnp.where(ok_prev, sp, neg)
                sc = jnp.where(band_cur, sc, neg)
                m = jnp.maximum(jnp.max(sp, axis=-1, keepdims=True),
                                jnp.max(sc, axis=-1, keepdims=True))
                pp = jnp.exp(sp - m)
                pc = jnp.exp(sc - m)
                l = jnp.sum(pp, axis=-1, keepdims=True) + jnp.sum(pc, axis=-1, keepdims=True)
                acc = (jnp.dot(pp.astype(BF16), vp, preferred_element_type=F32)
                       + jnp.dot(pc.astype(BF16), vc, preferred_element_type=F32))
                out_rows = rows_of(cur - TQ)
                acc_buf[g, out_rows, :] = acc
                m_buf[g, out_rows, :] = jnp.broadcast_to(m, (Q, HEAD_DIM))
                l_buf[g, out_rows, :] = jnp.broadcast_to(l, (Q, HEAD_DIM))

    gn = gn_ref[...]
    for r in range(TQ // RB):
        rows = pl.ds(r * RB, RB)
        ms = [m_buf[g, rows, :] for g in range(len(DILATIONS))]
        m_max = functools.reduce(jnp.maximum, ms)
        num = jnp.zeros((RB, HEAD_DIM), F32)
        den = jnp.zeros((RB, HEAD_DIM), F32)
        for g in range(len(DILATIONS)):
            w = jnp.exp(ms[g] - m_max)
            num = num + w * acc_buf[g, rows, :]
            den = den + w * l_buf[g, rows, :]
        o_ref[rows, :] = _rms(num / den, gn).astype(o_ref.dtype)

    k_buf[pl.ds(0, TQ), :] = k_buf[pl.ds(TQ, TQ), :]
    v_buf[pl.ds(0, TQ), :] = v_buf[pl.ds(TQ, TQ), :]


def _attn(z, cos, sin, g_norm, batch, seq):
    TQ = ATTN_TILE
    nt = seq // TQ
    nh = N_HEADS
    nb = len(DILATIONS)

    def col(off):
        return pl.BlockSpec((TQ, HEAD_DIM), lambda b, h, t, off=off: (b * nt + t, off + h))

    tab = pl.BlockSpec((TQ, HEAD_DIM), lambda b, h, t: (b * nt + t, 0))
    return pl.pallas_call(
        _attn_kernel,
        grid=(batch, nh, nt),
        in_specs=[col(4 * nh), col(5 * nh), col(6 * nh), tab, tab,
                  pl.BlockSpec((1, HEAD_DIM), lambda b, h, t: (0, 0))],
        out_specs=pl.BlockSpec((TQ, HEAD_DIM), lambda b, h, t: (b * nt + t, h)),
        out_shape=jax.ShapeDtypeStruct((batch * seq, GROUP_WIDTH), BF16),
        scratch_shapes=[pltpu.VMEM((TQ, HEAD_DIM), F32),
                        pltpu.VMEM((2 * TQ, HEAD_DIM), F32),
                        pltpu.VMEM((2 * TQ, HEAD_DIM), F32),
                        pltpu.VMEM((nb, TQ, HEAD_DIM), F32),
                        pltpu.VMEM((nb, TQ, HEAD_DIM), F32),
                        pltpu.VMEM((nb, TQ, HEAD_DIM), F32)],
        compiler_params=_params(("parallel", "parallel", "arbitrary")),
        name="dilated_attn",
    )(z, z, z, cos, sin, g_norm)


def _out_proj_kernel(oh_ref, oa_ref, w_ref, h_ref, g_ref, h_out_ref, u_out_ref):
    half = oh_ref.shape[1]
    acc = jnp.dot(oh_ref[...], w_ref[pl.ds(0, half), :], preferred_element_type=F32)
    acc = acc + jnp.dot(oa_ref[...], w_ref[pl.ds(half, half), :], preferred_element_type=F32)
    h = h_ref[...] + acc
    h_out_ref[...] = h
    u_out_ref[...] = _rms(h, g_ref[...]).astype(u_out_ref.dtype)


def _out_proj(oh, oa, w, h, gain, tm=512):
    m, d = h.shape
    half = oh.shape[1]
    row = lambda i: (i, 0)
    fixed = lambda i: (0, 0)
    return pl.pallas_call(
        _out_proj_kernel,
        grid=(m // tm,),
        in_specs=[pl.BlockSpec((tm, half), row), pl.BlockSpec((tm, half), row),
                  pl.BlockSpec(w.shape, fixed), pl.BlockSpec((tm, d), row),
                  pl.BlockSpec((1, d), fixed)],
        out_specs=[pl.BlockSpec((tm, d), row), pl.BlockSpec((tm, d), row)],
        out_shape=[jax.ShapeDtypeStruct((m, d), F32), jax.ShapeDtypeStruct((m, d), BF16)],
        compiler_params=_params(("parallel",)),
        name="out_proj",
    )(oh, oa, w, h, gain)


def _mlp_kernel(u_ref, w1_ref, w2_ref, h_ref, o_ref):
    f = pl.program_id(1)
    a = jnp.maximum(jnp.dot(u_ref[...], w1_ref[...], preferred_element_type=F32), 0.0)
    part = jnp.dot((a * a).astype(BF16), w2_ref[...], preferred_element_type=F32)

    @pl.when(f == 0)
    def _():
        o_ref[...] = h_ref[...] + part

    @pl.when(f > 0)
    def _():
        o_ref[...] += part


def _mlp(u, w1, w2, h, tm=512, tf=512):
    m, d = h.shape
    dff = w1.shape[1]
    return pl.pallas_call(
        _mlp_kernel,
        grid=(m // tm, dff // tf),
        in_specs=[pl.BlockSpec((tm, d), lambda i, f: (i, 0)),
                  pl.BlockSpec((d, tf), lambda i, f: (0, f)),
                  pl.BlockSpec((tf, d), lambda i, f: (f, 0)),
                  pl.BlockSpec((tm, d), lambda i, f: (i, 0))],
        out_specs=pl.BlockSpec((tm, d), lambda i, f: (i, 0)),
        out_shape=jax.ShapeDtypeStruct((m, d), F32),
        compiler_params=_params(("parallel", "arbitrary")),
        name="mlp",
    )(u, w1, w2, h)


def _ple_kernel(last, h_ref, p_ref, wg_ref, wp_ref, gp_ref, gn_ref, *out_refs):
    h = h_ref[...]
    u = _rms(h, gp_ref[...]).astype(BF16)
    zg = jnp.dot(u, wg_ref[...], preferred_element_type=F32)
    gate = 1.0 / (1.0 + jnp.exp(-zg))
    pe = jnp.dot(p_ref[...].astype(BF16), wp_ref[...], preferred_element_type=F32)
    h = h + pe * gate
    nxt = _rms(h, gn_ref[...])
    if last:
        out_refs[0][...] = nxt
    else:
        out_refs[0][...] = h
        out_refs[1][...] = nxt.astype(out_refs[1].dtype)


def _ple(h, p, w_pg, w_pp, g_ple, g_next, last, tm=512):
    m, d = h.shape
    pd = p.shape[1]
    row = lambda i: (i, 0)
    fixed = lambda i: (0, 0)
    if last:
        out_specs = [pl.BlockSpec((tm, d), row)]
        out_shape = [jax.ShapeDtypeStruct((m, d), F32)]
    else:
        out_specs = [pl.BlockSpec((tm, d), row), pl.BlockSpec((tm, d), row)]
        out_shape = [jax.ShapeDtypeStruct((m, d), F32), jax.ShapeDtypeStruct((m, d), BF16)]
    return pl.pallas_call(
        functools.partial(_ple_kernel, last),
        grid=(m // tm,),
        in_specs=[pl.BlockSpec((tm, d), row), pl.BlockSpec((tm, pd), row),
                  pl.BlockSpec(w_pg.shape, fixed), pl.BlockSpec(w_pp.shape, fixed),
                  pl.BlockSpec((1, d), fixed), pl.BlockSpec((1, d), fixed)],
        out_specs=out_specs,
        out_shape=out_shape,
        compiler_params=_params(("parallel",)),
        name="ple",
    )(h, p, w_pg, w_pp, g_ple, g_next)


def kernel(x, p, positions, norm1, w_in, lb_param, hgrn_norm, attn_norm, w_out, norm2, w1, w2,
           ple_norm, w_pg, w_pp, final_norm):
    batch, seq, d = x.shape
    depth = w_in.shape[0]
    m = batch * seq
    assert seq % ATTN_TILE == 0 and d == 2 * GROUP_WIDTH

    cos, sin = _rope_tables(positions)
    h = x.reshape(m, d)
    u = _norm(h, norm1[0].reshape(1, d), BF16)
    out = None
    for i in range(depth):
        last = i == depth - 1
        z = _in_proj(u, w_in[i].astype(BF16))
        o_hgrn = _hgrn(z, lb_param, hgrn_norm[i].reshape(1, HEAD_DIM), i, batch, seq)
        o_attn = _attn(z, cos, sin, attn_norm[i].reshape(1, HEAD_DIM), batch, seq)
        h, u2 = _out_proj(o_hgrn, o_attn, w_out[i].astype(BF16), h, norm2[i].reshape(1, d))
        h = _mlp(u2, w1[i].astype(BF16), w2[i].astype(BF16), h)
        g_next = final_norm if last else norm1[i + 1]
        res = _ple(h, p[i].reshape(m, -1), w_pg[i].astype(BF16), w_pp[i].astype(BF16),
                   ple_norm[i].reshape(1, d), g_next.reshape(1, d), last)
        if last:
            out = res[0]
        else:
            h, u = res
    return out.reshape(batch, seq, d)
```

```python
import functools

import jax
import jax.numpy as jnp
from jax import lax
from jax.experimental import pallas as pl
from jax.experimental.pallas import tpu as pltpu

F32 = jnp.float32
BF16 = jnp.bfloat16

HEAD_DIM = 128
N_HEADS = 8
GROUP_WIDTH = N_HEADS * HEAD_DIM
ROT_DIM = HEAD_DIM // 4
ROPE_THETA = 500000.0
NORM_EPS = 1e-6
HGRN_CHUNK = 64
HGRN_SUB = 16
HGRN_MAX_HALF_DECAY = 60.0
ATTN_BLOCK = 128
ATTN_TILE = 2048
DILATIONS = (1, 4, 16)
VMEM_LIMIT = 56 * 1024 * 1024

NT_DIMS = (((1,), (1,)), ((), ()))
TN_DIMS = (((0,), (0,)), ((), ()))


def _params(semantics):
    return pltpu.CompilerParams(dimension_semantics=semantics, vmem_limit_bytes=VMEM_LIMIT)


def _rms(x, gain):
    ms = jnp.mean(x * x, axis=-1, keepdims=True)
    return x * lax.rsqrt(ms + NORM_EPS) * gain


def _silu(x):
    return x / (1.0 + jnp.exp(-x))


def _norm_kernel(x_ref, g_ref, o_ref):
    o_ref[...] = _rms(x_ref[...], g_ref[...]).astype(o_ref.dtype)


def _norm(x, gain, out_dtype, tm=512):
    m, d = x.shape
    return pl.pallas_call(
        _norm_kernel,
        grid=(m // tm,),
        in_specs=[pl.BlockSpec((tm, d), lambda i: (i, 0)),
                  pl.BlockSpec((1, d), lambda i: (0, 0))],
        out_specs=pl.BlockSpec((tm, d), lambda i: (i, 0)),
        out_shape=jax.ShapeDtypeStruct((m, d), out_dtype),
        compiler_params=_params(("parallel",)),
        name="rmsnorm",
    )(x, gain)


def _rope_kernel(pos_ref, invf_ref, cos_ref, sin_ref):
    ang = pos_ref[...] * invf_ref[...]
    lane = lax.broadcasted_iota(jnp.int32, ang.shape, 1)
    s = jnp.sin(ang)
    cos_ref[...] = jnp.cos(ang)
    sin_ref[...] = jnp.where(lane < ROT_DIM // 2, -s, s)


def _rope_tables(positions, tr=1024):
    n = positions.size
    pos = positions.astype(F32).reshape(n, 1)
    half = ROT_DIM // 2
    inv = 1.0 / (ROPE_THETA ** (jnp.arange(0, ROT_DIM, 2, dtype=F32) / ROT_DIM))
    invf = jnp.concatenate([inv, inv, jnp.zeros((HEAD_DIM - 2 * half,), F32)]).reshape(1, HEAD_DIM)
    return pl.pallas_call(
        _rope_kernel,
        grid=(n // tr,),
        in_specs=[pl.BlockSpec((tr, 1), lambda i: (i, 0)),
                  pl.BlockSpec((1, HEAD_DIM), lambda i: (0, 0))],
        out_specs=[pl.BlockSpec((tr, HEAD_DIM), lambda i: (i, 0))] * 2,
        out_shape=[jax.ShapeDtypeStruct((n, HEAD_DIM), F32)] * 2,
        compiler_params=_params(("parallel",)),
        name="rope_tables",
    )(pos, invf)


def _matmul_kernel(x_ref, w_ref, o_ref):
    o_ref[...] = jnp.dot(x_ref[...], w_ref[...], preferred_element_type=F32).astype(o_ref.dtype)


def _in_proj(u, w, tm=1024, tn=1024):
    m, k = u.shape
    n = w.shape[1]
    return pl.pallas_call(
        _matmul_kernel,
        grid=(m // tm, n // tn),
        in_specs=[pl.BlockSpec((tm, k), lambda i, j: (i, 0)),
                  pl.BlockSpec((k, tn), lambda i, j: (0, j))],
        out_specs=pl.BlockSpec((tm, tn), lambda i, j: (i, j)),
        out_shape=jax.ShapeDtypeStruct((m, n), F32),
        compiler_params=_params(("parallel", "parallel")),
        name="in_proj",
    )(u, w)


def _hgrn_kernel(layer, zq_ref, zf_ref, zi_ref, zg_ref, lbp_ref, gn_ref, o_ref,
                 st_ref, q_s, k_s, b_s):
    C, SB = HGRN_CHUNK, HGRN_SUB
    tb = zq_ref.shape[0]
    n_chunks = tb // C
    mid = C // 2 - 1

    @pl.when(pl.program_id(2) == 0)
    def _():
        st_ref[...] = jnp.zeros_like(st_ref)

    lbp = lbp_ref[...]
    e = jnp.exp(lbp - jnp.max(lbp, axis=0, keepdims=True))
    sm = e / jnp.sum(e, axis=0, keepdims=True)
    lb = jnp.zeros((1, HEAD_DIM), F32)
    for j in range(1, layer + 1):
        lb = lb + sm[j:j + 1, :]
    lb_pos = lb > 0.0
    log_lb = jnp.log(jnp.where(lb_pos, lb, 1.0))
    log1m_lb = jnp.log1p(-lb)
    one_m_lb = 1.0 - lb
    gn = gn_ref[...]

    row = lax.broadcasted_iota(jnp.int32, (C, C), 0)
    col = lax.broadcasted_iota(jnp.int32, (C, C), 1)
    causal = row >= col
    tril = jnp.where(causal, 1.0, 0.0).astype(BF16)

    worst = jnp.zeros((1, HEAD_DIM), F32)
    for c in range(n_chunks):
        rows = pl.ds(c * C, C)
        zf = zf_ref[rows, :]
        en = jnp.exp(-jnp.abs(zf))
        rcp = 1.0 / (1.0 + en)
        sig_neg = jnp.where(zf >= 0.0, en * rcp, rcp)
        log_sig = jnp.minimum(zf, 0.0) - jnp.log1p(en)
        t2 = log1m_lb + log_sig
        if layer > 0:
            mx = jnp.maximum(log_lb, t2)
            lae = mx + jnp.log1p(jnp.exp(-jnp.abs(log_lb - t2)))
            lf = jnp.where(lb_pos, lae, t2)
        else:
            lf = t2
        hi = lf.astype(BF16)
        lo = (lf - hi.astype(F32)).astype(BF16)
        bb = jnp.dot(tril, jnp.concatenate([hi, lo], axis=1), preferred_element_type=F32)
        b = bb[:, :HEAD_DIM] + bb[:, HEAD_DIM:]
        q_s[rows, :] = _silu(zq_ref[rows, :])
        k_s[rows, :] = one_m_lb * sig_neg
        b_s[rows, :] = b
        worst = jnp.maximum(worst, jnp.maximum(-b[mid:mid + 1, :],
                                               b[mid:mid + 1, :] - b[C - 1:C, :]))
    factorisable = jnp.max(worst) <= HGRN_MAX_HALF_DECAY

    def finish(o, rows):
        o = _rms(o, gn) * _silu(zg_ref[rows, :])
        o_ref[rows, :] = o.astype(o_ref.dtype)

    def state_step(st, vb, kk, b):
        b_end = b[C - 1:C, :]
        kd = (kk * jnp.exp(b_end - b)).astype(BF16)
        return st * jnp.exp(b_end) + lax.dot_general(vb, kd, TN_DIMS, preferred_element_type=F32)

    @pl.when(factorisable)
    def _():
        st = st_ref[...]
        for c in range(n_chunks):
            rows = pl.ds(c * C, C)
            q, kk, b = q_s[rows, :], k_s[rows, :], b_s[rows, :]
            vb = zi_ref[rows, :].astype(BF16)
            r = b[mid:mid + 1, :]
            qm = (q * jnp.exp(b - r)).astype(BF16)
            km = (kk * jnp.exp(r - b)).astype(BF16)
            a = lax.dot_general(qm, km, NT_DIMS, preferred_element_type=F32)
            a = jnp.where(causal, a, 0.0).astype(BF16)
            qe = (q * jnp.exp(b)).astype(BF16)
            o = (jnp.dot(a, vb, preferred_element_type=F32)
                 + lax.dot_general(qe, st.astype(BF16), NT_DIMS, preferred_element_type=F32))
            st = state_step(st, vb, kk, b)
            finish(o, rows)
        st_ref[...] = st

    @pl.when(jnp.logical_not(factorisable))
    def _():
        ones = jnp.ones((HEAD_DIM, HEAD_DIM), BF16)
        sub_t = lax.broadcasted_iota(jnp.int32, (SB, HEAD_DIM), 0)

        def chunk(c, carry):
            r0 = pl.multiple_of(c * C, C)
            rows = pl.ds(r0, C)
            q, kk, b = q_s[rows, :], k_s[rows, :], b_s[rows, :]
            v = zi_ref[rows, :]
            vb = v.astype(BF16)
            st = st_ref[...]
            o_inter = lax.dot_general((q * jnp.exp(b)).astype(BF16), st.astype(BF16), NT_DIMS,
                                      preferred_element_type=F32)
            outs = []
            for i in range(C // SB):
                lo_r = i * SB
                q_i = q[lo_r:lo_r + SB, :]
                b_i = b[lo_r:lo_r + SB, :]
                o_i = o_inter[lo_r:lo_r + SB, :]
                if i > 0:
                    bref = b[lo_r - 1:lo_r, :]
                    qs = (q_i * jnp.exp(b_i - bref)).astype(BF16)
                    ks = (kk[:lo_r, :] * jnp.exp(bref - b[:lo_r, :])).astype(BF16)
                    a = lax.dot_general(qs, ks, NT_DIMS, preferred_element_type=F32)
                    o_i = o_i + jnp.dot(a.astype(BF16), vb[:lo_r, :], preferred_element_type=F32)
                slabs = []
                for s in range(SB):
                    b_row = b_s[pl.ds(r0 + lo_r + s, 1), :]
                    k_row = k_s[pl.ds(r0 + lo_r + s, 1), :]
                    w = q_i * k_row * jnp.exp(jnp.minimum(b_i - b_row, 0.0))
                    slabs.append(jnp.where(sub_t >= s, w, 0.0).astype(BF16))
                red = jnp.dot(jnp.concatenate(slabs, axis=0), ones, preferred_element_type=F32)
                for s in range(SB):
                    o_i = o_i + red[s * SB:(s + 1) * SB, :] * v[lo_r + s:lo_r + s + 1, :]
                outs.append(o_i)
            st_ref[...] = state_step(st, vb, kk, b)
            finish(jnp.concatenate(outs, axis=0), rows)
            return carry

        lax.fori_loop(0, n_chunks, chunk, 0)


def _hgrn(z, lb_param, g_norm, layer, batch, seq, tb=512):
    nt = seq // tb
    nh = N_HEADS

    def col(off):
        return pl.BlockSpec((tb, HEAD_DIM), lambda b, h, t, off=off: (b * nt + t, off + h))

    return pl.pallas_call(
        functools.partial(_hgrn_kernel, layer),
        grid=(batch, nh, nt),
        in_specs=[col(0), col(nh), col(2 * nh), col(3 * nh),
                  pl.BlockSpec((lb_param.shape[0], HEAD_DIM), lambda b, h, t: (0, h)),
                  pl.BlockSpec((1, HEAD_DIM), lambda b, h, t: (0, 0))],
        out_specs=pl.BlockSpec((tb, HEAD_DIM), lambda b, h, t: (b * nt + t, h)),
        out_shape=jax.ShapeDtypeStruct((batch * seq, GROUP_WIDTH), BF16),
        scratch_shapes=[pltpu.VMEM((HEAD_DIM, HEAD_DIM), F32),
                        pltpu.VMEM((tb, HEAD_DIM), F32),
                        pltpu.VMEM((tb, HEAD_DIM), F32),
                        pltpu.VMEM((tb, HEAD_DIM), F32)],
        compiler_params=_params(("parallel", "parallel", "arbitrary")),
        name="hgrn2",
    )(z, z, z, z, lb_param, g_norm)


def _rotary(x, cos, sin_signed, lane):
    partner = jnp.where(lane < ROT_DIM // 2,
                        pltpu.roll(x, HEAD_DIM - ROT_DIM // 2, axis=1),
                        pltpu.roll(x, ROT_DIM // 2, axis=1))
    return x * cos + partner * sin_signed


def _attn_kernel(zq_ref, zk_ref, zv_ref, cos_ref, sin_ref, gn_ref, o_ref,
                 q_buf, k_buf, v_buf, o_buf, c_buf):
    TQ, Q = ATTN_TILE, ATTN_BLOCK
    t = pl.program_id(2)
    RB = 256
    lane = lax.broadcasted_iota(jnp.int32, (RB, HEAD_DIM), 1)

    @pl.when(t == 0)
    def _():
        k_buf[pl.ds(0, TQ), :] = jnp.zeros((TQ, HEAD_DIM), F32)
        v_buf[pl.ds(0, TQ), :] = jnp.zeros((TQ, HEAD_DIM), F32)

    scale = HEAD_DIM ** -0.5
    for r in range(TQ // RB):
        rows = pl.ds(r * RB, RB)
        c, s = cos_ref[rows, :], sin_ref[rows, :]
        q_buf[rows, :] = _rotary(zq_ref[rows, :], c, s, lane) * scale
        k_buf[pl.ds(TQ + r * RB, RB), :] = _rotary(zk_ref[rows, :], c, s, lane)
        v_buf[pl.ds(TQ + r * RB, RB), :] = zv_ref[rows, :]

    qi = lax.broadcasted_iota(jnp.int32, (Q, Q), 0)
    kj = lax.broadcasted_iota(jnp.int32, (Q, Q), 1)
    band_prev = kj >= qi
    band_cur = kj <= qi
    has_prev_tile = t > 0
    neg = -jnp.inf
    gn = gn_ref[...]

    def block(d, res, nb):
        cur = TQ + res + d * Q * nb
        prev = cur - d * Q

        def rows_of(start):
            return pl.ds(start, Q, stride=d) if d > 1 else pl.ds(start, Q)

        qb = q_buf[rows_of(cur - TQ), :].astype(BF16)
        kp = k_buf[rows_of(prev), :].astype(BF16)
        kc = k_buf[rows_of(cur), :].astype(BF16)
        vp = v_buf[rows_of(prev), :].astype(BF16)
        vc = v_buf[rows_of(cur), :].astype(BF16)
        sp = lax.dot_general(qb, kp, NT_DIMS, preferred_element_type=F32)
        sc = lax.dot_general(qb, kc, NT_DIMS, preferred_element_type=F32)
        ok_prev = band_prev if nb > 0 else jnp.logical_and(band_prev, has_prev_tile)
        sp = jnp.where(ok_prev, sp, neg)
        sc = jnp.where(band_cur, sc, neg)
        m = jnp.max(jnp.maximum(sp, sc), axis=-1, keepdims=True)
        pp = jnp.exp(sp - m)
        pc = jnp.exp(sc - m)
        l = jnp.sum(pp + pc, axis=-1, keepdims=True)
        acc = (jnp.dot(pp.astype(BF16), vp, preferred_element_type=F32)
               + jnp.dot(pc.astype(BF16), vc, preferred_element_type=F32))
        return acc * (1.0 / l), m + jnp.log(l), rows_of(cur - TQ)

    strided = [d for d in DILATIONS if d > 1]
    for g, d in enumerate(strided):
        for res in range(d):
            for nb in range(TQ // (Q * d)):
                o, lse, out_rows = block(d, res, nb)
                o_buf[g, out_rows, :] = o
                c_buf[g, out_rows, :] = jnp.broadcast_to(lse, (Q, HEAD_DIM))

    for nb in range(TQ // Q):
        o, lse, out_rows = block(1, 0, nb)
        cs = [c_buf[g, out_rows, :] for g in range(len(strided))]
        c_max = functools.reduce(jnp.maximum, cs, lse)
        w = jnp.exp(lse - c_max)
        num = w * o
        den = w
        for g in range(len(strided)):
            w = jnp.exp(cs[g] - c_max)
            num = num + w * o_buf[g, out_rows, :]
            den = den + w
        o_ref[out_rows, :] = _rms(num / den, gn).astype(o_ref.dtype)

    k_buf[pl.ds(0, TQ), :] = k_buf[pl.ds(TQ, TQ), :]
    v_buf[pl.ds(0, TQ), :] = v_buf[pl.ds(TQ, TQ), :]


def _attn(z, cos, sin, g_norm, batch, seq):
    TQ = ATTN_TILE
    nt = seq // TQ
    nh = N_HEADS
    n_strided = len(DILATIONS) - 1

    def col(off):
        return pl.BlockSpec((TQ, HEAD_DIM), lambda b, h, t, off=off: (b * nt + t, off + h))

    tab = pl.BlockSpec((TQ, HEAD_DIM), lambda b, h, t: (b * nt + t, 0))
    return pl.pallas_call(
        _attn_kernel,
        grid=(batch, nh, nt),
        in_specs=[col(4 * nh), col(5 * nh), col(6 * nh), tab, tab,
                  pl.BlockSpec((1, HEAD_DIM), lambda b, h, t: (0, 0))],
        out_specs=pl.BlockSpec((TQ, HEAD_DIM), lambda b, h, t: (b * nt + t, h)),
        out_shape=jax.ShapeDtypeStruct((batch * seq, GROUP_WIDTH), BF16),
        scratch_shapes=[pltpu.VMEM((TQ, HEAD_DIM), F32),
                        pltpu.VMEM((2 * TQ, HEAD_DIM), F32),
                        pltpu.VMEM((2 * TQ, HEAD_DIM), F32),
                        pltpu.VMEM((n_strided, TQ, HEAD_DIM), F32),
                        pltpu.VMEM((n_strided, TQ, HEAD_DIM), F32)],
        compiler_params=_params(("parallel", "parallel", "arbitrary")),
        name="dilated_attn",
    )(z, z, z, cos, sin, g_norm)


def _out_proj_kernel(oh_ref, oa_ref, w_ref, h_ref, g_ref, h_out_ref, u_out_ref):
    half = oh_ref.shape[1]
    acc = jnp.dot(oh_ref[...], w_ref[pl.ds(0, half), :], preferred_element_type=F32)
    acc = acc + jnp.dot(oa_ref[...], w_ref[pl.ds(half, half), :], preferred_element_type=F32)
    h = h_ref[...] + acc
    h_out_ref[...] = h
    u_out_ref[...] = _rms(h, g_ref[...]).astype(u_out_ref.dtype)


def _out_proj(oh, oa, w, h, gain, tm=512):
    m, d = h.shape
    half = oh.shape[1]
    row = lambda i: (i, 0)
    fixed = lambda i: (0, 0)
    return pl.pallas_call(
        _out_proj_kernel,
        grid=(m // tm,),
        in_specs=[pl.BlockSpec((tm, half), row), pl.BlockSpec((tm, half), row),
                  pl.BlockSpec(w.shape, fixed), pl.BlockSpec((tm, d), row),
                  pl.BlockSpec((1, d), fixed)],
        out_specs=[pl.BlockSpec((tm, d), row), pl.BlockSpec((tm, d), row)],
        out_shape=[jax.ShapeDtypeStruct((m, d), F32), jax.ShapeDtypeStruct((m, d), BF16)],
        compiler_params=_params(("parallel",)),
        name="out_proj",
    )(oh, oa, w, h, gain)


def _mlp_kernel(u_ref, w1_ref, w2_ref, h_ref, o_ref):
    f = pl.program_id(1)
    a = jnp.maximum(jnp.dot(u_ref[...], w1_ref[...], preferred_element_type=F32), 0.0)
    part = jnp.dot((a * a).astype(BF16), w2_ref[...], preferred_element_type=F32)

    @pl.when(f == 0)
    def _():
        o_ref[...] = h_ref[...] + part

    @pl.when(f > 0)
    def _():
        o_ref[...] += part


def _mlp(u, w1, w2, h, tm=1024, tf=512):
    m, d = h.shape
    dff = w1.shape[1]
    return pl.pallas_call(
        _mlp_kernel,
        grid=(m // tm, dff // tf),
        in_specs=[pl.BlockSpec((tm, d), lambda i, f: (i, 0)),
                  pl.BlockSpec((d, tf), lambda i, f: (0, f)),
                  pl.BlockSpec((tf, d), lambda i, f: (f, 0)),
                  pl.BlockSpec((tm, d), lambda i, f: (i, 0), pipeline_mode=pl.Buffered(1))],
        out_specs=pl.BlockSpec((tm, d), lambda i, f: (i, 0)),
        out_shape=jax.ShapeDtypeStruct((m, d), F32),
        compiler_params=_params(("parallel", "arbitrary")),
        name="mlp",
    )(u, w1, w2, h)


def _ple_kernel(last, h_ref, p_ref, wg_ref, wp_ref, gp_ref, gn_ref, *out_refs):
    h = h_ref[...]
    u = _rms(h, gp_ref[...]).astype(BF16)
    zg = jnp.dot(u, wg_ref[...], preferred_element_type=F32)
    gate = 1.0 / (1.0 + jnp.exp(-zg))
    pe = jnp.dot(p_ref[...].astype(BF16), wp_ref[...], preferred_element_type=F32)
    h = h + pe * gate
    nxt = _rms(h, gn_ref[...])
    if last:
        out_refs[0][...] = nxt
    else:
        out_refs[0][...] = h
        out_refs[1][...] = nxt.astype(out_refs[1].dtype)


def _ple(h, p, w_pg, w_pp, g_ple, g_next, last, tm=512):
    m, d = h.shape
    pd = p.shape[1]
    row = lambda i: (i, 0)
    fixed = lambda i: (0, 0)
    if last:
        out_specs = [pl.BlockSpec((tm, d), row)]
        out_shape = [jax.ShapeDtypeStruct((m, d), F32)]
    else:
        out_specs = [pl.BlockSpec((tm, d), row), pl.BlockSpec((tm, d), row)]
        out_shape = [jax.ShapeDtypeStruct((m, d), F32), jax.ShapeDtypeStruct((m, d), BF16)]
    return pl.pallas_call(
        functools.partial(_ple_kernel, last),
        grid=(m // tm,),
        in_specs=[pl.BlockSpec((tm, d), row), pl.BlockSpec((tm, pd), row),
                  pl.BlockSpec(w_pg.shape, fixed), pl.BlockSpec(w_pp.shape, fixed),
                  pl.BlockSpec((1, d), fixed), pl.BlockSpec((1, d), fixed)],
        out_specs=out_specs,
        out_shape=out_shape,
        compiler_params=_params(("parallel",)),
        name="ple",
    )(h, p, w_pg, w_pp, g_ple, g_next)


def kernel(x, p, positions, norm1, w_in, lb_param, hgrn_norm, attn_norm, w_out, norm2, w1, w2,
           ple_norm, w_pg, w_pp, final_norm):
    batch, seq, d = x.shape
    depth = w_in.shape[0]
    m = batch * seq
    assert seq % ATTN_TILE == 0 and d == 2 * GROUP_WIDTH

    cos, sin = _rope_tables(positions)
    h = x.reshape(m, d)
    u = _norm(h, norm1[0].reshape(1, d), BF16)
    out = None
    for i in range(depth):
        last = i == depth - 1
        z = _in_proj(u, w_in[i].astype(BF16))
        o_hgrn = _hgrn(z, lb_param, hgrn_norm[i].reshape(1, HEAD_DIM), i, batch, seq)
        o_attn = _attn(z, cos, sin, attn_norm[i].reshape(1, HEAD_DIM), batch, seq)
        h, u2 = _out_proj(o_hgrn, o_attn, w_out[i].astype(BF16), h, norm2[i].reshape(1, d))
        h = _mlp(u2, w1[i].astype(BF16), w2[i].astype(BF16), h)
        g_next = final_norm if last else norm1[i + 1]
        res = _ple(h, p[i].reshape(m, -1), w_pg[i].astype(BF16), w_pp[i].astype(BF16),
                   ple_norm[i].reshape(1, d), g_next.reshape(1, d), last)
        if last:
            out = res[0]
        else:
            h, u = res
    return out.reshape(batch, seq, d)
```

```python
import functools

import jax
import jax.numpy as jnp
from jax import lax
from jax.experimental import pallas as pl
from jax.experimental.pallas import tpu as pltpu

F32 = jnp.float32
BF16 = jnp.bfloat16

HEAD_DIM = 128
N_HEADS = 8
GROUP_WIDTH = N_HEADS * HEAD_DIM
ROT_DIM = HEAD_DIM // 4
ROPE_THETA = 500000.0
NORM_EPS = 1e-6
HGRN_CHUNK = 64
HGRN_SUB = 16
HGRN_MAX_HALF_DECAY = 60.0
ATTN_BLOCK = 128
ATTN_TILE = 2048
ATTN_GROUP = 4
DILATIONS = (1, 4, 16)
VMEM_LIMIT = 56 * 1024 * 1024

NT_DIMS = (((1,), (1,)), ((), ()))
TN_DIMS = (((0,), (0,)), ((), ()))


def _params(semantics):
    return pltpu.CompilerParams(dimension_semantics=semantics, vmem_limit_bytes=VMEM_LIMIT)


def _rms(x, gain):
    ms = jnp.mean(x * x, axis=-1, keepdims=True)
    return x * lax.rsqrt(ms + NORM_EPS) * gain


def _silu(x):
    return x / (1.0 + jnp.exp(-x))


def _norm_kernel(x_ref, g_ref, o_ref):
    o_ref[...] = _rms(x_ref[...], g_ref[...]).astype(o_ref.dtype)


def _norm(x, gain, out_dtype, tm=512):
    m, d = x.shape
    return pl.pallas_call(
        _norm_kernel,
        grid=(m // tm,),
        in_specs=[pl.BlockSpec((tm, d), lambda i: (i, 0)),
                  pl.BlockSpec((1, d), lambda i: (0, 0))],
        out_specs=pl.BlockSpec((tm, d), lambda i: (i, 0)),
        out_shape=jax.ShapeDtypeStruct((m, d), out_dtype),
        compiler_params=_params(("parallel",)),
        name="rmsnorm",
    )(x, gain)


def _rope_kernel(pos_ref, invf_ref, cos_ref, sin_ref):
    ang = pos_ref[...] * invf_ref[...]
    lane = lax.broadcasted_iota(jnp.int32, ang.shape, 1)
    s = jnp.sin(ang)
    cos_ref[...] = jnp.cos(ang)
    sin_ref[...] = jnp.where(lane < ROT_DIM // 2, -s, s)


def _rope_tables(positions, tr=1024):
    n = positions.size
    pos = positions.astype(F32).reshape(n, 1)
    half = ROT_DIM // 2
    inv = 1.0 / (ROPE_THETA ** (jnp.arange(0, ROT_DIM, 2, dtype=F32) / ROT_DIM))
    invf = jnp.concatenate([inv, inv, jnp.zeros((HEAD_DIM - 2 * half,), F32)]).reshape(1, HEAD_DIM)
    return pl.pallas_call(
        _rope_kernel,
        grid=(n // tr,),
        in_specs=[pl.BlockSpec((tr, 1), lambda i: (i, 0)),
                  pl.BlockSpec((1, HEAD_DIM), lambda i: (0, 0))],
        out_specs=[pl.BlockSpec((tr, HEAD_DIM), lambda i: (i, 0))] * 2,
        out_shape=[jax.ShapeDtypeStruct((n, HEAD_DIM), F32)] * 2,
        compiler_params=_params(("parallel",)),
        name="rope_tables",
    )(pos, invf)


def _matmul_kernel(x_ref, w_ref, o_ref):
    o_ref[...] = jnp.dot(x_ref[...], w_ref[...], preferred_element_type=F32).astype(o_ref.dtype)


def _in_proj(u, w, tm=1024, tn=1024):
    m, k = u.shape
    n = w.shape[1]
    return pl.pallas_call(
        _matmul_kernel,
        grid=(m // tm, n // tn),
        in_specs=[pl.BlockSpec((tm, k), lambda i, j: (i, 0)),
                  pl.BlockSpec((k, tn), lambda i, j: (0, j))],
        out_specs=pl.BlockSpec((tm, tn), lambda i, j: (i, j)),
        out_shape=jax.ShapeDtypeStruct((m, n), F32),
        compiler_params=_params(("parallel", "parallel")),
        name="in_proj",
    )(u, w)


def _hgrn_kernel(layer, zq_ref, zf_ref, zi_ref, zg_ref, lbp_ref, gn_ref, o_ref,
                 st_ref, q_s, k_s, b_s):
    C, SB = HGRN_CHUNK, HGRN_SUB
    tb = zq_ref.shape[0]
    n_chunks = tb // C
    mid = C // 2 - 1

    @pl.when(pl.program_id(2) == 0)
    def _():
        st_ref[...] = jnp.zeros_like(st_ref)

    lbp = lbp_ref[...]
    e = jnp.exp(lbp - jnp.max(lbp, axis=0, keepdims=True))
    sm = e / jnp.sum(e, axis=0, keepdims=True)
    lb = jnp.zeros((1, HEAD_DIM), F32)
    for j in range(1, layer + 1):
        lb = lb + sm[j:j + 1, :]
    lb_pos = lb > 0.0
    log_lb = jnp.log(jnp.where(lb_pos, lb, 1.0))
    log1m_lb = jnp.log1p(-lb)
    one_m_lb = 1.0 - lb
    gn = gn_ref[...]

    row = lax.broadcasted_iota(jnp.int32, (C, C), 0)
    col = lax.broadcasted_iota(jnp.int32, (C, C), 1)
    causal = row >= col
    tril = jnp.where(causal, 1.0, 0.0).astype(BF16)

    worst = jnp.zeros((1, HEAD_DIM), F32)
    for c in range(n_chunks):
        rows = pl.ds(c * C, C)
        zf = zf_ref[rows, :]
        en = jnp.exp(-jnp.abs(zf))
        rcp = 1.0 / (1.0 + en)
        sig_neg = jnp.where(zf >= 0.0, en * rcp, rcp)
        log_sig = jnp.minimum(zf, 0.0) - jnp.log1p(en)
        t2 = log1m_lb + log_sig
        if layer > 0:
            mx = jnp.maximum(log_lb, t2)
            lae = mx + jnp.log1p(jnp.exp(-jnp.abs(log_lb - t2)))
            lf = jnp.where(lb_pos, lae, t2)
        else:
            lf = t2
        hi = lf.astype(BF16)
        lo = (lf - hi.astype(F32)).astype(BF16)
        bb = jnp.dot(tril, jnp.concatenate([hi, lo], axis=1), preferred_element_type=F32)
        b = bb[:, :HEAD_DIM] + bb[:, HEAD_DIM:]
        q_s[rows, :] = _silu(zq_ref[rows, :])
        k_s[rows, :] = one_m_lb * sig_neg
        b_s[rows, :] = b
        worst = jnp.maximum(worst, jnp.maximum(-b[mid:mid + 1, :],
                                               b[mid:mid + 1, :] - b[C - 1:C, :]))
    factorisable = jnp.max(worst) <= HGRN_MAX_HALF_DECAY

    def finish(o, rows):
        o = _rms(o, gn) * _silu(zg_ref[rows, :])
        o_ref[rows, :] = o.astype(o_ref.dtype)

    def state_step(st, vb, kk, b):
        b_end = b[C - 1:C, :]
        kd = (kk * jnp.exp(b_end - b)).astype(BF16)
        return st * jnp.exp(b_end) + lax.dot_general(vb, kd, TN_DIMS, preferred_element_type=F32)

    @pl.when(factorisable)
    def _():
        st = st_ref[...]
        for c in range(n_chunks):
            rows = pl.ds(c * C, C)
            q, kk, b = q_s[rows, :], k_s[rows, :], b_s[rows, :]
            vb = zi_ref[rows, :].astype(BF16)
            r = b[mid:mid + 1, :]
            qm = (q * jnp.exp(b - r)).astype(BF16)
            km = (kk * jnp.exp(r - b)).astype(BF16)
            a = lax.dot_general(qm, km, NT_DIMS, preferred_element_type=F32)
            a = jnp.where(causal, a, 0.0).astype(BF16)
            qe = (q * jnp.exp(b)).astype(BF16)
            o = (jnp.dot(a, vb, preferred_element_type=F32)
                 + lax.dot_general(qe, st.astype(BF16), NT_DIMS, preferred_element_type=F32))
            st = state_step(st, vb, kk, b)
            finish(o, rows)
        st_ref[...] = st

    @pl.when(jnp.logical_not(factorisable))
    def _():
        ones = jnp.ones((HEAD_DIM, HEAD_DIM), BF16)
        sub_t = lax.broadcasted_iota(jnp.int32, (SB, HEAD_DIM), 0)

        def chunk(c, carry):
            r0 = pl.multiple_of(c * C, C)
            rows = pl.ds(r0, C)
            q, kk, b = q_s[rows, :], k_s[rows, :], b_s[rows, :]
            v = zi_ref[rows, :]
            vb = v.astype(BF16)
            st = st_ref[...]
            o_inter = lax.dot_general((q * jnp.exp(b)).astype(BF16), st.astype(BF16), NT_DIMS,
                                      preferred_element_type=F32)
            outs = []
            for i in range(C // SB):
                lo_r = i * SB
                q_i = q[lo_r:lo_r + SB, :]
                b_i = b[lo_r:lo_r + SB, :]
                o_i = o_inter[lo_r:lo_r + SB, :]
                if i > 0:
                    bref = b[lo_r - 1:lo_r, :]
                    qs = (q_i * jnp.exp(b_i - bref)).astype(BF16)
                    ks = (kk[:lo_r, :] * jnp.exp(bref - b[:lo_r, :])).astype(BF16)
                    a = lax.dot_general(qs, ks, NT_DIMS, preferred_element_type=F32)
                    o_i = o_i + jnp.dot(a.astype(BF16), vb[:lo_r, :], preferred_element_type=F32)
                slabs = []
                for s in range(SB):
                    b_row = b_s[pl.ds(r0 + lo_r + s, 1), :]
                    k_row = k_s[pl.ds(r0 + lo_r + s, 1), :]
                    w = q_i * k_row * jnp.exp(jnp.minimum(b_i - b_row, 0.0))
                    slabs.append(jnp.where(sub_t >= s, w, 0.0).astype(BF16))
                red = jnp.dot(jnp.concatenate(slabs, axis=0), ones, preferred_element_type=F32)
                for s in range(SB):
                    o_i = o_i + red[s * SB:(s + 1) * SB, :] * v[lo_r + s:lo_r + s + 1, :]
                outs.append(o_i)
            st_ref[...] = state_step(st, vb, kk, b)
            finish(jnp.concatenate(outs, axis=0), rows)
            return carry

        lax.fori_loop(0, n_chunks, chunk, 0)


def _hgrn(z, lb_param, g_norm, layer, batch, seq, tb=1024):
    nt = seq // tb
    nh = N_HEADS

    def col(off):
        return pl.BlockSpec((tb, HEAD_DIM), lambda b, h, t, off=off: (b * nt + t, off + h))

    return pl.pallas_call(
        functools.partial(_hgrn_kernel, layer),
        grid=(batch, nh, nt),
        in_specs=[col(0), col(nh), col(2 * nh), col(3 * nh),
                  pl.BlockSpec((lb_param.shape[0], HEAD_DIM), lambda b, h, t: (0, h)),
                  pl.BlockSpec((1, HEAD_DIM), lambda b, h, t: (0, 0))],
        out_specs=pl.BlockSpec((tb, HEAD_DIM), lambda b, h, t: (b * nt + t, h)),
        out_shape=jax.ShapeDtypeStruct((batch * seq, GROUP_WIDTH), BF16),
        scratch_shapes=[pltpu.VMEM((HEAD_DIM, HEAD_DIM), F32),
                        pltpu.VMEM((tb, HEAD_DIM), F32),
                        pltpu.VMEM((tb, HEAD_DIM), F32),
                        pltpu.VMEM((tb, HEAD_DIM), F32)],
        compiler_params=_params(("parallel", "parallel", "arbitrary")),
        name="hgrn2",
    )(z, z, z, z, lb_param, g_norm)


def _rotary(x, cos, sin_signed, lane):
    partner = jnp.where(lane < ROT_DIM // 2,
                        pltpu.roll(x, HEAD_DIM - ROT_DIM // 2, axis=1),
                        pltpu.roll(x, ROT_DIM // 2, axis=1))
    return x * cos + partner * sin_signed


def _attn_kernel(zq_ref, zk_ref, zv_ref, cos_ref, sin_ref, gn_ref, o_ref,
                 q_buf, k_buf, v_buf, o_buf, c_buf, bias_buf):
    TQ, Q, G = ATTN_TILE, ATTN_BLOCK, ATTN_GROUP
    t = pl.program_id(2)
    RB = 256
    lane = lax.broadcasted_iota(jnp.int32, (RB, HEAD_DIM), 1)

    @pl.when(t == 0)
    def _():
        k_buf[pl.ds(0, TQ), :] = jnp.zeros((TQ, HEAD_DIM), F32)
        v_buf[pl.ds(0, TQ), :] = jnp.zeros((TQ, HEAD_DIM), F32)

    scale = HEAD_DIM ** -0.5
    for r in range(TQ // RB):
        rows = pl.ds(r * RB, RB)
        c, s = cos_ref[rows, :], sin_ref[rows, :]
        q_buf[rows, :] = _rotary(zq_ref[rows, :], c, s, lane) * scale
        k_buf[pl.ds(TQ + r * RB, RB), :] = _rotary(zk_ref[rows, :], c, s, lane)
        v_buf[pl.ds(TQ + r * RB, RB), :] = zv_ref[rows, :]

    qi = lax.broadcasted_iota(jnp.int32, (Q, 2 * Q), 0)
    kj = lax.broadcasted_iota(jnp.int32, (Q, 2 * Q), 1)
    dist = kj - qi
    band = jnp.where((dist >= 0) & (dist <= Q), 0.0, -jnp.inf)
    first_key = jnp.where(t > 0, 0, Q)
    bias_buf[0] = band
    bias_buf[1] = jnp.where(kj >= first_key, band, -jnp.inf)
    ones_v = jnp.ones((2 * Q, HEAD_DIM), BF16)
    gn = gn_ref[...]
    strided = [d for d in DILATIONS if d > 1]
    n_strided = len(strided)

    def run_branch(d, g):
        nb_per_res = TQ // (Q * d)

        def rows_of(start):
            if d > 1:
                return pl.ds(start, Q, stride=d)
            return pl.ds(pl.multiple_of(start, Q), Q)

        def group(it, carry):
            blocks = []
            for j in range(G):
                if nb_per_res >= G:
                    base = it * G
                    res, nb0 = base // nb_per_res, base % nb_per_res
                    first = jnp.where(nb0 == 0, 1, 0) if j == 0 else 0
                    blocks.append((res + d * Q * (nb0 + j), first))
                else:
                    res = it * (G // nb_per_res) + j // nb_per_res
                    nb = j % nb_per_res
                    blocks.append((res + d * Q * nb, 1 if nb == 0 else 0))

            qs = [q_buf[rows_of(r0), :].astype(BF16) for r0, _ in blocks]
            ks = [jnp.concatenate([k_buf[rows_of(TQ + r0 - d * Q), :], k_buf[rows_of(TQ + r0), :]],
                                  axis=0).astype(BF16) for r0, _ in blocks]
            ss = [lax.dot_general(q, k, NT_DIMS, preferred_element_type=F32) + bias_buf[sel]
                  for q, k, (_, sel) in zip(qs, ks, blocks)]
            ms = [jnp.max(s, axis=-1, keepdims=True) for s in ss]
            ps = [jnp.exp(s - m).astype(BF16) for s, m in zip(ss, ms)]
            vs = [jnp.concatenate([v_buf[rows_of(TQ + r0 - d * Q), :], v_buf[rows_of(TQ + r0), :]],
                                  axis=0).astype(BF16) for r0, _ in blocks]
            accs = [jnp.dot(p, jnp.concatenate([v, ones_v], axis=1), preferred_element_type=F32)
                    for p, v in zip(ps, vs)]
            for acc, m, (r0, _) in zip(accs, ms, blocks):
                l = acc[:, HEAD_DIM:]
                o = acc[:, :HEAD_DIM] * (1.0 / l)
                lse = m + jnp.log(l)
                rows = rows_of(r0)
                if g is not None:
                    o_buf[g, rows, :] = o
                    c_buf[g, rows, :] = lse
                else:
                    cs = [c_buf[i, rows, :] for i in range(n_strided)]
                    c_max = functools.reduce(jnp.maximum, cs, lse)
                    w = jnp.exp(lse - c_max)
                    num, den = w * o, w
                    for i in range(n_strided):
                        w = jnp.exp(cs[i] - c_max)
                        num = num + w * o_buf[i, rows, :]
                        den = den + w
                    o_ref[rows, :] = _rms(num / den, gn).astype(o_ref.dtype)
            return carry

        lax.fori_loop(0, TQ // (Q * G), group, 0)

    for g, d in enumerate(strided):
        run_branch(d, g)
    run_branch(1, None)

    k_buf[pl.ds(0, TQ), :] = k_buf[pl.ds(TQ, TQ), :]
    v_buf[pl.ds(0, TQ), :] = v_buf[pl.ds(TQ, TQ), :]


def _attn(z, cos, sin, g_norm, batch, seq):
    TQ = ATTN_TILE
    nt = seq // TQ
    nh = N_HEADS
    n_strided = len(DILATIONS) - 1

    def col(off):
        return pl.BlockSpec((TQ, HEAD_DIM), lambda b, h, t, off=off: (b * nt + t, off + h))

    tab = pl.BlockSpec((TQ, HEAD_DIM), lambda b, h, t: (b * nt + t, 0))
    return pl.pallas_call(
        _attn_kernel,
        grid=(batch, nh, nt),
        in_specs=[col(4 * nh), col(5 * nh), col(6 * nh), tab, tab,
                  pl.BlockSpec((1, HEAD_DIM), lambda b, h, t: (0, 0))],
        out_specs=pl.BlockSpec((TQ, HEAD_DIM), lambda b, h, t: (b * nt + t, h)),
        out_shape=jax.ShapeDtypeStruct((batch * seq, GROUP_WIDTH), BF16),
        scratch_shapes=[pltpu.VMEM((TQ, HEAD_DIM), F32),
                        pltpu.VMEM((2 * TQ, HEAD_DIM), F32),
                        pltpu.VMEM((2 * TQ, HEAD_DIM), F32),
                        pltpu.VMEM((n_strided, TQ, HEAD_DIM), F32),
                        pltpu.VMEM((n_strided, TQ, HEAD_DIM), F32),
                        pltpu.VMEM((2, ATTN_BLOCK, 2 * ATTN_BLOCK), F32)],
        compiler_params=_params(("parallel", "parallel", "arbitrary")),
        name="dilated_attn",
    )(z, z, z, cos, sin, g_norm)


def _out_proj_kernel(oh_ref, oa_ref, w_ref, h_ref, g_ref, h_out_ref, u_out_ref):
    half = oh_ref.shape[1]
    acc = jnp.dot(oh_ref[...], w_ref[pl.ds(0, half), :], preferred_element_type=F32)
    acc = acc + jnp.dot(oa_ref[...], w_ref[pl.ds(half, half), :], preferred_element_type=F32)
    h = h_ref[...] + acc
    h_out_ref[...] = h
    u_out_ref[...] = _rms(h, g_ref[...]).astype(u_out_ref.dtype)


def _out_proj(oh, oa, w, h, gain, tm=512):
    m, d = h.shape
    half = oh.shape[1]
    row = lambda i: (i, 0)
    fixed = lambda i: (0, 0)
    return pl.pallas_call(
        _out_proj_kernel,
        grid=(m // tm,),
        in_specs=[pl.BlockSpec((tm, half), row), pl.BlockSpec((tm, half), row),
                  pl.BlockSpec(w.shape, fixed), pl.BlockSpec((tm, d), row),
                  pl.BlockSpec((1, d), fixed)],
        out_specs=[pl.BlockSpec((tm, d), row), pl.BlockSpec((tm, d), row)],
        out_shape=[jax.ShapeDtypeStruct((m, d), F32), jax.ShapeDtypeStruct((m, d), BF16)],
        compiler_params=_params(("parallel",)),
        name="out_proj",
    )(oh, oa, w, h, gain)


def _mlp_kernel(u_ref, w1_ref, w2_ref, h_ref, o_ref):
    f = pl.program_id(1)

    def step(base_ref):
        a = jnp.maximum(jnp.dot(u_ref[...], w1_ref[...], preferred_element_type=F32), 0.0)
        o_ref[...] = base_ref[...] + jnp.dot((a * a).astype(BF16), w2_ref[...],
                                             preferred_element_type=F32)

    @pl.when(f == 0)
    def _():
        step(h_ref)

    @pl.when(f > 0)
    def _():
        step(o_ref)


def _mlp(u, w1, w2, h, tm=1024, tf=512):
    m, d = h.shape
    dff = w1.shape[1]
    return pl.pallas_call(
        _mlp_kernel,
        grid=(m // tm, dff // tf),
        in_specs=[pl.BlockSpec((tm, d), lambda i, f: (i, 0)),
                  pl.BlockSpec((d, tf), lambda i, f: (0, f)),
                  pl.BlockSpec((tf, d), lambda i, f: (f, 0)),
                  pl.BlockSpec((tm, d), lambda i, f: (i, 0), pipeline_mode=pl.Buffered(1))],
        out_specs=pl.BlockSpec((tm, d), lambda i, f: (i, 0)),
        out_shape=jax.ShapeDtypeStruct((m, d), F32),
        compiler_params=_params(("parallel", "arbitrary")),
        name="mlp",
    )(u, w1, w2, h)


def _ple_kernel(last, h_ref, p_ref, wg_ref, wp_ref, gp_ref, gn_ref, *out_refs):
    h = h_ref[...]
    u = _rms(h, gp_ref[...]).astype(BF16)
    zg = jnp.dot(u, wg_ref[...], preferred_element_type=F32)
    gate = 1.0 / (1.0 + jnp.exp(-zg))
    pe = jnp.dot(p_ref[...].astype(BF16), wp_ref[...], preferred_element_type=F32)
    h = h + pe * gate
    nxt = _rms(h, gn_ref[...])
    if last:
        out_refs[0][...] = nxt
    else:
        out_refs[0][...] = h
        out_refs[1][...] = nxt.astype(out_refs[1].dtype)


def _ple(h, p, w_pg, w_pp, g_ple, g_next, last, tm=512):
    m, d = h.shape
    pd = p.shape[1]
    row = lambda i: (i, 0)
    fixed = lambda i: (0, 0)
    if last:
        out_specs = [pl.BlockSpec((tm, d), row)]
        out_shape = [jax.ShapeDtypeStruct((m, d), F32)]
    else:
        out_specs = [pl.BlockSpec((tm, d), row), pl.BlockSpec((tm, d), row)]
        out_shape = [jax.ShapeDtypeStruct((m, d), F32), jax.ShapeDtypeStruct((m, d), BF16)]
    return pl.pallas_call(
        functools.partial(_ple_kernel, last),
        grid=(m // tm,),
        in_specs=[pl.BlockSpec((tm, d), row), pl.BlockSpec((tm, pd), row),
                  pl.BlockSpec(w_pg.shape, fixed), pl.BlockSpec(w_pp.shape, fixed),
                  pl.BlockSpec((1, d), fixed), pl.BlockSpec((1, d), fixed)],
        out_specs=out_specs,
        out_shape=out_shape,
        compiler_params=_params(("parallel",)),
        name="ple",
    )(h, p, w_pg, w_pp, g_ple, g_next)


def kernel(x, p, positions, norm1, w_in, lb_param, hgrn_norm, attn_norm, w_out, norm2, w1, w2,
           ple_norm, w_pg, w_pp, final_norm):
    batch, seq, d = x.shape
    depth = w_in.shape[0]
    m = batch * seq
    assert seq % ATTN_TILE == 0 and d == 2 * GROUP_WIDTH

    cos, sin = _rope_tables(positions)
    h = x.reshape(m, d)
    u = _norm(h, norm1[0].reshape(1, d), BF16)
    out = None
    for i in range(depth):
        last = i == depth - 1
        z = _in_proj(u, w_in[i].astype(BF16))
        o_hgrn = _hgrn(z, lb_param, hgrn_norm[i].reshape(1, HEAD_DIM), i, batch, seq)
        o_attn = _attn(z, cos, sin, attn_norm[i].reshape(1, HEAD_DIM), batch, seq)
        h, u2 = _out_proj(o_hgrn, o_attn, w_out[i].astype(BF16), h, norm2[i].reshape(1, d))
        h = _mlp(u2, w1[i].astype(BF16), w2[i].astype(BF16), h)
        g_next = final_norm if last else norm1[i + 1]
        res = _ple(h, p[i].reshape(m, -1), w_pg[i].astype(BF16), w_pp[i].astype(BF16),
                   ple_norm[i].reshape(1, d), g_next.reshape(1, d), last)
        if last:
            out = res[0]
        else:
            h, u = res
    return out.reshape(batch, seq, d)
```

```python
import functools

import jax
import jax.numpy as jnp
from jax import lax
from jax.experimental import pallas as pl
from jax.experimental.pallas import tpu as pltpu

F32 = jnp.float32
BF16 = jnp.bfloat16

HEAD_DIM = 128
N_HEADS = 8
GROUP_WIDTH = N_HEADS * HEAD_DIM
ROT_DIM = HEAD_DIM // 4
ROPE_THETA = 500000.0
NORM_EPS = 1e-6
HGRN_CHUNK = 64
HGRN_SUB = 16
HGRN_MAX_HALF_DECAY = 60.0
ATTN_BLOCK = 128
ATTN_TILE = 2048
ATTN_GROUP = 8
DILATIONS = (1, 4, 16)
VMEM_LIMIT = 56 * 1024 * 1024

NT_DIMS = (((1,), (1,)), ((), ()))
TN_DIMS = (((0,), (0,)), ((), ()))


def _params(semantics):
    return pltpu.CompilerParams(dimension_semantics=semantics, vmem_limit_bytes=VMEM_LIMIT)


def _rms(x, gain):
    ms = jnp.mean(x * x, axis=-1, keepdims=True)
    return x * lax.rsqrt(ms + NORM_EPS) * gain


def _silu(x):
    return x / (1.0 + jnp.exp(-x))


def _norm_kernel(x_ref, g_ref, o_ref):
    o_ref[...] = _rms(x_ref[...], g_ref[...]).astype(o_ref.dtype)


def _norm(x, gain, out_dtype, tm=512):
    m, d = x.shape
    return pl.pallas_call(
        _norm_kernel,
        grid=(m // tm,),
        in_specs=[pl.BlockSpec((tm, d), lambda i: (i, 0)),
                  pl.BlockSpec((1, d), lambda i: (0, 0))],
        out_specs=pl.BlockSpec((tm, d), lambda i: (i, 0)),
        out_shape=jax.ShapeDtypeStruct((m, d), out_dtype),
        compiler_params=_params(("parallel",)),
        name="rmsnorm",
    )(x, gain)


def _rope_kernel(pos_ref, invf_ref, cos_ref, sin_ref):
    ang = pos_ref[...] * invf_ref[...]
    lane = lax.broadcasted_iota(jnp.int32, ang.shape, 1)
    s = jnp.sin(ang)
    cos_ref[...] = jnp.cos(ang)
    sin_ref[...] = jnp.where(lane < ROT_DIM // 2, -s, s)


def _rope_tables(positions, tr=1024):
    n = positions.size
    pos = positions.astype(F32).reshape(n, 1)
    half = ROT_DIM // 2
    inv = 1.0 / (ROPE_THETA ** (jnp.arange(0, ROT_DIM, 2, dtype=F32) / ROT_DIM))
    invf = jnp.concatenate([inv, inv, jnp.zeros((HEAD_DIM - 2 * half,), F32)]).reshape(1, HEAD_DIM)
    return pl.pallas_call(
        _rope_kernel,
        grid=(n // tr,),
        in_specs=[pl.BlockSpec((tr, 1), lambda i: (i, 0)),
                  pl.BlockSpec((1, HEAD_DIM), lambda i: (0, 0))],
        out_specs=[pl.BlockSpec((tr, HEAD_DIM), lambda i: (i, 0))] * 2,
        out_shape=[jax.ShapeDtypeStruct((n, HEAD_DIM), F32)] * 2,
        compiler_params=_params(("parallel",)),
        name="rope_tables",
    )(pos, invf)


def _rotary(x, cos, sin_signed, lane):
    partner = jnp.where(lane < ROT_DIM // 2,
                        pltpu.roll(x, HEAD_DIM - ROT_DIM // 2, axis=1),
                        pltpu.roll(x, ROT_DIM // 2, axis=1))
    return x * cos + partner * sin_signed


def _in_proj_kernel(q_tile, k_tile, x_ref, w_ref, cos_ref, sin_ref, o_ref):
    j = pl.program_id(1)
    rotated = jnp.logical_or(j == q_tile, j == k_tile)

    @pl.when(jnp.logical_not(rotated))
    def _():
        o_ref[...] = jnp.dot(x_ref[...], w_ref[...], preferred_element_type=F32)

    @pl.when(rotated)
    def _():
        z = jnp.dot(x_ref[...], w_ref[...], preferred_element_type=F32)
        scale = jnp.where(j == q_tile, HEAD_DIM ** -0.5, 1.0)
        cos = cos_ref[...] * scale
        sin = sin_ref[...] * scale
        lane = lax.broadcasted_iota(jnp.int32, cos.shape, 1)
        for h in range(o_ref.shape[1] // HEAD_DIM):
            cols = slice(h * HEAD_DIM, (h + 1) * HEAD_DIM)
            o_ref[:, cols] = _rotary(z[:, cols], cos, sin, lane)


def _in_proj(u, w, cos, sin, tm=1024, tn=GROUP_WIDTH):
    m, k = u.shape
    n = w.shape[1]
    tab = pl.BlockSpec((tm, HEAD_DIM), lambda i, j: (i, 0))
    return pl.pallas_call(
        functools.partial(_in_proj_kernel, 4, 5),
        grid=(m // tm, n // tn),
        in_specs=[pl.BlockSpec((tm, k), lambda i, j: (i, 0)),
                  pl.BlockSpec((k, tn), lambda i, j: (0, j)), tab, tab],
        out_specs=pl.BlockSpec((tm, tn), lambda i, j: (i, j)),
        out_shape=jax.ShapeDtypeStruct((m, n), F32),
        compiler_params=_params(("parallel", "parallel")),
        name="in_proj",
    )(u, w, cos, sin)


def _hgrn_kernel(layer, zq_ref, zf_ref, zi_ref, zg_ref, lbp_ref, gn_ref, o_ref,
                 st_ref, q_s, k_s, b_s):
    C, SB = HGRN_CHUNK, HGRN_SUB
    tb = zq_ref.shape[0]
    n_chunks = tb // C
    mid = C // 2 - 1

    @pl.when(pl.program_id(2) == 0)
    def _():
        st_ref[...] = jnp.zeros_like(st_ref)

    lbp = lbp_ref[...]
    e = jnp.exp(lbp - jnp.max(lbp, axis=0, keepdims=True))
    sm = e / jnp.sum(e, axis=0, keepdims=True)
    lb = jnp.zeros((1, HEAD_DIM), F32)
    for j in range(1, layer + 1):
        lb = lb + sm[j:j + 1, :]
    lb_pos = lb > 0.0
    log_lb = jnp.log(jnp.where(lb_pos, lb, 1.0))
    log1m_lb = jnp.log1p(-lb)
    one_m_lb = 1.0 - lb
    gn = gn_ref[...]

    row = lax.broadcasted_iota(jnp.int32, (C, C), 0)
    col = lax.broadcasted_iota(jnp.int32, (C, C), 1)
    causal = row >= col
    tril = jnp.where(causal, 1.0, 0.0).astype(BF16)

    worst = jnp.zeros((1, HEAD_DIM), F32)
    for c in range(n_chunks):
        rows = pl.ds(c * C, C)
        zf = zf_ref[rows, :]
        en = jnp.exp(-jnp.abs(zf))
        rcp = 1.0 / (1.0 + en)
        sig_neg = jnp.where(zf >= 0.0, en * rcp, rcp)
        log_sig = jnp.minimum(zf, 0.0) - jnp.log1p(en)
        t2 = log1m_lb + log_sig
        if layer > 0:
            mx = jnp.maximum(log_lb, t2)
            lae = mx + jnp.log1p(jnp.exp(-jnp.abs(log_lb - t2)))
            lf = jnp.where(lb_pos, lae, t2)
        else:
            lf = t2
        hi = lf.astype(BF16)
        lo = (lf - hi.astype(F32)).astype(BF16)
        bb = jnp.dot(tril, jnp.concatenate([hi, lo], axis=1), preferred_element_type=F32)
        b = bb[:, :HEAD_DIM] + bb[:, HEAD_DIM:]
        q_s[rows, :] = _silu(zq_ref[rows, :])
        k_s[rows, :] = one_m_lb * sig_neg
        b_s[rows, :] = b
        worst = jnp.maximum(worst, jnp.maximum(-b[mid:mid + 1, :],
                                               b[mid:mid + 1, :] - b[C - 1:C, :]))
    factorisable = jnp.max(worst) <= HGRN_MAX_HALF_DECAY

    def finish(o, rows):
        o = _rms(o, gn) * _silu(zg_ref[rows, :])
        o_ref[rows, :] = o.astype(o_ref.dtype)

    def state_step(st, vb, kk, b):
        b_end = b[C - 1:C, :]
        kd = (kk * jnp.exp(b_end - b)).astype(BF16)
        return st * jnp.exp(b_end) + lax.dot_general(vb, kd, TN_DIMS, preferred_element_type=F32)

    @pl.when(factorisable)
    def _():
        st = st_ref[...]
        for c in range(n_chunks):
            rows = pl.ds(c * C, C)
            q, kk, b = q_s[rows, :], k_s[rows, :], b_s[rows, :]
            vb = zi_ref[rows, :].astype(BF16)
            r = b[mid:mid + 1, :]
            qm = (q * jnp.exp(b - r)).astype(BF16)
            km = (kk * jnp.exp(r - b)).astype(BF16)
            a = lax.dot_general(qm, km, NT_DIMS, preferred_element_type=F32)
            a = jnp.where(causal, a, 0.0).astype(BF16)
            qe = (q * jnp.exp(b)).astype(BF16)
            o = (jnp.dot(a, vb, preferred_element_type=F32)
                 + lax.dot_general(qe, st.astype(BF16), NT_DIMS, preferred_element_type=F32))
            st = state_step(st, vb, kk, b)
            finish(o, rows)
        st_ref[...] = st

    @pl.when(jnp.logical_not(factorisable))
    def _():
        ones = jnp.ones((HEAD_DIM, HEAD_DIM), BF16)
        sub_t = lax.broadcasted_iota(jnp.int32, (SB, HEAD_DIM), 0)

        def chunk(c, carry):
            r0 = pl.multiple_of(c * C, C)
            rows = pl.ds(r0, C)
            q, kk, b = q_s[rows, :], k_s[rows, :], b_s[rows, :]
            v = zi_ref[rows, :]
            vb = v.astype(BF16)
            st = st_ref[...]
            o_inter = lax.dot_general((q * jnp.exp(b)).astype(BF16), st.astype(BF16), NT_DIMS,
                                      preferred_element_type=F32)
            outs = []
            for i in range(C // SB):
                lo_r = i * SB
                q_i = q[lo_r:lo_r + SB, :]
                b_i = b[lo_r:lo_r + SB, :]
                o_i = o_inter[lo_r:lo_r + SB, :]
                if i > 0:
                    bref = b[lo_r - 1:lo_r, :]
                    qs = (q_i * jnp.exp(b_i - bref)).astype(BF16)
                    ks = (kk[:lo_r, :] * jnp.exp(bref - b[:lo_r, :])).astype(BF16)
                    a = lax.dot_general(qs, ks, NT_DIMS, preferred_element_type=F32)
                    o_i = o_i + jnp.dot(a.astype(BF16), vb[:lo_r, :], preferred_element_type=F32)
                slabs = []
                for s in range(SB):
                    b_row = b_s[pl.ds(r0 + lo_r + s, 1), :]
                    k_row = k_s[pl.ds(r0 + lo_r + s, 1), :]
                    w = q_i * k_row * jnp.exp(jnp.minimum(b_i - b_row, 0.0))
                    slabs.append(jnp.where(sub_t >= s, w, 0.0).astype(BF16))
                red = jnp.dot(jnp.concatenate(slabs, axis=0), ones, preferred_element_type=F32)
                for s in range(SB):
                    o_i = o_i + red[s * SB:(s + 1) * SB, :] * v[lo_r + s:lo_r + s + 1, :]
                outs.append(o_i)
            st_ref[...] = state_step(st, vb, kk, b)
            finish(jnp.concatenate(outs, axis=0), rows)
            return carry

        lax.fori_loop(0, n_chunks, chunk, 0)


def _hgrn(z, lb_param, g_norm, layer, batch, seq, tb=1024):
    nt = seq // tb
    nh = N_HEADS

    def col(off):
        return pl.BlockSpec((tb, HEAD_DIM), lambda b, h, t, off=off: (b * nt + t, off + h))

    return pl.pallas_call(
        functools.partial(_hgrn_kernel, layer),
        grid=(batch, nh, nt),
        in_specs=[col(0), col(nh), col(2 * nh), col(3 * nh),
                  pl.BlockSpec((lb_param.shape[0], HEAD_DIM), lambda b, h, t: (0, h)),
                  pl.BlockSpec((1, HEAD_DIM), lambda b, h, t: (0, 0))],
        out_specs=pl.BlockSpec((tb, HEAD_DIM), lambda b, h, t: (b * nt + t, h)),
        out_shape=jax.ShapeDtypeStruct((batch * seq, GROUP_WIDTH), BF16),
        scratch_shapes=[pltpu.VMEM((HEAD_DIM, HEAD_DIM), F32),
                        pltpu.VMEM((tb, HEAD_DIM), F32),
                        pltpu.VMEM((tb, HEAD_DIM), F32),
                        pltpu.VMEM((tb, HEAD_DIM), F32)],
        compiler_params=_params(("parallel", "parallel", "arbitrary")),
        name="hgrn2",
    )(z, z, z, z, lb_param, g_norm)


def _attn_kernel(q_ref, k_ref, kp_ref, v_ref, vp_ref, gn_ref, o_ref, o_buf, c_buf, bias_buf):
    TQ, Q, G = ATTN_TILE, ATTN_BLOCK, ATTN_GROUP
    t = pl.program_id(2)

    qi = lax.broadcasted_iota(jnp.int32, (Q, 2 * Q), 0)
    kj = lax.broadcasted_iota(jnp.int32, (Q, 2 * Q), 1)
    dist = kj - qi
    band = jnp.where((dist >= 0) & (dist <= Q), 0.0, -jnp.inf)
    first_key = jnp.where(t > 0, 0, Q)
    bias_buf[0] = band
    bias_buf[1] = jnp.where(kj >= first_key, band, -jnp.inf)
    ones_v = jnp.ones((2 * Q, HEAD_DIM), BF16)
    gn = gn_ref[...]
    strided = [d for d in DILATIONS if d > 1]
    n_strided = len(strided)

    def run_branch(d, g):
        nb_per_res = TQ // (Q * d)

        def rows_of(start):
            if d > 1:
                return pl.ds(start, Q, stride=d)
            return pl.ds(start if isinstance(start, int) else pl.multiple_of(start, Q), Q)

        def group(res, nb0, at_start):
            blocks = []
            for j in range(G):
                if nb_per_res >= G:
                    blocks.append((res + d * Q * (nb0 + j), at_start and j == 0))
                else:
                    nb = j % nb_per_res
                    blocks.append((res + j // nb_per_res + d * Q * nb, nb == 0))

            def with_prev(cur_ref, prev_ref, r0, first):
                prev = prev_ref[rows_of(r0 + TQ - d * Q), :] if first else cur_ref[rows_of(r0 - d * Q), :]
                return jnp.concatenate([prev, cur_ref[rows_of(r0), :]], axis=0).astype(BF16)

            qs = [q_ref[rows_of(r0), :].astype(BF16) for r0, _ in blocks]
            ks = [with_prev(k_ref, kp_ref, r0, first) for r0, first in blocks]
            ss = [lax.dot_general(q, k, NT_DIMS, preferred_element_type=F32)
                  + bias_buf[1 if first else 0] for q, k, (_, first) in zip(qs, ks, blocks)]
            ms = [jnp.max(s, axis=-1, keepdims=True) for s in ss]
            ps = [jnp.exp(s - m).astype(BF16) for s, m in zip(ss, ms)]
            vs = [with_prev(v_ref, vp_ref, r0, first) for r0, first in blocks]
            accs = [jnp.dot(p, jnp.concatenate([v, ones_v], axis=1), preferred_element_type=F32)
                    for p, v in zip(ps, vs)]
            for acc, m, (r0, _) in zip(accs, ms, blocks):
                l = acc[:, HEAD_DIM:]
                o = acc[:, :HEAD_DIM] * (1.0 / l)
                lse = m + jnp.log(l)
                rows = rows_of(r0)
                if g is not None:
                    o_buf[g, rows, :] = o
                    c_buf[g, rows, :] = lse
                else:
                    cs = [c_buf[i, rows, :] for i in range(n_strided)]
                    c_max = functools.reduce(jnp.maximum, cs, lse)
                    w = jnp.exp(lse - c_max)
                    num, den = w * o, w
                    for i in range(n_strided):
                        w = jnp.exp(cs[i] - c_max)
                        num = num + w * o_buf[i, rows, :]
                        den = den + w
                    o_ref[rows, :] = _rms(num / den, gn).astype(o_ref.dtype)

        def loop(lo, hi, body):
            def step(it, carry):
                body(it)
                return carry
            lax.fori_loop(lo, hi, step, 0)

        if nb_per_res > G:
            for res in range(d):
                group(res, 0, True)
                loop(1, nb_per_res // G, lambda it: group(res, it * G, False))
        elif nb_per_res == G:
            loop(0, d, lambda res: group(res, 0, True))
        else:
            per_group = G // nb_per_res
            loop(0, d // per_group, lambda it: group(it * per_group, 0, True))

    for g, d in enumerate(strided):
        run_branch(d, g)
    run_branch(1, None)


def _attn(z, g_norm, batch, seq):
    TQ = ATTN_TILE
    nt = seq // TQ
    nh = N_HEADS
    n_strided = len(DILATIONS) - 1

    def col(off, back=0):
        return pl.BlockSpec((TQ, HEAD_DIM),
                            lambda b, h, t: (b * nt + jnp.maximum(t - back, 0), off + h))

    return pl.pallas_call(
        _attn_kernel,
        grid=(batch, nh, nt),
        in_specs=[col(4 * nh), col(5 * nh), col(5 * nh, 1), col(6 * nh), col(6 * nh, 1),
                  pl.BlockSpec((1, HEAD_DIM), lambda b, h, t: (0, 0))],
        out_specs=pl.BlockSpec((TQ, HEAD_DIM), lambda b, h, t: (b * nt + t, h)),
        out_shape=jax.ShapeDtypeStruct((batch * seq, GROUP_WIDTH), BF16),
        scratch_shapes=[pltpu.VMEM((n_strided, TQ, HEAD_DIM), F32),
                        pltpu.VMEM((n_strided, TQ, HEAD_DIM), F32),
                        pltpu.VMEM((2, ATTN_BLOCK, 2 * ATTN_BLOCK), F32)],
        compiler_params=_params(("parallel", "parallel", "parallel")),
        name="dilated_attn",
    )(z, z, z, z, z, g_norm)


def _out_proj_kernel(oh_ref, oa_ref, w_ref, h_ref, g_ref, h_out_ref, u_out_ref):
    half = oh_ref.shape[1]
    acc = jnp.dot(oh_ref[...], w_ref[pl.ds(0, half), :], preferred_element_type=F32)
    acc = acc + jnp.dot(oa_ref[...], w_ref[pl.ds(half, half), :], preferred_element_type=F32)
    h = h_ref[...] + acc
    h_out_ref[...] = h
    u_out_ref[...] = _rms(h, g_ref[...]).astype(u_out_ref.dtype)


def _out_proj(oh, oa, w, h, gain, tm=512):
    m, d = h.shape
    half = oh.shape[1]
    row = lambda i: (i, 0)
    fixed = lambda i: (0, 0)
    return pl.pallas_call(
        _out_proj_kernel,
        grid=(m // tm,),
        in_specs=[pl.BlockSpec((tm, half), row), pl.BlockSpec((tm, half), row),
                  pl.BlockSpec(w.shape, fixed), pl.BlockSpec((tm, d), row),
                  pl.BlockSpec((1, d), fixed)],
        out_specs=[pl.BlockSpec((tm, d), row), pl.BlockSpec((tm, d), row)],
        out_shape=[jax.ShapeDtypeStruct((m, d), F32), jax.ShapeDtypeStruct((m, d), BF16)],
        compiler_params=_params(("parallel",)),
        name="out_proj",
    )(oh, oa, w, h, gain)


def _mlp_kernel(u_ref, w1_ref, w2_ref, h_ref, o_ref):
    f = pl.program_id(1)

    def step(base_ref):
        a = jnp.maximum(jnp.dot(u_ref[...], w1_ref[...], preferred_element_type=F32), 0.0)
        o_ref[...] = base_ref[...] + jnp.dot((a * a).astype(BF16), w2_ref[...],
                                             preferred_element_type=F32)

    @pl.when(f == 0)
    def _():
        step(h_ref)

    @pl.when(f > 0)
    def _():
        step(o_ref)


def _mlp(u, w1, w2, h, tm=1024, tf=512):
    m, d = h.shape
    dff = w1.shape[1]
    return pl.pallas_call(
        _mlp_kernel,
        grid=(m // tm, dff // tf),
        in_specs=[pl.BlockSpec((tm, d), lambda i, f: (i, 0)),
                  pl.BlockSpec((d, tf), lambda i, f: (0, f)),
                  pl.BlockSpec((tf, d), lambda i, f: (f, 0)),
                  pl.BlockSpec((tm, d), lambda i, f: (i, 0), pipeline_mode=pl.Buffered(1))],
        out_specs=pl.BlockSpec((tm, d), lambda i, f: (i, 0)),
        out_shape=jax.ShapeDtypeStruct((m, d), F32),
        compiler_params=_params(("parallel", "arbitrary")),
        name="mlp",
    )(u, w1, w2, h)


def _ple_kernel(last, h_ref, p_ref, wg_ref, wp_ref, gp_ref, gn_ref, *out_refs):
    h = h_ref[...]
    u = _rms(h, gp_ref[...]).astype(BF16)
    zg = jnp.dot(u, wg_ref[...], preferred_element_type=F32)
    gate = 1.0 / (1.0 + jnp.exp(-zg))
    pe = jnp.dot(p_ref[...].astype(BF16), wp_ref[...], preferred_element_type=F32)
    h = h + pe * gate
    nxt = _rms(h, gn_ref[...])
    if last:
        out_refs[0][...] = nxt
    else:
        out_refs[0][...] = h
        out_refs[1][...] = nxt.astype(out_refs[1].dtype)


def _ple(h, p, layer, w_pg, w_pp, g_ple, g_next, last, tm=512):
    m, d = h.shape
    pd = p.shape[1]
    row = lambda i: (i, 0)
    fixed = lambda i: (0, 0)
    p_row = lambda i: (layer * (m // tm) + i, 0)
    if last:
        out_specs = [pl.BlockSpec((tm, d), row)]
        out_shape = [jax.ShapeDtypeStruct((m, d), F32)]
    else:
        out_specs = [pl.BlockSpec((tm, d), row), pl.BlockSpec((tm, d), row)]
        out_shape = [jax.ShapeDtypeStruct((m, d), F32), jax.ShapeDtypeStruct((m, d), BF16)]
    return pl.pallas_call(
        functools.partial(_ple_kernel, last),
        grid=(m // tm,),
        in_specs=[pl.BlockSpec((tm, d), row), pl.BlockSpec((tm, pd), p_row),
                  pl.BlockSpec(w_pg.shape, fixed), pl.BlockSpec(w_pp.shape, fixed),
                  pl.BlockSpec((1, d), fixed), pl.BlockSpec((1, d), fixed)],
        out_specs=out_specs,
        out_shape=out_shape,
        compiler_params=_params(("parallel",)),
        name="ple",
    )(h, p, w_pg, w_pp, g_ple, g_next)


def kernel(x, p, positions, norm1, w_in, lb_param, hgrn_norm, attn_norm, w_out, norm2, w1, w2,
           ple_norm, w_pg, w_pp, final_norm):
    batch, seq, d = x.shape
    depth = w_in.shape[0]
    m = batch * seq
    assert seq % ATTN_TILE == 0 and d == 2 * GROUP_WIDTH

    cos, sin = _rope_tables(positions)
    h = x.reshape(m, d)
    p_rows = p.reshape(depth * m, p.shape[-1])
    u = _norm(h, norm1[0].reshape(1, d), BF16)
    out = None
    for i in range(depth):
        last = i == depth - 1
        z = _in_proj(u, w_in[i].astype(BF16), cos, sin)
        o_hgrn = _hgrn(z, lb_param, hgrn_norm[i].reshape(1, HEAD_DIM), i, batch, seq)
        o_attn = _attn(z, attn_norm[i].reshape(1, HEAD_DIM), batch, seq)
        h, u2 = _out_proj(o_hgrn, o_attn, w_out[i].astype(BF16), h, norm2[i].reshape(1, d))
        h = _mlp(u2, w1[i].astype(BF16), w2[i].astype(BF16), h)
        g_next = final_norm if last else norm1[i + 1]
        res = _ple(h, p_rows, i, w_pg[i].astype(BF16), w_pp[i].astype(BF16),
                   ple_norm[i].reshape(1, d), g_next.reshape(1, d), last)
        if last:
            out = res[0]
        else:
            h, u = res
    return out.reshape(batch, seq, d)
```

```python
import functools

import jax
import jax.numpy as jnp
from jax import lax
from jax.experimental import pallas as pl
from jax.experimental.pallas import tpu as pltpu

F32 = jnp.float32
BF16 = jnp.bfloat16

HEAD_DIM = 128
N_HEADS = 8
GROUP_WIDTH = N_HEADS * HEAD_DIM
ROT_DIM = HEAD_DIM // 4
ROPE_THETA = 500000.0
NORM_EPS = 1e-6
HGRN_CHUNK = 64
HGRN_SUB = 16
HGRN_MAX_HALF_DECAY = 60.0
ATTN_BLOCK = 128
ATTN_TILE = 2048
ATTN_GROUP = 8
DILATIONS = (1, 4, 16)
VMEM_LIMIT = 56 * 1024 * 1024

NT_DIMS = (((1,), (1,)), ((), ()))
TN_DIMS = (((0,), (0,)), ((), ()))


def _params(semantics):
    return pltpu.CompilerParams(dimension_semantics=semantics, vmem_limit_bytes=VMEM_LIMIT)


def _rms(x, gain):
    ms = jnp.mean(x * x, axis=-1, keepdims=True)
    return x * lax.rsqrt(ms + NORM_EPS) * gain


def _silu(x):
    return x / (1.0 + jnp.exp(-x))


def _norm_kernel(x_ref, g_ref, o_ref):
    o_ref[...] = _rms(x_ref[...], g_ref[...]).astype(o_ref.dtype)


def _norm(x, gain, out_dtype, tm=512):
    m, d = x.shape
    return pl.pallas_call(
        _norm_kernel,
        grid=(m // tm,),
        in_specs=[pl.BlockSpec((tm, d), lambda i: (i, 0)),
                  pl.BlockSpec((1, d), lambda i: (0, 0))],
        out_specs=pl.BlockSpec((tm, d), lambda i: (i, 0)),
        out_shape=jax.ShapeDtypeStruct((m, d), out_dtype),
        compiler_params=_params(("parallel",)),
        name="rmsnorm",
    )(x, gain)


def _rope_kernel(pos_ref, invf_ref, cos_ref, sin_ref):
    ang = pos_ref[...] * invf_ref[...]
    lane = lax.broadcasted_iota(jnp.int32, ang.shape, 1)
    s = jnp.sin(ang)
    cos_ref[...] = jnp.cos(ang)
    sin_ref[...] = jnp.where(lane < HEAD_DIM // 2, -s, s)


def _rope_tables(positions, tr=1024):
    n = positions.size
    pos = positions.astype(F32).reshape(n, 1)
    half = ROT_DIM // 2
    inv = 1.0 / (ROPE_THETA ** (jnp.arange(0, ROT_DIM, 2, dtype=F32) / ROT_DIM))
    gap = jnp.zeros((HEAD_DIM // 2 - half,), F32)
    invf = jnp.concatenate([inv, gap, inv, gap]).reshape(1, HEAD_DIM)
    return pl.pallas_call(
        _rope_kernel,
        grid=(n // tr,),
        in_specs=[pl.BlockSpec((tr, 1), lambda i: (i, 0)),
                  pl.BlockSpec((1, HEAD_DIM), lambda i: (0, 0))],
        out_specs=[pl.BlockSpec((tr, HEAD_DIM), lambda i: (i, 0))] * 2,
        out_shape=[jax.ShapeDtypeStruct((n, HEAD_DIM), F32)] * 2,
        compiler_params=_params(("parallel",)),
        name="rope_tables",
    )(pos, invf)


def _pair_rotary_lanes(w_in, first_head, n_heads):
    depth, k, n = w_in.shape
    half = ROT_DIM // 2
    heads = w_in.reshape(depth, k, n // HEAD_DIM, HEAD_DIM)
    sel = heads[:, :, first_head:first_head + n_heads]
    mid = HEAD_DIM // 2 + half
    paired = jnp.concatenate([sel[..., :half], sel[..., ROT_DIM:mid],
                              sel[..., half:ROT_DIM], sel[..., mid:]], axis=-1)
    heads = jnp.concatenate([heads[:, :, :first_head], paired,
                             heads[:, :, first_head + n_heads:]], axis=2)
    return heads.reshape(depth, k, n)


def _col_tiles_bf16(w, tn):
    depth, k, n = w.shape
    w = w.astype(BF16).reshape(depth, k, n // tn, tn).transpose(0, 2, 1, 3)
    return w.reshape(depth * (n // tn) * k, tn)


def _rows_bf16(w):
    return w.astype(BF16).reshape(-1, w.shape[-1])


def _rotary(x, cos, sin_signed):
    return x * cos + pltpu.roll(x, HEAD_DIM // 2, axis=1) * sin_signed


def _in_proj_kernel(q_tile, k_tile, x_ref, w_ref, cos_ref, sin_ref, o_ref):
    j = pl.program_id(1)
    rotated = jnp.logical_or(j == q_tile, j == k_tile)

    @pl.when(jnp.logical_not(rotated))
    def _():
        o_ref[...] = jnp.dot(x_ref[...], w_ref[...], preferred_element_type=F32)

    @pl.when(rotated)
    def _():
        z = jnp.dot(x_ref[...], w_ref[...], preferred_element_type=F32)
        scale = jnp.where(j == q_tile, HEAD_DIM ** -0.5, 1.0)
        cos = cos_ref[...] * scale
        sin = sin_ref[...] * scale
        for h in range(o_ref.shape[1] // HEAD_DIM):
            cols = slice(h * HEAD_DIM, (h + 1) * HEAD_DIM)
            o_ref[:, cols] = _rotary(z[:, cols], cos, sin)


def _in_proj(u, w_tiles, layer, n, cos, sin, tm=1024, tn=GROUP_WIDTH):
    m, k = u.shape
    tab = pl.BlockSpec((tm, HEAD_DIM), lambda i, j: (i, 0))
    return pl.pallas_call(
        functools.partial(_in_proj_kernel, 4, 5),
        grid=(m // tm, n // tn),
        in_specs=[pl.BlockSpec((tm, k), lambda i, j: (i, 0)),
                  pl.BlockSpec((k, tn), lambda i, j: (layer * (n // tn) + j, 0)), tab, tab],
        out_specs=pl.BlockSpec((tm, tn), lambda i, j: (i, j)),
        out_shape=jax.ShapeDtypeStruct((m, n), F32),
        compiler_params=_params(("parallel", "parallel")),
        name="in_proj",
    )(u, w_tiles, cos, sin)


def _hgrn_kernel(layer, zq_ref, zf_ref, zi_ref, zg_ref, lbp_ref, gn_ref, o_ref,
                 st_ref, q_s, k_s, b_s):
    C, SB = HGRN_CHUNK, HGRN_SUB
    tb = zq_ref.shape[0]
    n_chunks = tb // C
    mid = C // 2 - 1

    @pl.when(pl.program_id(2) == 0)
    def _():
        st_ref[...] = jnp.zeros_like(st_ref)

    lbp = lbp_ref[...]
    e = jnp.exp(lbp - jnp.max(lbp, axis=0, keepdims=True))
    sm = e / jnp.sum(e, axis=0, keepdims=True)
    lb = jnp.zeros((1, HEAD_DIM), F32)
    for j in range(1, layer + 1):
        lb = lb + sm[j:j + 1, :]
    lb_pos = lb > 0.0
    log_lb = jnp.log(jnp.where(lb_pos, lb, 1.0))
    log1m_lb = jnp.log1p(-lb)
    one_m_lb = 1.0 - lb
    gn = gn_ref[...]

    row = lax.broadcasted_iota(jnp.int32, (C, C), 0)
    col = lax.broadcasted_iota(jnp.int32, (C, C), 1)
    causal = row >= col
    tril = jnp.where(causal, 1.0, 0.0).astype(BF16)

    worst = jnp.zeros((1, HEAD_DIM), F32)
    for c in range(n_chunks):
        rows = pl.ds(c * C, C)
        zf = zf_ref[rows, :]
        en = jnp.exp(-jnp.abs(zf))
        rcp = 1.0 / (1.0 + en)
        sig_neg = jnp.where(zf >= 0.0, en * rcp, rcp)
        log_sig = jnp.minimum(zf, 0.0) - jnp.log1p(en)
        t2 = log1m_lb + log_sig
        if layer > 0:
            mx = jnp.maximum(log_lb, t2)
            lae = mx + jnp.log1p(jnp.exp(-jnp.abs(log_lb - t2)))
            lf = jnp.where(lb_pos, lae, t2)
        else:
            lf = t2
        hi = lf.astype(BF16)
        lo = (lf - hi.astype(F32)).astype(BF16)
        bb = jnp.dot(tril, jnp.concatenate([hi, lo], axis=1), preferred_element_type=F32)
        b = bb[:, :HEAD_DIM] + bb[:, HEAD_DIM:]
        q_s[rows, :] = _silu(zq_ref[rows, :])
        k_s[rows, :] = one_m_lb * sig_neg
        b_s[rows, :] = b
        worst = jnp.maximum(worst, jnp.maximum(-b[mid:mid + 1, :],
                                               b[mid:mid + 1, :] - b[C - 1:C, :]))
    factorisable = jnp.max(worst) <= HGRN_MAX_HALF_DECAY

    def finish(o, rows):
        o = _rms(o, gn) * _silu(zg_ref[rows, :])
        o_ref[rows, :] = o.astype(o_ref.dtype)

    def state_step(st, vb, kk, b):
        b_end = b[C - 1:C, :]
        kd = (kk * jnp.exp(b_end - b)).astype(BF16)
        return st * jnp.exp(b_end) + lax.dot_general(vb, kd, TN_DIMS, preferred_element_type=F32)

    @pl.when(factorisable)
    def _():
        st = st_ref[...]
        for c in range(n_chunks):
            rows = pl.ds(c * C, C)
            q, kk, b = q_s[rows, :], k_s[rows, :], b_s[rows, :]
            vb = zi_ref[rows, :].astype(BF16)
            r = b[mid:mid + 1, :]
            b_end = b[C - 1:C, :]
            q_mid = q * jnp.exp(b - r)
            k_mid = kk * jnp.exp(r - b)
            a = lax.dot_general(q_mid.astype(BF16), k_mid.astype(BF16), NT_DIMS,
                                preferred_element_type=F32)
            a = jnp.where(causal, a, 0.0).astype(BF16)
            qe = (q_mid * jnp.exp(r)).astype(BF16)
            kd = (k_mid * jnp.exp(b_end - r)).astype(BF16)
            o = (jnp.dot(a, vb, preferred_element_type=F32)
                 + lax.dot_general(qe, st.astype(BF16), NT_DIMS, preferred_element_type=F32))
            st = st * jnp.exp(b_end) + lax.dot_general(vb, kd, TN_DIMS,
                                                       preferred_element_type=F32)
            finish(o, rows)
        st_ref[...] = st

    @pl.when(jnp.logical_not(factorisable))
    def _():
        ones = jnp.ones((HEAD_DIM, HEAD_DIM), BF16)
        sub_t = lax.broadcasted_iota(jnp.int32, (SB, HEAD_DIM), 0)

        def chunk(c, carry):
            r0 = pl.multiple_of(c * C, C)
            rows = pl.ds(r0, C)
            q, kk, b = q_s[rows, :], k_s[rows, :], b_s[rows, :]
            v = zi_ref[rows, :]
            vb = v.astype(BF16)
            st = st_ref[...]
            o_inter = lax.dot_general((q * jnp.exp(b)).astype(BF16), st.astype(BF16), NT_DIMS,
                                      preferred_element_type=F32)
            outs = []
            for i in range(C // SB):
                lo_r = i * SB
                q_i = q[lo_r:lo_r + SB, :]
                b_i = b[lo_r:lo_r + SB, :]
                o_i = o_inter[lo_r:lo_r + SB, :]
                if i > 0:
                    bref = b[lo_r - 1:lo_r, :]
                    qs = (q_i * jnp.exp(b_i - bref)).astype(BF16)
                    ks = (kk[:lo_r, :] * jnp.exp(bref - b[:lo_r, :])).astype(BF16)
                    a = lax.dot_general(qs, ks, NT_DIMS, preferred_element_type=F32)
                    o_i = o_i + jnp.dot(a.astype(BF16), vb[:lo_r, :], preferred_element_type=F32)
                slabs = []
                for s in range(SB):
                    b_row = b_s[pl.ds(r0 + lo_r + s, 1), :]
                    k_row = k_s[pl.ds(r0 + lo_r + s, 1), :]
                    w = q_i * k_row * jnp.exp(jnp.minimum(b_i - b_row, 0.0))
                    slabs.append(jnp.where(sub_t >= s, w, 0.0).astype(BF16))
                red = jnp.dot(jnp.concatenate(slabs, axis=0), ones, preferred_element_type=F32)
                for s in range(SB):
                    o_i = o_i + red[s * SB:(s + 1) * SB, :] * v[lo_r + s:lo_r + s + 1, :]
                outs.append(o_i)
            st_ref[...] = state_step(st, vb, kk, b)
            finish(jnp.concatenate(outs, axis=0), rows)
            return carry

        lax.fori_loop(0, n_chunks, chunk, 0)


def _hgrn(z, lb_param, g_norm, layer, batch, seq, tb=1024):
    nt = seq // tb
    nh = N_HEADS

    def col(off):
        return pl.BlockSpec((tb, HEAD_DIM), lambda b, h, t, off=off: (b * nt + t, off + h))

    return pl.pallas_call(
        functools.partial(_hgrn_kernel, layer),
        grid=(batch, nh, nt),
        in_specs=[col(0), col(nh), col(2 * nh), col(3 * nh),
                  pl.BlockSpec((lb_param.shape[0], HEAD_DIM), lambda b, h, t: (0, h)),
                  pl.BlockSpec((1, HEAD_DIM), lambda b, h, t: (0, 0))],
        out_specs=pl.BlockSpec((tb, HEAD_DIM), lambda b, h, t: (b * nt + t, h)),
        out_shape=jax.ShapeDtypeStruct((batch * seq, GROUP_WIDTH), BF16),
        scratch_shapes=[pltpu.VMEM((HEAD_DIM, HEAD_DIM), F32),
                        pltpu.VMEM((tb, HEAD_DIM), F32),
                        pltpu.VMEM((tb, HEAD_DIM), F32),
                        pltpu.VMEM((tb, HEAD_DIM), F32)],
        compiler_params=_params(("parallel", "parallel", "arbitrary")),
        name="hgrn2",
    )(z, z, z, z, lb_param, g_norm)


def _attn_kernel(q_ref, k_ref, kp_ref, v_ref, vp_ref, gn_ref, o_ref, o_buf, c_buf, bias_buf):
    TQ, Q, G = ATTN_TILE, ATTN_BLOCK, ATTN_GROUP
    t = pl.program_id(2)

    qi = lax.broadcasted_iota(jnp.int32, (Q, 2 * Q), 0)
    kj = lax.broadcasted_iota(jnp.int32, (Q, 2 * Q), 1)
    dist = kj - qi
    band = jnp.where((dist >= 0) & (dist <= Q), 0.0, -jnp.inf)
    first_key = jnp.where(t > 0, 0, Q)
    bias_buf[0] = band
    bias_buf[1] = jnp.where(kj >= first_key, band, -jnp.inf)
    ones_v = jnp.ones((2 * Q, HEAD_DIM), BF16)
    gn = gn_ref[...]
    strided = [d for d in DILATIONS if d > 1]
    n_strided = len(strided)

    def run_branch(d, g):
        nb_per_res = TQ // (Q * d)

        def rows_of(start):
            if d > 1:
                return pl.ds(start, Q, stride=d)
            return pl.ds(start if isinstance(start, int) else pl.multiple_of(start, Q), Q)

        def group(res, nb0, at_start):
            blocks = []
            for j in range(G):
                if nb_per_res >= G:
                    blocks.append((res + d * Q * (nb0 + j), at_start and j == 0))
                else:
                    nb = j % nb_per_res
                    blocks.append((res + j // nb_per_res + d * Q * nb, nb == 0))

            def with_prev(cur_ref, prev_ref, r0, first):
                prev = prev_ref[rows_of(r0 + TQ - d * Q), :] if first else cur_ref[rows_of(r0 - d * Q), :]
                return jnp.concatenate([prev, cur_ref[rows_of(r0), :]], axis=0).astype(BF16)

            qs = [q_ref[rows_of(r0), :].astype(BF16) for r0, _ in blocks]
            ks = [with_prev(k_ref, kp_ref, r0, first) for r0, first in blocks]
            ss = [lax.dot_general(q, k, NT_DIMS, preferred_element_type=F32)
                  + bias_buf[1 if first else 0] for q, k, (_, first) in zip(qs, ks, blocks)]
            ms = [jnp.max(s, axis=-1, keepdims=True) for s in ss]
            ps = [jnp.exp(s - m).astype(BF16) for s, m in zip(ss, ms)]
            vs = [with_prev(v_ref, vp_ref, r0, first) for r0, first in blocks]
            accs = [jnp.dot(p, jnp.concatenate([v, ones_v], axis=1), preferred_element_type=F32)
                    for p, v in zip(ps, vs)]
            for acc, m, (r0, _) in zip(accs, ms, blocks):
                l = acc[:, HEAD_DIM:]
                o = acc[:, :HEAD_DIM] * (1.0 / l)
                lse = m + jnp.log(l)
                rows = rows_of(r0)
                if g is not None:
                    o_buf[g, rows, :] = o
                    c_buf[g, rows, :] = lse
                else:
                    cs = [c_buf[i, rows, :] for i in range(n_strided)]
                    c_max = functools.reduce(jnp.maximum, cs, lse)
                    w = jnp.exp(lse - c_max)
                    num, den = w * o, w
                    for i in range(n_strided):
                        w = jnp.exp(cs[i] - c_max)
                        num = num + w * o_buf[i, rows, :]
                        den = den + w
                    o_ref[rows, :] = _rms(num / den, gn).astype(o_ref.dtype)

        def loop(lo, hi, body):
            def step(it, carry):
                body(it)
                return carry
            lax.fori_loop(lo, hi, step, 0)

        if nb_per_res > G:
            for res in range(d):
                group(res, 0, True)
                loop(1, nb_per_res // G, lambda it: group(res, it * G, False))
        elif nb_per_res == G:
            loop(0, d, lambda res: group(res, 0, True))
        else:
            per_group = G // nb_per_res
            loop(0, d // per_group, lambda it: group(it * per_group, 0, True))

    for g, d in enumerate(strided):
        run_branch(d, g)
    run_branch(1, None)


def _attn(z, g_norm, batch, seq):
    TQ = ATTN_TILE
    nt = seq // TQ
    nh = N_HEADS
    n_strided = len(DILATIONS) - 1

    def col(off, back=0):
        return pl.BlockSpec((TQ, HEAD_DIM),
                            lambda b, h, t: (b * nt + jnp.maximum(t - back, 0), off + h))

    return pl.pallas_call(
        _attn_kernel,
        grid=(batch, nh, nt),
        in_specs=[col(4 * nh), col(5 * nh), col(5 * nh, 1), col(6 * nh), col(6 * nh, 1),
                  pl.BlockSpec((1, HEAD_DIM), lambda b, h, t: (0, 0))],
        out_specs=pl.BlockSpec((TQ, HEAD_DIM), lambda b, h, t: (b * nt + t, h)),
        out_shape=jax.ShapeDtypeStruct((batch * seq, GROUP_WIDTH), BF16),
        scratch_shapes=[pltpu.VMEM((n_strided, TQ, HEAD_DIM), F32),
                        pltpu.VMEM((n_strided, TQ, HEAD_DIM), F32),
                        pltpu.VMEM((2, ATTN_BLOCK, 2 * ATTN_BLOCK), F32)],
        compiler_params=_params(("parallel", "parallel", "parallel")),
        name="dilated_attn",
    )(z, z, z, z, z, g_norm)


def _out_proj_kernel(oh_ref, oa_ref, w_ref, h_ref, g_ref, h_out_ref, u_out_ref):
    half = oh_ref.shape[1]
    acc = jnp.dot(oh_ref[...], w_ref[pl.ds(0, half), :], preferred_element_type=F32)
    acc = acc + jnp.dot(oa_ref[...], w_ref[pl.ds(half, half), :], preferred_element_type=F32)
    h = h_ref[...] + acc
    h_out_ref[...] = h
    u_out_ref[...] = _rms(h, g_ref[...]).astype(u_out_ref.dtype)


def _out_proj(oh, oa, w_rows, layer, h, gain, tm=512):
    m, d = h.shape
    half = oh.shape[1]
    row = lambda i: (i, 0)
    fixed = lambda i: (0, 0)
    return pl.pallas_call(
        _out_proj_kernel,
        grid=(m // tm,),
        in_specs=[pl.BlockSpec((tm, half), row), pl.BlockSpec((tm, half), row),
                  pl.BlockSpec((2 * half, d), lambda i: (layer, 0)), pl.BlockSpec((tm, d), row),
                  pl.BlockSpec((1, d), fixed)],
        out_specs=[pl.BlockSpec((tm, d), row), pl.BlockSpec((tm, d), row)],
        out_shape=[jax.ShapeDtypeStruct((m, d), F32), jax.ShapeDtypeStruct((m, d), BF16)],
        compiler_params=_params(("parallel",)),
        name="out_proj",
    )(oh, oa, w_rows, h, gain)


def _mlp_kernel(u_ref, w1_ref, w2_ref, h_ref, o_ref):
    f = pl.program_id(1)

    def step(base_ref):
        a = jnp.maximum(jnp.dot(u_ref[...], w1_ref[...], preferred_element_type=F32), 0.0)
        o_ref[...] = base_ref[...] + jnp.dot((a * a).astype(BF16), w2_ref[...],
                                             preferred_element_type=F32)

    @pl.when(f == 0)
    def _():
        step(h_ref)

    @pl.when(f > 0)
    def _():
        step(o_ref)


MLP_TF = 512


def _mlp(u, w1_tiles, w2_rows, layer, dff, h, tm=1024, tf=MLP_TF):
    m, d = h.shape
    nf = dff // tf
    return pl.pallas_call(
        _mlp_kernel,
        grid=(m // tm, nf),
        in_specs=[pl.BlockSpec((tm, d), lambda i, f: (i, 0)),
                  pl.BlockSpec((d, tf), lambda i, f: (layer * nf + f, 0)),
                  pl.BlockSpec((tf, d), lambda i, f: (layer * nf + f, 0)),
                  pl.BlockSpec((tm, d), lambda i, f: (i, 0), pipeline_mode=pl.Buffered(1))],
        out_specs=pl.BlockSpec((tm, d), lambda i, f: (i, 0)),
        out_shape=jax.ShapeDtypeStruct((m, d), F32),
        compiler_params=_params(("parallel", "arbitrary")),
        name="mlp",
    )(u, w1_tiles, w2_rows, h)


def _ple_kernel(last, h_ref, p_ref, wg_ref, wp_ref, gp_ref, gn_ref, *out_refs):
    h = h_ref[...]
    u = _rms(h, gp_ref[...]).astype(BF16)
    zg = jnp.dot(u, wg_ref[...], preferred_element_type=F32)
    gate = 1.0 / (1.0 + jnp.exp(-zg))
    pe = jnp.dot(p_ref[...].astype(BF16), wp_ref[...], preferred_element_type=F32)
    h = h + pe * gate
    nxt = _rms(h, gn_ref[...])
    if last:
        out_refs[0][...] = nxt
    else:
        out_refs[0][...] = h
        out_refs[1][...] = nxt.astype(out_refs[1].dtype)


def _ple(h, p, layer, wg_rows, wp_rows, g_ple, g_next, last, tm=512):
    m, d = h.shape
    pd = p.shape[1]
    row = lambda i: (i, 0)
    fixed = lambda i: (0, 0)
    of_layer = lambda i: (layer, 0)
    p_row = lambda i: (layer * (m // tm) + i, 0)
    if last:
        out_specs = [pl.BlockSpec((tm, d), row)]
        out_shape = [jax.ShapeDtypeStruct((m, d), F32)]
    else:
        out_specs = [pl.BlockSpec((tm, d), row), pl.BlockSpec((tm, d), row)]
        out_shape = [jax.ShapeDtypeStruct((m, d), F32), jax.ShapeDtypeStruct((m, d), BF16)]
    return pl.pallas_call(
        functools.partial(_ple_kernel, last),
        grid=(m // tm,),
        in_specs=[pl.BlockSpec((tm, d), row), pl.BlockSpec((tm, pd), p_row),
                  pl.BlockSpec((d, d), of_layer), pl.BlockSpec((pd, d), of_layer),
                  pl.BlockSpec((1, d), fixed), pl.BlockSpec((1, d), fixed)],
        out_specs=out_specs,
        out_shape=out_shape,
        compiler_params=_params(("parallel",)),
        name="ple",
    )(h, p, wg_rows, wp_rows, g_ple, g_next)


def kernel(x, p, positions, norm1, w_in, lb_param, hgrn_norm, attn_norm, w_out, norm2, w1, w2,
           ple_norm, w_pg, w_pp, final_norm):
    batch, seq, d = x.shape
    depth, _, in_cols = w_in.shape
    dff = w1.shape[-1]
    m = batch * seq
    assert seq % ATTN_TILE == 0 and d == 2 * GROUP_WIDTH

    first_attn_head = 4 * N_HEADS
    w_in_t = _col_tiles_bf16(_pair_rotary_lanes(w_in, first_attn_head, 2 * N_HEADS), GROUP_WIDTH)
    w1_t, w2_r = _col_tiles_bf16(w1, MLP_TF), _rows_bf16(w2)
    w_out_r, w_pg_r, w_pp_r = _rows_bf16(w_out), _rows_bf16(w_pg), _rows_bf16(w_pp)

    cos, sin = _rope_tables(positions)
    h = x.reshape(m, d)
    p_rows = p.reshape(depth * m, p.shape[-1])
    u = _norm(h, norm1[0].reshape(1, d), BF16)
    out = None
    for i in range(depth):
        last = i == depth - 1
        z = _in_proj(u, w_in_t, i, in_cols, cos, sin)
        o_hgrn = _hgrn(z, lb_param, hgrn_norm[i].reshape(1, HEAD_DIM), i, batch, seq)
        o_attn = _attn(z, attn_norm[i].reshape(1, HEAD_DIM), batch, seq)
        h, u2 = _out_proj(o_hgrn, o_attn, w_out_r, i, h, norm2[i].reshape(1, d))
        h = _mlp(u2, w1_t, w2_r, i, dff, h)
        g_next = final_norm if last else norm1[i + 1]
        res = _ple(h, p_rows, i, w_pg_r, w_pp_r, ple_norm[i].reshape(1, d),
                   g_next.reshape(1, d), last)
        if last:
            out = res[0]
        else:
            h, u = res
    return out.reshape(batch, seq, d)
```

```python
import functools

import jax
import jax.numpy as jnp
from jax import lax
from jax.experimental import pallas as pl
from jax.experimental.pallas import tpu as pltpu

F32 = jnp.float32
BF16 = jnp.bfloat16

HEAD_DIM = 128
N_HEADS = 8
GROUP_WIDTH = N_HEADS * HEAD_DIM
ROT_DIM = HEAD_DIM // 4
ROPE_THETA = 500000.0
NORM_EPS = 1e-6
HGRN_CHUNK = 64
HGRN_SUB = 16
HGRN_MAX_HALF_DECAY = 60.0
ATTN_BLOCK = 128
ATTN_TILE = 2048
ATTN_GROUP = 8
DILATIONS = (1, 4, 16)
VMEM_LIMIT = 56 * 1024 * 1024

NT_DIMS = (((1,), (1,)), ((), ()))
TN_DIMS = (((0,), (0,)), ((), ()))


def _params(semantics):
    return pltpu.CompilerParams(dimension_semantics=semantics, vmem_limit_bytes=VMEM_LIMIT)


def _rms(x, gain):
    ms = jnp.mean(x * x, axis=-1, keepdims=True)
    return x * lax.rsqrt(ms + NORM_EPS) * gain


def _silu(x):
    return x / (1.0 + jnp.exp(-x))


def _norm_kernel(x_ref, g_ref, o_ref):
    o_ref[...] = _rms(x_ref[...], g_ref[...]).astype(o_ref.dtype)


def _norm(x, gain, out_dtype, tm=512):
    m, d = x.shape
    return pl.pallas_call(
        _norm_kernel,
        grid=(m // tm,),
        in_specs=[pl.BlockSpec((tm, d), lambda i: (i, 0)),
                  pl.BlockSpec((1, d), lambda i: (0, 0))],
        out_specs=pl.BlockSpec((tm, d), lambda i: (i, 0)),
        out_shape=jax.ShapeDtypeStruct((m, d), out_dtype),
        compiler_params=_params(("parallel",)),
        name="rmsnorm",
    )(x, gain)


def _rope_kernel(pos_ref, invf_ref, cos_ref, sin_ref):
    ang = pos_ref[...] * invf_ref[...]
    lane = lax.broadcasted_iota(jnp.int32, ang.shape, 1)
    s = jnp.sin(ang)
    cos_ref[...] = jnp.cos(ang)
    sin_ref[...] = jnp.where(lane < HEAD_DIM // 2, -s, s)


def _rope_tables(positions, tr=1024):
    n = positions.size
    pos = positions.astype(F32).reshape(n, 1)
    half = ROT_DIM // 2
    inv = 1.0 / (ROPE_THETA ** (jnp.arange(0, ROT_DIM, 2, dtype=F32) / ROT_DIM))
    gap = jnp.zeros((HEAD_DIM // 2 - half,), F32)
    invf = jnp.concatenate([inv, gap, inv, gap]).reshape(1, HEAD_DIM)
    return pl.pallas_call(
        _rope_kernel,
        grid=(n // tr,),
        in_specs=[pl.BlockSpec((tr, 1), lambda i: (i, 0)),
                  pl.BlockSpec((1, HEAD_DIM), lambda i: (0, 0))],
        out_specs=[pl.BlockSpec((tr, HEAD_DIM), lambda i: (i, 0))] * 2,
        out_shape=[jax.ShapeDtypeStruct((n, HEAD_DIM), F32)] * 2,
        compiler_params=_params(("parallel",)),
        name="rope_tables",
    )(pos, invf)


def _rows_bf16(w):
    return w.astype(BF16).reshape(-1, w.shape[-1])


def _pair_rotary_lanes(w):
    half = ROT_DIM // 2
    lane = lax.broadcasted_iota(jnp.int32, w.shape, 1)
    up = pltpu.roll(w, HEAD_DIM - half, axis=1)
    down = pltpu.roll(w, HEAD_DIM // 2 - half, axis=1)
    moved = jnp.where(lane < HEAD_DIM // 2, up, down)
    keep = jnp.logical_or(lane < half, lane >= HEAD_DIM // 2 + half)
    return jnp.where(keep, w, moved)


def _rotary(x, cos, sin_signed):
    return x * cos + pltpu.roll(x, HEAD_DIM // 2, axis=1) * sin_signed


def _in_proj_kernel(q_tile, k_tile, x_ref, w_ref, cos_ref, sin_ref, o_ref, wb_ref):
    j = pl.program_id(0)
    first_row_tile = pl.program_id(1) == 0
    rotated = jnp.logical_or(j == q_tile, j == k_tile)
    plain = jnp.logical_not(rotated)
    heads = [slice(h * HEAD_DIM, (h + 1) * HEAD_DIM) for h in range(o_ref.shape[1] // HEAD_DIM)]

    @pl.when(jnp.logical_and(first_row_tile, plain))
    def _():
        wb_ref[...] = w_ref[...].astype(BF16)

    @pl.when(jnp.logical_and(first_row_tile, rotated))
    def _():
        for cols in heads:
            wb_ref[:, cols] = _pair_rotary_lanes(w_ref[:, cols]).astype(BF16)

    @pl.when(plain)
    def _():
        o_ref[...] = jnp.dot(x_ref[...], wb_ref[...], preferred_element_type=F32)

    @pl.when(rotated)
    def _():
        z = jnp.dot(x_ref[...], wb_ref[...], preferred_element_type=F32)
        scale = jnp.where(j == q_tile, HEAD_DIM ** -0.5, 1.0)
        cos = cos_ref[...] * scale
        sin = sin_ref[...] * scale
        for cols in heads:
            o_ref[:, cols] = _rotary(z[:, cols], cos, sin)


def _in_proj(u, w_rows, layer, cos, sin, tm=1024, tn=GROUP_WIDTH):
    m, k = u.shape
    n = w_rows.shape[1]
    tab = pl.BlockSpec((tm, HEAD_DIM), lambda j, i: (i, 0))
    return pl.pallas_call(
        functools.partial(_in_proj_kernel, 4, 5),
        grid=(n // tn, m // tm),
        in_specs=[pl.BlockSpec((tm, k), lambda j, i: (i, 0)),
                  pl.BlockSpec((k, tn), lambda j, i: (layer, j)), tab, tab],
        out_specs=pl.BlockSpec((tm, tn), lambda j, i: (i, j)),
        out_shape=jax.ShapeDtypeStruct((m, n), F32),
        scratch_shapes=[pltpu.VMEM((k, tn), BF16)],
        compiler_params=_params(("parallel", "arbitrary")),
        name="in_proj",
    )(u, w_rows, cos, sin)


def _hgrn_kernel(layer, zq_ref, zf_ref, zi_ref, zg_ref, lbp_ref, gn_ref, o_ref,
                 st_ref, q_s, k_s, b_s):
    C, SB = HGRN_CHUNK, HGRN_SUB
    tb = zq_ref.shape[0]
    n_chunks = tb // C
    mid = C // 2 - 1

    @pl.when(pl.program_id(2) == 0)
    def _():
        st_ref[...] = jnp.zeros_like(st_ref)

    lbp = lbp_ref[...]
    e = jnp.exp(lbp - jnp.max(lbp, axis=0, keepdims=True))
    sm = e / jnp.sum(e, axis=0, keepdims=True)
    lb = jnp.zeros((1, HEAD_DIM), F32)
    for j in range(1, layer + 1):
        lb = lb + sm[j:j + 1, :]
    lb_pos = lb > 0.0
    log_lb = jnp.log(jnp.where(lb_pos, lb, 1.0))
    log1m_lb = jnp.log1p(-lb)
    one_m_lb = 1.0 - lb
    gn = gn_ref[...]

    row = lax.broadcasted_iota(jnp.int32, (C, C), 0)
    col = lax.broadcasted_iota(jnp.int32, (C, C), 1)
    causal = row >= col
    tril = jnp.where(causal, 1.0, 0.0).astype(BF16)

    worst = jnp.zeros((1, HEAD_DIM), F32)
    for c in range(n_chunks):
        rows = pl.ds(c * C, C)
        zf = zf_ref[rows, :]
        en = jnp.exp(-jnp.abs(zf))
        rcp = 1.0 / (1.0 + en)
        sig_neg = jnp.where(zf >= 0.0, en * rcp, rcp)
        log_sig = jnp.minimum(zf, 0.0) - jnp.log1p(en)
        t2 = log1m_lb + log_sig
        if layer > 0:
            mx = jnp.maximum(log_lb, t2)
            lae = mx + jnp.log1p(jnp.exp(-jnp.abs(log_lb - t2)))
            lf = jnp.where(lb_pos, lae, t2)
        else:
            lf = t2
        hi = lf.astype(BF16)
        lo = (lf - hi.astype(F32)).astype(BF16)
        bb = jnp.dot(tril, jnp.concatenate([hi, lo], axis=1), preferred_element_type=F32)
        b = bb[:, :HEAD_DIM] + bb[:, HEAD_DIM:]
        q_s[rows, :] = _silu(zq_ref[rows, :])
        k_s[rows, :] = one_m_lb * sig_neg
        b_s[rows, :] = b
        worst = jnp.maximum(worst, jnp.maximum(-b[mid:mid + 1, :],
                                               b[mid:mid + 1, :] - b[C - 1:C, :]))
    factorisable = jnp.max(worst) <= HGRN_MAX_HALF_DECAY

    def finish(o, rows):
        o = _rms(o, gn) * _silu(zg_ref[rows, :])
        o_ref[rows, :] = o.astype(o_ref.dtype)

    def state_step(st, vb, kk, b):
        b_end = b[C - 1:C, :]
        kd = (kk * jnp.exp(b_end - b)).astype(BF16)
        return st * jnp.exp(b_end) + lax.dot_general(vb, kd, TN_DIMS, preferred_element_type=F32)

    @pl.when(factorisable)
    def _():
        st = st_ref[...]
        for c in range(n_chunks):
            rows = pl.ds(c * C, C)
            q, kk, b = q_s[rows, :], k_s[rows, :], b_s[rows, :]
            vb = zi_ref[rows, :].astype(BF16)
            r = b[mid:mid + 1, :]
            b_end = b[C - 1:C, :]
            q_mid = q * jnp.exp(b - r)
            k_mid = kk * jnp.exp(r - b)
            a = lax.dot_general(q_mid.astype(BF16), k_mid.astype(BF16), NT_DIMS,
                                preferred_element_type=F32)
            a = jnp.where(causal, a, 0.0).astype(BF16)
            qe = (q_mid * jnp.exp(r)).astype(BF16)
            kd = (k_mid * jnp.exp(b_end - r)).astype(BF16)
            o = (jnp.dot(a, vb, preferred_element_type=F32)
                 + lax.dot_general(qe, st.astype(BF16), NT_DIMS, preferred_element_type=F32))
            st = st * jnp.exp(b_end) + lax.dot_general(vb, kd, TN_DIMS,
                                                       preferred_element_type=F32)
            finish(o, rows)
        st_ref[...] = st

    @pl.when(jnp.logical_not(factorisable))
    def _():
        ones = jnp.ones((HEAD_DIM, HEAD_DIM), BF16)
        sub_t = lax.broadcasted_iota(jnp.int32, (SB, HEAD_DIM), 0)

        def chunk(c, carry):
            r0 = pl.multiple_of(c * C, C)
            rows = pl.ds(r0, C)
            q, kk, b = q_s[rows, :], k_s[rows, :], b_s[rows, :]
            v = zi_ref[rows, :]
            vb = v.astype(BF16)
            st = st_ref[...]
            o_inter = lax.dot_general((q * jnp.exp(b)).astype(BF16), st.astype(BF16), NT_DIMS,
                                      preferred_element_type=F32)
            outs = []
            for i in range(C // SB):
                lo_r = i * SB
                q_i = q[lo_r:lo_r + SB, :]
                b_i = b[lo_r:lo_r + SB, :]
                o_i = o_inter[lo_r:lo_r + SB, :]
                if i > 0:
                    bref = b[lo_r - 1:lo_r, :]
                    qs = (q_i * jnp.exp(b_i - bref)).astype(BF16)
                    ks = (kk[:lo_r, :] * jnp.exp(bref - b[:lo_r, :])).astype(BF16)
                    a = lax.dot_general(qs, ks, NT_DIMS, preferred_element_type=F32)
                    o_i = o_i + jnp.dot(a.astype(BF16), vb[:lo_r, :], preferred_element_type=F32)
                slabs = []
                for s in range(SB):
                    b_row = b_s[pl.ds(r0 + lo_r + s, 1), :]
                    k_row = k_s[pl.ds(r0 + lo_r + s, 1), :]
                    w = q_i * k_row * jnp.exp(jnp.minimum(b_i - b_row, 0.0))
                    slabs.append(jnp.where(sub_t >= s, w, 0.0).astype(BF16))
                red = jnp.dot(jnp.concatenate(slabs, axis=0), ones, preferred_element_type=F32)
                for s in range(SB):
                    o_i = o_i + red[s * SB:(s + 1) * SB, :] * v[lo_r + s:lo_r + s + 1, :]
                outs.append(o_i)
            st_ref[...] = state_step(st, vb, kk, b)
            finish(jnp.concatenate(outs, axis=0), rows)
            return carry

        lax.fori_loop(0, n_chunks, chunk, 0)


def _hgrn(z, lb_param, g_norm, layer, batch, seq, tb=1024):
    nt = seq // tb
    nh = N_HEADS

    def col(off):
        return pl.BlockSpec((tb, HEAD_DIM), lambda b, h, t, off=off: (b * nt + t, off + h))

    return pl.pallas_call(
        functools.partial(_hgrn_kernel, layer),
        grid=(batch, nh, nt),
        in_specs=[col(0), col(nh), col(2 * nh), col(3 * nh),
                  pl.BlockSpec((lb_param.shape[0], HEAD_DIM), lambda b, h, t: (0, h)),
                  pl.BlockSpec((1, HEAD_DIM), lambda b, h, t: (0, 0))],
        out_specs=pl.BlockSpec((tb, HEAD_DIM), lambda b, h, t: (b * nt + t, h)),
        out_shape=jax.ShapeDtypeStruct((batch * seq, GROUP_WIDTH), BF16),
        scratch_shapes=[pltpu.VMEM((HEAD_DIM, HEAD_DIM), F32),
                        pltpu.VMEM((tb, HEAD_DIM), F32),
                        pltpu.VMEM((tb, HEAD_DIM), F32),
                        pltpu.VMEM((tb, HEAD_DIM), F32)],
        compiler_params=_params(("parallel", "parallel", "arbitrary")),
        name="hgrn2",
    )(z, z, z, z, lb_param, g_norm)


def _attn_kernel(q_ref, k_ref, kp_ref, v_ref, vp_ref, gn_ref, o_ref, o_buf, c_buf, bias_buf):
    TQ, Q, G = ATTN_TILE, ATTN_BLOCK, ATTN_GROUP
    t = pl.program_id(2)

    qi = lax.broadcasted_iota(jnp.int32, (Q, 2 * Q), 0)
    kj = lax.broadcasted_iota(jnp.int32, (Q, 2 * Q), 1)
    dist = kj - qi
    band = jnp.where((dist >= 0) & (dist <= Q), 0.0, -jnp.inf)
    first_key = jnp.where(t > 0, 0, Q)
    bias_buf[0] = band
    bias_buf[1] = jnp.where(kj >= first_key, band, -jnp.inf)
    ones_v = jnp.ones((2 * Q, HEAD_DIM), BF16)
    gn = gn_ref[...]
    strided = [d for d in DILATIONS if d > 1]
    n_strided = len(strided)

    def run_branch(d, g):
        nb_per_res = TQ // (Q * d)

        def rows_of(start):
            if d > 1:
                return pl.ds(start, Q, stride=d)
            return pl.ds(start if isinstance(start, int) else pl.multiple_of(start, Q), Q)

        def group(res, nb0, at_start):
            blocks = []
            for j in range(G):
                if nb_per_res >= G:
                    blocks.append((res + d * Q * (nb0 + j), at_start and j == 0))
                else:
                    nb = j % nb_per_res
                    blocks.append((res + j // nb_per_res + d * Q * nb, nb == 0))

            def with_prev(cur_ref, prev_ref, r0, first):
                prev = prev_ref[rows_of(r0 + TQ - d * Q), :] if first else cur_ref[rows_of(r0 - d * Q), :]
                return jnp.concatenate([prev, cur_ref[rows_of(r0), :]], axis=0).astype(BF16)

            qs = [q_ref[rows_of(r0), :].astype(BF16) for r0, _ in blocks]
            ks = [with_prev(k_ref, kp_ref, r0, first) for r0, first in blocks]
            ss = [lax.dot_general(q, k, NT_DIMS, preferred_element_type=F32)
                  + bias_buf[1 if first else 0] for q, k, (_, first) in zip(qs, ks, blocks)]
            ms = [jnp.max(s, axis=-1, keepdims=True) for s in ss]
            ps = [jnp.exp(s - m).astype(BF16) for s, m in zip(ss, ms)]
            vs = [with_prev(v_ref, vp_ref, r0, first) for r0, first in blocks]
            accs = [jnp.dot(p, jnp.concatenate([v, ones_v], axis=1), preferred_element_type=F32)
                    for p, v in zip(ps, vs)]
            for acc, m, (r0, _) in zip(accs, ms, blocks):
                l = acc[:, HEAD_DIM:]
                o = acc[:, :HEAD_DIM] * (1.0 / l)
                lse = m + jnp.log(l)
                rows = rows_of(r0)
                if g is not None:
                    o_buf[g, rows, :] = o
                    c_buf[g, rows, :] = lse
                else:
                    cs = [c_buf[i, rows, :] for i in range(n_strided)]
                    c_max = functools.reduce(jnp.maximum, cs, lse)
                    w = jnp.exp(lse - c_max)
                    num, den = w * o, w
                    for i in range(n_strided):
                        w = jnp.exp(cs[i] - c_max)
                        num = num + w * o_buf[i, rows, :]
                        den = den + w
                    o_ref[rows, :] = _rms(num / den, gn).astype(o_ref.dtype)

        def loop(lo, hi, body):
            def step(it, carry):
                body(it)
                return carry
            lax.fori_loop(lo, hi, step, 0)

        if nb_per_res > G:
            for res in range(d):
                group(res, 0, True)
                loop(1, nb_per_res // G, lambda it: group(res, it * G, False))
        elif nb_per_res == G:
            loop(0, d, lambda res: group(res, 0, True))
        else:
            per_group = G // nb_per_res
            loop(0, d // per_group, lambda it: group(it * per_group, 0, True))

    for g, d in enumerate(strided):
        run_branch(d, g)
    run_branch(1, None)


def _attn(z, g_norm, batch, seq):
    TQ = ATTN_TILE
    nt = seq // TQ
    nh = N_HEADS
    n_strided = len(DILATIONS) - 1

    def col(off, back=0):
        return pl.BlockSpec((TQ, HEAD_DIM),
                            lambda b, h, t: (b * nt + jnp.maximum(t - back, 0), off + h))

    return pl.pallas_call(
        _attn_kernel,
        grid=(batch, nh, nt),
        in_specs=[col(4 * nh), col(5 * nh), col(5 * nh, 1), col(6 * nh), col(6 * nh, 1),
                  pl.BlockSpec((1, HEAD_DIM), lambda b, h, t: (0, 0))],
        out_specs=pl.BlockSpec((TQ, HEAD_DIM), lambda b, h, t: (b * nt + t, h)),
        out_shape=jax.ShapeDtypeStruct((batch * seq, GROUP_WIDTH), BF16),
        scratch_shapes=[pltpu.VMEM((n_strided, TQ, HEAD_DIM), F32),
                        pltpu.VMEM((n_strided, TQ, HEAD_DIM), F32),
                        pltpu.VMEM((2, ATTN_BLOCK, 2 * ATTN_BLOCK), F32)],
        compiler_params=_params(("parallel", "parallel", "parallel")),
        name="dilated_attn",
    )(z, z, z, z, z, g_norm)


def _out_proj_kernel(oh_ref, oa_ref, w_ref, h_ref, g_ref, h_out_ref, u_out_ref):
    half = oh_ref.shape[1]
    acc = jnp.dot(oh_ref[...], w_ref[pl.ds(0, half), :], preferred_element_type=F32)
    acc = acc + jnp.dot(oa_ref[...], w_ref[pl.ds(half, half), :], preferred_element_type=F32)
    h = h_ref[...] + acc
    h_out_ref[...] = h
    u_out_ref[...] = _rms(h, g_ref[...]).astype(u_out_ref.dtype)


def _out_proj(oh, oa, w_rows, layer, h, gain, tm=512):
    m, d = h.shape
    half = oh.shape[1]
    row = lambda i: (i, 0)
    fixed = lambda i: (0, 0)
    return pl.pallas_call(
        _out_proj_kernel,
        grid=(m // tm,),
        in_specs=[pl.BlockSpec((tm, half), row), pl.BlockSpec((tm, half), row),
                  pl.BlockSpec((2 * half, d), lambda i: (layer, 0)), pl.BlockSpec((tm, d), row),
                  pl.BlockSpec((1, d), fixed)],
        out_specs=[pl.BlockSpec((tm, d), row), pl.BlockSpec((tm, d), row)],
        out_shape=[jax.ShapeDtypeStruct((m, d), F32), jax.ShapeDtypeStruct((m, d), BF16)],
        compiler_params=_params(("parallel",)),
        name="out_proj",
    )(oh, oa, w_rows, h, gain)


def _mlp_kernel(u_ref, w1_ref, w2_ref, h_ref, o_ref):
    f = pl.program_id(1)

    def step(base_ref):
        a = jnp.maximum(jnp.dot(u_ref[...], w1_ref[...], preferred_element_type=F32), 0.0)
        o_ref[...] = base_ref[...] + jnp.dot((a * a).astype(BF16), w2_ref[...],
                                             preferred_element_type=F32)

    @pl.when(f == 0)
    def _():
        step(h_ref)

    @pl.when(f > 0)
    def _():
        step(o_ref)


MLP_TF = 512


def _mlp(u, w1_rows, w2_rows, layer, h, tm=1024, tf=MLP_TF):
    m, d = h.shape
    nf = w1_rows.shape[1] // tf
    return pl.pallas_call(
        _mlp_kernel,
        grid=(m // tm, nf),
        in_specs=[pl.BlockSpec((tm, d), lambda i, f: (i, 0)),
                  pl.BlockSpec((d, tf), lambda i, f: (layer, f)),
                  pl.BlockSpec((tf, d), lambda i, f: (layer * nf + f, 0)),
                  pl.BlockSpec((tm, d), lambda i, f: (i, 0))],
        out_specs=pl.BlockSpec((tm, d), lambda i, f: (i, 0)),
        out_shape=jax.ShapeDtypeStruct((m, d), F32),
        compiler_params=_params(("parallel", "arbitrary")),
        name="mlp",
    )(u, w1_rows, w2_rows, h)


def _ple_kernel(last, h_ref, p_ref, wg_ref, wp_ref, gp_ref, gn_ref, *out_refs):
    h = h_ref[...]
    u = _rms(h, gp_ref[...]).astype(BF16)
    zg = jnp.dot(u, wg_ref[...], preferred_element_type=F32)
    gate = 1.0 / (1.0 + jnp.exp(-zg))
    pe = jnp.dot(p_ref[...].astype(BF16), wp_ref[...], preferred_element_type=F32)
    h = h + pe * gate
    nxt = _rms(h, gn_ref[...])
    if last:
        out_refs[0][...] = nxt
    else:
        out_refs[0][...] = h
        out_refs[1][...] = nxt.astype(out_refs[1].dtype)


def _ple(h, p, layer, wg_rows, wp_rows, g_ple, g_next, last, tm=512):
    m, d = h.shape
    pd = p.shape[1]
    row = lambda i: (i, 0)
    fixed = lambda i: (0, 0)
    of_layer = lambda i: (layer, 0)
    p_row = lambda i: (layer * (m // tm) + i, 0)
    if last:
        out_specs = [pl.BlockSpec((tm, d), row)]
        out_shape = [jax.ShapeDtypeStruct((m, d), F32)]
    else:
        out_specs = [pl.BlockSpec((tm, d), row), pl.BlockSpec((tm, d), row)]
        out_shape = [jax.ShapeDtypeStruct((m, d), F32), jax.ShapeDtypeStruct((m, d), BF16)]
    return pl.pallas_call(
        functools.partial(_ple_kernel, last),
        grid=(m // tm,),
        in_specs=[pl.BlockSpec((tm, d), row), pl.BlockSpec((tm, pd), p_row),
                  pl.BlockSpec((d, d), of_layer), pl.BlockSpec((pd, d), of_layer),
                  pl.BlockSpec((1, d), fixed), pl.BlockSpec((1, d), fixed)],
        out_specs=out_specs,
        out_shape=out_shape,
        compiler_params=_params(("parallel",)),
        name="ple",
    )(h, p, wg_rows, wp_rows, g_ple, g_next)


def kernel(x, p, positions, norm1, w_in, lb_param, hgrn_norm, attn_norm, w_out, norm2, w1, w2,
           ple_norm, w_pg, w_pp, final_norm):
    batch, seq, d = x.shape
    depth, _, in_cols = w_in.shape
    m = batch * seq
    assert seq % ATTN_TILE == 0 and d == 2 * GROUP_WIDTH

    w_in_r = w_in.reshape(depth * d, in_cols)
    w1_r, w2_r = _rows_bf16(w1), _rows_bf16(w2)
    w_out_r, w_pg_r, w_pp_r = _rows_bf16(w_out), _rows_bf16(w_pg), _rows_bf16(w_pp)

    cos, sin = _rope_tables(positions)
    h = x.reshape(m, d)
    p_rows = p.reshape(depth * m, p.shape[-1])
    u = _norm(h, norm1[0].reshape(1, d), BF16)
    out = None
    for i in range(depth):
        last = i == depth - 1
        z = _in_proj(u, w_in_r, i, cos, sin)
        o_hgrn = _hgrn(z, lb_param, hgrn_norm[i].reshape(1, HEAD_DIM), i, batch, seq)
        o_attn = _attn(z, attn_norm[i].reshape(1, HEAD_DIM), batch, seq)
        h, u2 = _out_proj(o_hgrn, o_attn, w_out_r, i, h, norm2[i].reshape(1, d))
        h = _mlp(u2, w1_r, w2_r, i, h)
        g_next = final_norm if last else norm1[i + 1]
        res = _ple(h, p_rows, i, w_pg_r, w_pp_r, ple_norm[i].reshape(1, d),
                   g_next.reshape(1, d), last)
        if last:
            out = res[0]
        else:
            h, u = res
    return out.reshape(batch, seq, d)
```

```python
import functools

import jax
import jax.numpy as jnp
from jax import lax
from jax.experimental import pallas as pl
from jax.experimental.pallas import tpu as pltpu

F32 = jnp.float32
BF16 = jnp.bfloat16

HEAD_DIM = 128
N_HEADS = 8
GROUP_WIDTH = N_HEADS * HEAD_DIM
ROT_DIM = HEAD_DIM // 4
ROPE_THETA = 500000.0
NORM_EPS = 1e-6
HGRN_CHUNK = 64
HGRN_SUB = 16
HGRN_MAX_HALF_DECAY = 60.0
ATTN_BLOCK = 128
ATTN_TILE = 2048
ATTN_GROUP = 8
DILATIONS = (1, 4, 16)
VMEM_LIMIT = 56 * 1024 * 1024

NT_DIMS = (((1,), (1,)), ((), ()))
TN_DIMS = (((0,), (0,)), ((), ()))


def _params(semantics):
    return pltpu.CompilerParams(dimension_semantics=semantics, vmem_limit_bytes=VMEM_LIMIT)


def _rms(x, gain):
    ms = jnp.mean(x * x, axis=-1, keepdims=True)
    return x * lax.rsqrt(ms + NORM_EPS) * gain


def _silu(x):
    return x / (1.0 + jnp.exp(-x))


def _norm_kernel(x_ref, g_ref, o_ref):
    o_ref[...] = _rms(x_ref[...], g_ref[...]).astype(o_ref.dtype)


def _norm(x, gain, out_dtype, tm=512):
    m, d = x.shape
    return pl.pallas_call(
        _norm_kernel,
        grid=(m // tm,),
        in_specs=[pl.BlockSpec((tm, d), lambda i: (i, 0)),
                  pl.BlockSpec((1, d), lambda i: (0, 0))],
        out_specs=pl.BlockSpec((tm, d), lambda i: (i, 0)),
        out_shape=jax.ShapeDtypeStruct((m, d), out_dtype),
        compiler_params=_params(("parallel",)),
        name="rmsnorm",
    )(x, gain)


def _rope_kernel(pos_ref, invf_ref, cos_ref, sin_ref):
    ang = pos_ref[...] * invf_ref[...]
    cos_ref[...] = jnp.cos(ang)
    sin_ref[...] = jnp.sin(ang)


def _rope_tables(positions, tr=512):
    n = positions.size
    half = ROT_DIM // 2
    per_row = HEAD_DIM // half
    inv = 1.0 / (ROPE_THETA ** (jnp.arange(0, ROT_DIM, 2, dtype=F32) / ROT_DIM))
    pos = jnp.repeat(positions.astype(F32).reshape(n // per_row, per_row), half, axis=1)
    invf = jnp.tile(inv, per_row).reshape(1, HEAD_DIM)
    cos, sin = pl.pallas_call(
        _rope_kernel,
        grid=(n // per_row // tr,),
        in_specs=[pl.BlockSpec((tr, HEAD_DIM), lambda i: (i, 0)),
                  pl.BlockSpec((1, HEAD_DIM), lambda i: (0, 0))],
        out_specs=[pl.BlockSpec((tr, HEAD_DIM), lambda i: (i, 0))] * 2,
        out_shape=[jax.ShapeDtypeStruct((n // per_row, HEAD_DIM), F32)] * 2,
        compiler_params=_params(("parallel",)),
        name="rope_tables",
    )(pos, invf)
    cos, sin = cos.reshape(n, half), sin.reshape(n, half)
    one = jnp.ones((n, HEAD_DIM // 2 - half), F32)
    zero = jnp.zeros_like(one)
    return (jnp.concatenate([cos, one, cos, one], axis=1),
            jnp.concatenate([-sin, zero, sin, zero], axis=1))


def _rows_bf16(w):
    return w.astype(BF16).reshape(-1, w.shape[-1])


def _pair_rotary_lanes(w):
    half = ROT_DIM // 2
    lane = lax.broadcasted_iota(jnp.int32, w.shape, 1)
    up = pltpu.roll(w, HEAD_DIM - half, axis=1)
    down = pltpu.roll(w, HEAD_DIM // 2 - half, axis=1)
    moved = jnp.where(lane < HEAD_DIM // 2, up, down)
    keep = jnp.logical_or(lane < half, lane >= HEAD_DIM // 2 + half)
    return jnp.where(keep, w, moved)


def _rotary(x, cos, sin_signed):
    return x * cos + pltpu.roll(x, HEAD_DIM // 2, axis=1) * sin_signed


def _in_proj_kernel(q_tile, k_tile, x_ref, w_ref, cos_ref, sin_ref, o_ref, wb_ref):
    j = pl.program_id(0)
    first_row_tile = pl.program_id(1) == 0
    rotated = jnp.logical_or(j == q_tile, j == k_tile)
    plain = jnp.logical_not(rotated)
    heads = [slice(h * HEAD_DIM, (h + 1) * HEAD_DIM) for h in range(o_ref.shape[1] // HEAD_DIM)]

    @pl.when(jnp.logical_and(first_row_tile, plain))
    def _():
        wb_ref[...] = w_ref[...].astype(BF16)

    @pl.when(jnp.logical_and(first_row_tile, rotated))
    def _():
        for cols in heads:
            wb_ref[:, cols] = _pair_rotary_lanes(w_ref[:, cols]).astype(BF16)

    @pl.when(plain)
    def _():
        o_ref[...] = jnp.dot(x_ref[...], wb_ref[...], preferred_element_type=F32)

    @pl.when(rotated)
    def _():
        z = jnp.dot(x_ref[...], wb_ref[...], preferred_element_type=F32)
        scale = jnp.where(j == q_tile, HEAD_DIM ** -0.5, 1.0)
        cos = cos_ref[...] * scale
        sin = sin_ref[...] * scale
        for cols in heads:
            o_ref[:, cols] = _rotary(z[:, cols], cos, sin)


def _in_proj(u, w_rows, layer, cos, sin, tm=1024, tn=GROUP_WIDTH):
    m, k = u.shape
    n = w_rows.shape[1]
    tab = pl.BlockSpec((tm, HEAD_DIM), lambda j, i: (i, 0))
    return pl.pallas_call(
        functools.partial(_in_proj_kernel, 4, 5),
        grid=(n // tn, m // tm),
        in_specs=[pl.BlockSpec((tm, k), lambda j, i: (i, 0)),
                  pl.BlockSpec((k, tn), lambda j, i: (layer, j)), tab, tab],
        out_specs=pl.BlockSpec((tm, tn), lambda j, i: (i, j)),
        out_shape=jax.ShapeDtypeStruct((m, n), F32),
        scratch_shapes=[pltpu.VMEM((k, tn), BF16)],
        compiler_params=_params(("parallel", "arbitrary")),
        name="in_proj",
    )(u, w_rows, cos, sin)


def _hgrn_kernel(layer, zq_ref, zf_ref, zi_ref, zg_ref, lbp_ref, gn_ref, o_ref,
                 st_ref, q_s, k_s, b_s):
    C, SB = HGRN_CHUNK, HGRN_SUB
    tb = zq_ref.shape[0]
    n_chunks = tb // C
    mid = C // 2 - 1

    @pl.when(pl.program_id(2) == 0)
    def _():
        st_ref[...] = jnp.zeros_like(st_ref)

    lbp = lbp_ref[...]
    e = jnp.exp(lbp - jnp.max(lbp, axis=0, keepdims=True))
    sm = e / jnp.sum(e, axis=0, keepdims=True)
    lb = jnp.zeros((1, HEAD_DIM), F32)
    for j in range(1, layer + 1):
        lb = lb + sm[j:j + 1, :]
    lb_pos = lb > 0.0
    log1m_lb = jnp.log1p(-lb)
    one_m_lb = 1.0 - lb
    gn = gn_ref[...]

    row = lax.broadcasted_iota(jnp.int32, (C, C), 0)
    col = lax.broadcasted_iota(jnp.int32, (C, C), 1)
    causal = row >= col
    tril = jnp.where(causal, 1.0, 0.0).astype(BF16)

    worst = jnp.zeros((1, HEAD_DIM), F32)
    for c in range(n_chunks):
        rows = pl.ds(c * C, C)
        zf = zf_ref[rows, :]
        en = jnp.exp(-jnp.abs(zf))
        one_p = 1.0 + en
        rcp = 1.0 / one_p
        small = en * rcp
        nonneg = zf >= 0.0
        sig_neg = jnp.where(nonneg, small, rcp)
        log_sig = jnp.minimum(zf, 0.0) - jnp.log(one_p)
        if layer > 0:
            f = lb + one_m_lb * jnp.where(nonneg, rcp, small)
            lf = jnp.where(lb_pos, jnp.log(f), log1m_lb + log_sig)
        else:
            lf = log_sig
        hi = lf.astype(BF16)
        lo = (lf - hi.astype(F32)).astype(BF16)
        bb = jnp.dot(tril, jnp.concatenate([hi, lo], axis=1), preferred_element_type=F32)
        b = bb[:, :HEAD_DIM] + bb[:, HEAD_DIM:]
        q_s[rows, :] = _silu(zq_ref[rows, :])
        k_s[rows, :] = one_m_lb * sig_neg
        b_s[rows, :] = b
        worst = jnp.maximum(worst, jnp.maximum(-b[mid:mid + 1, :],
                                               b[mid:mid + 1, :] - b[C - 1:C, :]))
    factorisable = jnp.max(worst) <= HGRN_MAX_HALF_DECAY

    def finish(o, rows):
        o = _rms(o, gn) * _silu(zg_ref[rows, :])
        o_ref[rows, :] = o.astype(o_ref.dtype)

    def state_step(st, vb, kk, b):
        b_end = b[C - 1:C, :]
        kd = (kk * jnp.exp(b_end - b)).astype(BF16)
        return st * jnp.exp(b_end) + lax.dot_general(vb, kd, TN_DIMS, preferred_element_type=F32)

    @pl.when(factorisable)
    def _():
        st = st_ref[...]
        for c in range(n_chunks):
            rows = pl.ds(c * C, C)
            q, kk, b = q_s[rows, :], k_s[rows, :], b_s[rows, :]
            vb = zi_ref[rows, :].astype(BF16)
            r = b[mid:mid + 1, :]
            b_end = b[C - 1:C, :]
            q_mid = q * jnp.exp(b - r)
            k_mid = kk * jnp.exp(r - b)
            a = lax.dot_general(q_mid.astype(BF16), k_mid.astype(BF16), NT_DIMS,
                                preferred_element_type=F32)
            a = jnp.where(causal, a, 0.0).astype(BF16)
            qe = (q_mid * jnp.exp(r)).astype(BF16)
            kd = (k_mid * jnp.exp(b_end - r)).astype(BF16)
            o = (jnp.dot(a, vb, preferred_element_type=F32)
                 + lax.dot_general(qe, st.astype(BF16), NT_DIMS, preferred_element_type=F32))
            st = st * jnp.exp(b_end) + lax.dot_general(vb, kd, TN_DIMS,
                                                       preferred_element_type=F32)
            finish(o, rows)
        st_ref[...] = st

    @pl.when(jnp.logical_not(factorisable))
    def _():
        ones = jnp.ones((HEAD_DIM, HEAD_DIM), BF16)
        sub_t = lax.broadcasted_iota(jnp.int32, (SB, HEAD_DIM), 0)

        def chunk(c, carry):
            r0 = pl.multiple_of(c * C, C)
            rows = pl.ds(r0, C)
            q, kk, b = q_s[rows, :], k_s[rows, :], b_s[rows, :]
            v = zi_ref[rows, :]
            vb = v.astype(BF16)
            st = st_ref[...]
            o_inter = lax.dot_general((q * jnp.exp(b)).astype(BF16), st.astype(BF16), NT_DIMS,
                                      preferred_element_type=F32)
            outs = []
            for i in range(C // SB):
                lo_r = i * SB
                q_i = q[lo_r:lo_r + SB, :]
                b_i = b[lo_r:lo_r + SB, :]
                o_i = o_inter[lo_r:lo_r + SB, :]
                if i > 0:
                    bref = b[lo_r - 1:lo_r, :]
                    qs = (q_i * jnp.exp(b_i - bref)).astype(BF16)
                    ks = (kk[:lo_r, :] * jnp.exp(bref - b[:lo_r, :])).astype(BF16)
                    a = lax.dot_general(qs, ks, NT_DIMS, preferred_element_type=F32)
                    o_i = o_i + jnp.dot(a.astype(BF16), vb[:lo_r, :], preferred_element_type=F32)
                slabs = []
                for s in range(SB):
                    b_row = b_s[pl.ds(r0 + lo_r + s, 1), :]
                    k_row = k_s[pl.ds(r0 + lo_r + s, 1), :]
                    w = q_i * k_row * jnp.exp(jnp.minimum(b_i - b_row, 0.0))
                    slabs.append(jnp.where(sub_t >= s, w, 0.0).astype(BF16))
                red = jnp.dot(jnp.concatenate(slabs, axis=0), ones, preferred_element_type=F32)
                for s in range(SB):
                    o_i = o_i + red[s * SB:(s + 1) * SB, :] * v[lo_r + s:lo_r + s + 1, :]
                outs.append(o_i)
            st_ref[...] = state_step(st, vb, kk, b)
            finish(jnp.concatenate(outs, axis=0), rows)
            return carry

        lax.fori_loop(0, n_chunks, chunk, 0)


def _hgrn(z, lb_param, g_norm, layer, batch, seq, tb=1024):
    nt = seq // tb
    nh = N_HEADS

    def col(off):
        return pl.BlockSpec((tb, HEAD_DIM), lambda b, h, t, off=off: (b * nt + t, off + h))

    return pl.pallas_call(
        functools.partial(_hgrn_kernel, layer),
        grid=(batch, nh, nt),
        in_specs=[col(0), col(nh), col(2 * nh), col(3 * nh),
                  pl.BlockSpec((lb_param.shape[0], HEAD_DIM), lambda b, h, t: (0, h)),
                  pl.BlockSpec((1, HEAD_DIM), lambda b, h, t: (0, 0))],
        out_specs=pl.BlockSpec((tb, HEAD_DIM), lambda b, h, t: (b * nt + t, h)),
        out_shape=jax.ShapeDtypeStruct((batch * seq, GROUP_WIDTH), BF16),
        scratch_shapes=[pltpu.VMEM((HEAD_DIM, HEAD_DIM), F32),
                        pltpu.VMEM((tb, HEAD_DIM), F32),
                        pltpu.VMEM((tb, HEAD_DIM), F32),
                        pltpu.VMEM((tb, HEAD_DIM), F32)],
        compiler_params=_params(("parallel", "parallel", "arbitrary")),
        name="hgrn2",
    )(z, z, z, z, lb_param, g_norm)


def _attn_kernel(q_ref, k_ref, kp_ref, v_ref, vp_ref, gn_ref, o_ref,
                 o_buf, c_buf, bias_buf, k_cache, v_cache):
    TQ, Q, G = ATTN_TILE, ATTN_BLOCK, ATTN_GROUP
    t = pl.program_id(2)

    @pl.when(t == 0)
    def _():
        k_cache[...] = jnp.zeros_like(k_cache)
        v_cache[...] = jnp.zeros_like(v_cache)

    qi = lax.broadcasted_iota(jnp.int32, (Q, 2 * Q), 0)
    kj = lax.broadcasted_iota(jnp.int32, (Q, 2 * Q), 1)
    dist = kj - qi
    band = jnp.where((dist >= 0) & (dist <= Q), 0.0, -jnp.inf)
    first_key = jnp.where(t > 0, 0, Q)
    bias_buf[0] = band
    bias_buf[1] = jnp.where(kj >= first_key, band, -jnp.inf)
    ones_v = jnp.ones((2 * Q, HEAD_DIM), BF16)
    gn = gn_ref[...]
    strided = [d for d in DILATIONS if d > 1]
    n_strided = len(strided)

    def run_branch(d, g):
        nb_per_res = TQ // (Q * d)

        def rows_of(start):
            if d > 1:
                return pl.ds(start, Q, stride=d)
            return pl.ds(start if isinstance(start, int) else pl.multiple_of(start, Q), Q)

        def group(res, nb0, at_start):
            blocks = []
            for j in range(G):
                if nb_per_res >= G:
                    blocks.append((res + d * Q * (nb0 + j), at_start and j == 0))
                else:
                    nb = j % nb_per_res
                    blocks.append((res + j // nb_per_res + d * Q * nb, nb == 0))

            def with_prev(cur_ref, prev_ref, cache, r0, first):
                cur = cur_ref[rows_of(r0), :].astype(BF16)
                if nb_per_res == 1:
                    prev = cache[r0]
                    fresh.append((cache, r0, cur))
                elif first:
                    prev = prev_ref[rows_of(r0 + TQ - d * Q), :].astype(BF16)
                else:
                    prev = cur_ref[rows_of(r0 - d * Q), :].astype(BF16)
                return jnp.concatenate([prev, cur], axis=0)

            fresh = []
            qs = [q_ref[rows_of(r0), :].astype(BF16) for r0, _ in blocks]
            ks = [with_prev(k_ref, kp_ref, k_cache, r0, first) for r0, first in blocks]
            ss = [lax.dot_general(q, k, NT_DIMS, preferred_element_type=F32)
                  + bias_buf[1 if first else 0] for q, k, (_, first) in zip(qs, ks, blocks)]
            ms = [jnp.max(s, axis=-1, keepdims=True) for s in ss]
            ps = [jnp.exp(s - m).astype(BF16) for s, m in zip(ss, ms)]
            vs = [with_prev(v_ref, vp_ref, v_cache, r0, first) for r0, first in blocks]
            accs = [jnp.dot(p, jnp.concatenate([v, ones_v], axis=1), preferred_element_type=F32)
                    for p, v in zip(ps, vs)]
            for cache, r0, cur in fresh:
                cache[r0] = cur
            for acc, m, (r0, _) in zip(accs, ms, blocks):
                l = acc[:, HEAD_DIM:]
                o = acc[:, :HEAD_DIM] * (1.0 / l)
                lse = m + jnp.log(l)
                rows = rows_of(r0)
                if g is not None:
                    o_buf[g, rows, :] = o
                    c_buf[g, rows, :] = lse
                else:
                    cs = [c_buf[i, rows, :] for i in range(n_strided)]
                    c_max = functools.reduce(jnp.maximum, cs, lse)
                    w = jnp.exp(lse - c_max)
                    num, den = w * o, w
                    for i in range(n_strided):
                        w = jnp.exp(cs[i] - c_max)
                        num = num + w * o_buf[i, rows, :]
                        den = den + w
                    o_ref[rows, :] = _rms(num / den, gn).astype(o_ref.dtype)

        def loop(lo, hi, body):
            def step(it, carry):
                body(it)
                return carry
            lax.fori_loop(lo, hi, step, 0)

        if nb_per_res > G:
            for res in range(d):
                group(res, 0, True)
                loop(1, nb_per_res // G, lambda it: group(res, it * G, False))
        elif nb_per_res == G:
            loop(0, d, lambda res: group(res, 0, True))
        else:
            per_group = G // nb_per_res
            loop(0, d // per_group, lambda it: group(it * per_group, 0, True))

    for g, d in enumerate(strided):
        run_branch(d, g)
    run_branch(1, None)


def _attn(z, g_norm, batch, seq):
    TQ = ATTN_TILE
    nt = seq // TQ
    nh = N_HEADS
    n_strided = len(DILATIONS) - 1

    def col(off, back=0):
        return pl.BlockSpec((TQ, HEAD_DIM),
                            lambda b, h, t: (b * nt + jnp.maximum(t - back, 0), off + h))

    return pl.pallas_call(
        _attn_kernel,
        grid=(batch, nh, nt),
        in_specs=[col(4 * nh), col(5 * nh), col(5 * nh, 1), col(6 * nh), col(6 * nh, 1),
                  pl.BlockSpec((1, HEAD_DIM), lambda b, h, t: (0, 0))],
        out_specs=pl.BlockSpec((TQ, HEAD_DIM), lambda b, h, t: (b * nt + t, h)),
        out_shape=jax.ShapeDtypeStruct((batch * seq, GROUP_WIDTH), BF16),
        scratch_shapes=[pltpu.VMEM((n_strided, TQ, HEAD_DIM), F32),
                        pltpu.VMEM((n_strided, TQ, HEAD_DIM), F32),
                        pltpu.VMEM((2, ATTN_BLOCK, 2 * ATTN_BLOCK), F32),
                        pltpu.VMEM((max(DILATIONS), ATTN_BLOCK, HEAD_DIM), BF16),
                        pltpu.VMEM((max(DILATIONS), ATTN_BLOCK, HEAD_DIM), BF16)],
        compiler_params=_params(("parallel", "parallel", "arbitrary")),
        name="dilated_attn",
    )(z, z, z, z, z, g_norm)


def _out_proj_kernel(oh_ref, oa_ref, w_ref, h_ref, g_ref, h_out_ref, u_out_ref):
    half = oh_ref.shape[1]
    acc = jnp.dot(oh_ref[...], w_ref[pl.ds(0, half), :], preferred_element_type=F32)
    acc = acc + jnp.dot(oa_ref[...], w_ref[pl.ds(half, half), :], preferred_element_type=F32)
    h = h_ref[...] + acc
    h_out_ref[...] = h
    u_out_ref[...] = _rms(h, g_ref[...]).astype(u_out_ref.dtype)


def _out_proj(oh, oa, w_rows, layer, h, gain, tm=512):
    m, d = h.shape
    half = oh.shape[1]
    row = lambda i: (i, 0)
    fixed = lambda i: (0, 0)
    return pl.pallas_call(
        _out_proj_kernel,
        grid=(m // tm,),
        in_specs=[pl.BlockSpec((tm, half), row), pl.BlockSpec((tm, half), row),
                  pl.BlockSpec((2 * half, d), lambda i: (layer, 0)), pl.BlockSpec((tm, d), row),
                  pl.BlockSpec((1, d), fixed)],
        out_specs=[pl.BlockSpec((tm, d), row), pl.BlockSpec((tm, d), row)],
        out_shape=[jax.ShapeDtypeStruct((m, d), F32), jax.ShapeDtypeStruct((m, d), BF16)],
        compiler_params=_params(("parallel",)),
        name="out_proj",
    )(oh, oa, w_rows, h, gain)


def _mlp_kernel(u_ref, w1_ref, w2_ref, h_ref, o_ref):
    f = pl.program_id(1)

    def step(base_ref):
        a = jnp.maximum(jnp.dot(u_ref[...], w1_ref[...], preferred_element_type=F32), 0.0)
        o_ref[...] = base_ref[...] + jnp.dot((a * a).astype(BF16), w2_ref[...],
                                             preferred_element_type=F32)

    @pl.when(f == 0)
    def _():
        step(h_ref)

    @pl.when(f > 0)
    def _():
        step(o_ref)


MLP_TF = 512


def _mlp(u, w1_rows, w2_rows, layer, h, tm=1024, tf=MLP_TF):
    m, d = h.shape
    nf = w1_rows.shape[1] // tf
    return pl.pallas_call(
        _mlp_kernel,
        grid=(m // tm, nf),
        in_specs=[pl.BlockSpec((tm, d), lambda i, f: (i, 0)),
                  pl.BlockSpec((d, tf), lambda i, f: (layer, f)),
                  pl.BlockSpec((tf, d), lambda i, f: (layer * nf + f, 0)),
                  pl.BlockSpec((tm, d), lambda i, f: (i, 0))],
        out_specs=pl.BlockSpec((tm, d), lambda i, f: (i, 0)),
        out_shape=jax.ShapeDtypeStruct((m, d), F32),
        compiler_params=_params(("parallel", "arbitrary")),
        name="mlp",
    )(u, w1_rows, w2_rows, h)


def _ple_kernel(last, h_ref, p_ref, wg_ref, wp_ref, gp_ref, gn_ref, *out_refs):
    h = h_ref[...]
    u = _rms(h, gp_ref[...]).astype(BF16)
    zg = jnp.dot(u, wg_ref[...], preferred_element_type=F32)
    gate = 1.0 / (1.0 + jnp.exp(-zg))
    pe = jnp.dot(p_ref[...].astype(BF16), wp_ref[...], preferred_element_type=F32)
    h = h + pe * gate
    nxt = _rms(h, gn_ref[...])
    if last:
        out_refs[0][...] = nxt
    else:
        out_refs[0][...] = h
        out_refs[1][...] = nxt.astype(out_refs[1].dtype)


def _ple(h, p, layer, wg_rows, wp_rows, g_ple, g_next, last, tm=512):
    m, d = h.shape
    pd = p.shape[1]
    row = lambda i: (i, 0)
    fixed = lambda i: (0, 0)
    of_layer = lambda i: (layer, 0)
    p_row = lambda i: (layer * (m // tm) + i, 0)
    if last:
        out_specs = [pl.BlockSpec((tm, d), row)]
        out_shape = [jax.ShapeDtypeStruct((m, d), F32)]
    else:
        out_specs = [pl.BlockSpec((tm, d), row), pl.BlockSpec((tm, d), row)]
        out_shape = [jax.ShapeDtypeStruct((m, d), F32), jax.ShapeDtypeStruct((m, d), BF16)]
    return pl.pallas_call(
        functools.partial(_ple_kernel, last),
        grid=(m // tm,),
        in_specs=[pl.BlockSpec((tm, d), row), pl.BlockSpec((tm, pd), p_row),
                  pl.BlockSpec((d, d), of_layer), pl.BlockSpec((pd, d), of_layer),
                  pl.BlockSpec((1, d), fixed), pl.BlockSpec((1, d), fixed)],
        out_specs=out_specs,
        out_shape=out_shape,
        compiler_params=_params(("parallel",)),
        name="ple",
    )(h, p, wg_rows, wp_rows, g_ple, g_next)


def kernel(x, p, positions, norm1, w_in, lb_param, hgrn_norm, attn_norm, w_out, norm2, w1, w2,
           ple_norm, w_pg, w_pp, final_norm):
    batch, seq, d = x.shape
    depth, _, in_cols = w_in.shape
    m = batch * seq
    assert seq % ATTN_TILE == 0 and d == 2 * GROUP_WIDTH

    w_in_r = w_in.reshape(depth * d, in_cols)
    w1_r, w2_r = _rows_bf16(w1), _rows_bf16(w2)
    w_out_r, w_pg_r, w_pp_r = _rows_bf16(w_out), _rows_bf16(w_pg), _rows_bf16(w_pp)

    cos, sin = _rope_tables(positions)
    h = x.reshape(m, d)
    p_rows = p.reshape(depth * m, p.shape[-1])
    u = _norm(h, norm1[0].reshape(1, d), BF16)
    out = None
    for i in range(depth):
        last = i == depth - 1
        z = _in_proj(u, w_in_r, i, cos, sin)
        o_hgrn = _hgrn(z, lb_param, hgrn_norm[i].reshape(1, HEAD_DIM), i, batch, seq)
        o_attn = _attn(z, attn_norm[i].reshape(1, HEAD_DIM), batch, seq)
        h, u2 = _out_proj(o_hgrn, o_attn, w_out_r, i, h, norm2[i].reshape(1, d))
        h = _mlp(u2, w1_r, w2_r, i, h)
        g_next = final_norm if last else norm1[i + 1]
        res = _ple(h, p_rows, i, w_pg_r, w_pp_r, ple_norm[i].reshape(1, d),
                   g_next.reshape(1, d), last)
        if last:
            out = res[0]
        else:
            h, u = res
    return out.reshape(batch, seq, d)
```

```python
import functools

import jax
import jax.numpy as jnp
from jax import lax
from jax.experimental import pallas as pl
from jax.experimental.pallas import tpu as pltpu

F32 = jnp.float32
BF16 = jnp.bfloat16

HEAD_DIM = 128
N_HEADS = 8
GROUP_WIDTH = N_HEADS * HEAD_DIM
ROT_DIM = HEAD_DIM // 4
ROPE_THETA = 500000.0
NORM_EPS = 1e-6
HGRN_CHUNK = 64
HGRN_GROUP = 8
HGRN_SUB = 16
HGRN_MAX_HALF_DECAY = 60.0
ATTN_BLOCK = 128
ATTN_TILE = 2048
ATTN_GROUP = 8
DILATIONS = (1, 4, 16)
VMEM_LIMIT = 56 * 1024 * 1024

NT_DIMS = (((1,), (1,)), ((), ()))
TN_DIMS = (((0,), (0,)), ((), ()))


def _params(semantics):
    return pltpu.CompilerParams(dimension_semantics=semantics, vmem_limit_bytes=VMEM_LIMIT)


def _rms(x, gain):
    ms = jnp.mean(x * x, axis=-1, keepdims=True)
    return x * lax.rsqrt(ms + NORM_EPS) * gain


def _silu(x):
    return x / (1.0 + jnp.exp(-x))


def _norm_kernel(x_ref, g_ref, o_ref):
    o_ref[...] = _rms(x_ref[...], g_ref[...]).astype(o_ref.dtype)


def _norm(x, gain, out_dtype, tm=512):
    m, d = x.shape
    return pl.pallas_call(
        _norm_kernel,
        grid=(m // tm,),
        in_specs=[pl.BlockSpec((tm, d), lambda i: (i, 0)),
                  pl.BlockSpec((1, d), lambda i: (0, 0))],
        out_specs=pl.BlockSpec((tm, d), lambda i: (i, 0)),
        out_shape=jax.ShapeDtypeStruct((m, d), out_dtype),
        compiler_params=_params(("parallel",)),
        name="rmsnorm",
    )(x, gain)


def _rope_kernel(pos_ref, invf_ref, cos_ref, sin_ref):
    ang = pos_ref[...] * invf_ref[...]
    lane = lax.broadcasted_iota(jnp.int32, ang.shape, 1)
    s = jnp.sin(ang)
    cos_ref[...] = jnp.cos(ang)
    sin_ref[...] = jnp.where(lane < HEAD_DIM // 2, -s, s)


def _rope_tables(positions, tr=1024):
    n = positions.size
    pos = positions.astype(F32).reshape(n, 1)
    half = ROT_DIM // 2
    inv = 1.0 / (ROPE_THETA ** (jnp.arange(0, ROT_DIM, 2, dtype=F32) / ROT_DIM))
    gap = jnp.zeros((HEAD_DIM // 2 - half,), F32)
    invf = jnp.concatenate([inv, gap, inv, gap]).reshape(1, HEAD_DIM)
    return pl.pallas_call(
        _rope_kernel,
        grid=(n // tr,),
        in_specs=[pl.BlockSpec((tr, 1), lambda i: (i, 0)),
                  pl.BlockSpec((1, HEAD_DIM), lambda i: (0, 0))],
        out_specs=[pl.BlockSpec((tr, HEAD_DIM), lambda i: (i, 0))] * 2,
        out_shape=[jax.ShapeDtypeStruct((n, HEAD_DIM), F32)] * 2,
        compiler_params=_params(("parallel",)),
        name="rope_tables",
    )(pos, invf)


def _rows_bf16(w):
    return w.astype(BF16).reshape(-1, w.shape[-1])


def _pair_rotary_lanes(w):
    half = ROT_DIM // 2
    lane = lax.broadcasted_iota(jnp.int32, w.shape, 1)
    up = pltpu.roll(w, HEAD_DIM - half, axis=1)
    down = pltpu.roll(w, HEAD_DIM // 2 - half, axis=1)
    moved = jnp.where(lane < HEAD_DIM // 2, up, down)
    keep = jnp.logical_or(lane < half, lane >= HEAD_DIM // 2 + half)
    return jnp.where(keep, w, moved)


def _rotary(x, cos, sin_signed):
    return x * cos + pltpu.roll(x, HEAD_DIM // 2, axis=1) * sin_signed


def _in_proj_kernel(q_tile, k_tile, x_ref, w_ref, cos_ref, sin_ref, o_ref, wb_ref):
    j = pl.program_id(0)
    first_row_tile = pl.program_id(1) == 0
    rotated = jnp.logical_or(j == q_tile, j == k_tile)
    plain = jnp.logical_not(rotated)
    heads = [slice(h * HEAD_DIM, (h + 1) * HEAD_DIM) for h in range(o_ref.shape[1] // HEAD_DIM)]

    @pl.when(jnp.logical_and(first_row_tile, plain))
    def _():
        wb_ref[...] = w_ref[...].astype(BF16)

    @pl.when(jnp.logical_and(first_row_tile, rotated))
    def _():
        for cols in heads:
            wb_ref[:, cols] = _pair_rotary_lanes(w_ref[:, cols]).astype(BF16)

    @pl.when(plain)
    def _():
        o_ref[...] = jnp.dot(x_ref[...], wb_ref[...], preferred_element_type=F32)

    @pl.when(rotated)
    def _():
        z = jnp.dot(x_ref[...], wb_ref[...], preferred_element_type=F32)
        scale = jnp.where(j == q_tile, HEAD_DIM ** -0.5, 1.0)
        cos = cos_ref[...] * scale
        sin = sin_ref[...] * scale
        for cols in heads:
            o_ref[:, cols] = _rotary(z[:, cols], cos, sin)


def _in_proj(u, w_rows, layer, cos, sin, tm=1024, tn=GROUP_WIDTH):
    m, k = u.shape
    n = w_rows.shape[1]
    tab = pl.BlockSpec((tm, HEAD_DIM), lambda j, i: (i, 0))
    return pl.pallas_call(
        functools.partial(_in_proj_kernel, 4, 5),
        grid=(n // tn, m // tm),
        in_specs=[pl.BlockSpec((tm, k), lambda j, i: (i, 0)),
                  pl.BlockSpec((k, tn), lambda j, i: (layer, j)), tab, tab],
        out_specs=pl.BlockSpec((tm, tn), lambda j, i: (i, j)),
        out_shape=jax.ShapeDtypeStruct((m, n), F32),
        scratch_shapes=[pltpu.VMEM((k, tn), BF16)],
        compiler_params=_params(("parallel", "arbitrary")),
        name="in_proj",
    )(u, w_rows, cos, sin)


def _hgrn_kernel(layer, zq_ref, zf_ref, zi_ref, zg_ref, lbp_ref, gn_ref, o_ref,
                 st_ref, q_s, k_s, b_s):
    C, SB = HGRN_CHUNK, HGRN_SUB
    tb = zq_ref.shape[0]
    n_chunks = tb // C
    mid = C // 2 - 1

    @pl.when(pl.program_id(2) == 0)
    def _():
        st_ref[...] = jnp.zeros_like(st_ref)

    lbp = lbp_ref[...]
    e = jnp.exp(lbp - jnp.max(lbp, axis=0, keepdims=True))
    sm = e / jnp.sum(e, axis=0, keepdims=True)
    lb = jnp.zeros((1, HEAD_DIM), F32)
    for j in range(1, layer + 1):
        lb = lb + sm[j:j + 1, :]
    lb_pos = lb > 0.0
    log1m_lb = jnp.log1p(-lb)
    one_m_lb = 1.0 - lb
    gn = gn_ref[...]

    row = lax.broadcasted_iota(jnp.int32, (C, C), 0)
    col = lax.broadcasted_iota(jnp.int32, (C, C), 1)
    causal = row >= col
    tril = jnp.where(causal, 1.0, 0.0).astype(BF16)

    def log_decay(rows):
        zf = zf_ref[rows, :]
        en = jnp.exp(-jnp.abs(zf))
        one_p = 1.0 + en
        rcp = 1.0 / one_p
        small = en * rcp
        nonneg = zf >= 0.0
        k_s[rows, :] = one_m_lb * jnp.where(nonneg, small, rcp)
        log_sig = jnp.minimum(zf, 0.0) - jnp.log(one_p)
        if layer == 0:
            return log_sig
        f = lb + one_m_lb * jnp.where(nonneg, rcp, small)
        return jnp.where(lb_pos, jnp.log(f), log1m_lb + log_sig)

    worst = jnp.zeros((1, HEAD_DIM), F32)
    for g0 in range(0, n_chunks, HGRN_GROUP):
        chunks = [pl.ds(c * C, C) for c in range(g0, min(g0 + HGRN_GROUP, n_chunks))]
        lfs = [log_decay(rows) for rows in chunks]
        his = [lf.astype(BF16) for lf in lfs]
        los = [(lf - hi.astype(F32)).astype(BF16) for lf, hi in zip(lfs, his)]
        bbs = [jnp.dot(tril, jnp.concatenate([hi, lo], axis=1), preferred_element_type=F32)
               for hi, lo in zip(his, los)]
        for rows, bb in zip(chunks, bbs):
            b = bb[:, :HEAD_DIM] + bb[:, HEAD_DIM:]
            b_s[rows, :] = b
            worst = jnp.maximum(worst, jnp.maximum(-b[mid:mid + 1, :],
                                                   b[mid:mid + 1, :] - b[C - 1:C, :]))
        for rows in chunks:
            q_s[rows, :] = _silu(zq_ref[rows, :])
    factorisable = jnp.max(worst) <= HGRN_MAX_HALF_DECAY

    def finish(o, rows):
        o = _rms(o, gn) * _silu(zg_ref[rows, :])
        o_ref[rows, :] = o.astype(o_ref.dtype)

    def state_step(st, vb, kk, b):
        b_end = b[C - 1:C, :]
        kd = (kk * jnp.exp(b_end - b)).astype(BF16)
        return st * jnp.exp(b_end) + lax.dot_general(vb, kd, TN_DIMS, preferred_element_type=F32)

    @pl.when(factorisable)
    def _():
        st = st_ref[...]
        for g0 in range(0, n_chunks, HGRN_GROUP):
            chunks = [pl.ds(c * C, C) for c in range(g0, min(g0 + HGRN_GROUP, n_chunks))]
            bs = [b_s[rows, :] for rows in chunks]
            rs = [b[mid:mid + 1, :] for b in bs]
            q_mids = [q_s[rows, :] * jnp.exp(b - r) for rows, b, r in zip(chunks, bs, rs)]
            k_mids = [k_s[rows, :] * jnp.exp(r - b) for rows, b, r in zip(chunks, bs, rs)]
            vbs = [zi_ref[rows, :].astype(BF16) for rows in chunks]
            scores = [lax.dot_general(qm.astype(BF16), km.astype(BF16), NT_DIMS,
                                      preferred_element_type=F32)
                      for qm, km in zip(q_mids, k_mids)]
            kds = [(km * jnp.exp(b[C - 1:C, :] - r)).astype(BF16)
                   for km, b, r in zip(k_mids, bs, rs)]
            upds = [lax.dot_general(vb, kd, TN_DIMS, preferred_element_type=F32)
                    for vb, kd in zip(vbs, kds)]
            qes = [(qm * jnp.exp(r)).astype(BF16) for qm, r in zip(q_mids, rs)]
            intras = [jnp.dot(jnp.where(causal, a, 0.0).astype(BF16), vb,
                              preferred_element_type=F32) for a, vb in zip(scores, vbs)]
            outs = []
            for qe, b, upd, intra in zip(qes, bs, upds, intras):
                outs.append(intra + lax.dot_general(qe, st.astype(BF16), NT_DIMS,
                                                    preferred_element_type=F32))
                st = st * jnp.exp(b[C - 1:C, :]) + upd
            for o, rows in zip(outs, chunks):
                finish(o, rows)
        st_ref[...] = st

    @pl.when(jnp.logical_not(factorisable))
    def _():
        ones = jnp.ones((HEAD_DIM, HEAD_DIM), BF16)
        sub_t = lax.broadcasted_iota(jnp.int32, (SB, HEAD_DIM), 0)

        def chunk(c, carry):
            r0 = pl.multiple_of(c * C, C)
            rows = pl.ds(r0, C)
            q, kk, b = q_s[rows, :], k_s[rows, :], b_s[rows, :]
            v = zi_ref[rows, :]
            vb = v.astype(BF16)
            st = st_ref[...]
            o_inter = lax.dot_general((q * jnp.exp(b)).astype(BF16), st.astype(BF16), NT_DIMS,
                                      preferred_element_type=F32)
            outs = []
            for i in range(C // SB):
                lo_r = i * SB
                q_i = q[lo_r:lo_r + SB, :]
                b_i = b[lo_r:lo_r + SB, :]
                o_i = o_inter[lo_r:lo_r + SB, :]
                if i > 0:
                    bref = b[lo_r - 1:lo_r, :]
                    qs = (q_i * jnp.exp(b_i - bref)).astype(BF16)
                    ks = (kk[:lo_r, :] * jnp.exp(bref - b[:lo_r, :])).astype(BF16)
                    a = lax.dot_general(qs, ks, NT_DIMS, preferred_element_type=F32)
                    o_i = o_i + jnp.dot(a.astype(BF16), vb[:lo_r, :], preferred_element_type=F32)
                slabs = []
                for s in range(SB):
                    b_row = b_s[pl.ds(r0 + lo_r + s, 1), :]
                    k_row = k_s[pl.ds(r0 + lo_r + s, 1), :]
                    w = q_i * k_row * jnp.exp(jnp.minimum(b_i - b_row, 0.0))
                    slabs.append(jnp.where(sub_t >= s, w, 0.0).astype(BF16))
                red = jnp.dot(jnp.concatenate(slabs, axis=0), ones, preferred_element_type=F32)
                for s in range(SB):
                    o_i = o_i + red[s * SB:(s + 1) * SB, :] * v[lo_r + s:lo_r + s + 1, :]
                outs.append(o_i)
            st_ref[...] = state_step(st, vb, kk, b)
            finish(jnp.concatenate(outs, axis=0), rows)
            return carry

        lax.fori_loop(0, n_chunks, chunk, 0)


def _hgrn(z, lb_param, g_norm, layer, batch, seq, tb=1024):
    nt = seq // tb
    nh = N_HEADS

    def col(off):
        return pl.BlockSpec((tb, HEAD_DIM), lambda b, h, t, off=off: (b * nt + t, off + h))

    return pl.pallas_call(
        functools.partial(_hgrn_kernel, layer),
        grid=(batch, nh, nt),
        in_specs=[col(0), col(nh), col(2 * nh), col(3 * nh),
                  pl.BlockSpec((lb_param.shape[0], HEAD_DIM), lambda b, h, t: (0, h)),
                  pl.BlockSpec((1, HEAD_DIM), lambda b, h, t: (0, 0))],
        out_specs=pl.BlockSpec((tb, HEAD_DIM), lambda b, h, t: (b * nt + t, h)),
        out_shape=jax.ShapeDtypeStruct((batch * seq, GROUP_WIDTH), BF16),
        scratch_shapes=[pltpu.VMEM((HEAD_DIM, HEAD_DIM), F32),
                        pltpu.VMEM((tb, HEAD_DIM), F32),
                        pltpu.VMEM((tb, HEAD_DIM), F32),
                        pltpu.VMEM((tb, HEAD_DIM), F32)],
        compiler_params=_params(("parallel", "parallel", "arbitrary")),
        name="hgrn2",
    )(z, z, z, z, lb_param, g_norm)


def _attn_kernel(q_ref, k_ref, kp_ref, v_ref, vp_ref, gn_ref, o_ref,
                 o_buf, c_buf, bias_buf, k_cache, v_cache):
    TQ, Q, G = ATTN_TILE, ATTN_BLOCK, ATTN_GROUP
    t = pl.program_id(2)

    @pl.when(t == 0)
    def _():
        k_cache[...] = jnp.zeros_like(k_cache)
        v_cache[...] = jnp.zeros_like(v_cache)

    qi = lax.broadcasted_iota(jnp.int32, (Q, 2 * Q), 0)
    kj = lax.broadcasted_iota(jnp.int32, (Q, 2 * Q), 1)
    dist = kj - qi
    band = jnp.where((dist >= 0) & (dist <= Q), 0.0, -jnp.inf)
    first_key = jnp.where(t > 0, 0, Q)
    bias_buf[0] = band
    bias_buf[1] = jnp.where(kj >= first_key, band, -jnp.inf)
    ones_v = jnp.ones((2 * Q, HEAD_DIM), BF16)
    gn = gn_ref[...]
    strided = [d for d in DILATIONS if d > 1]
    n_strided = len(strided)

    def run_branch(d, g):
        nb_per_res = TQ // (Q * d)

        def rows_of(start):
            if d > 1:
                return pl.ds(start, Q, stride=d)
            return pl.ds(start if isinstance(start, int) else pl.multiple_of(start, Q), Q)

        def group(res, nb0, at_start):
            blocks = []
            for j in range(G):
                if nb_per_res >= G:
                    blocks.append((res + d * Q * (nb0 + j), at_start and j == 0))
                else:
                    nb = j % nb_per_res
                    blocks.append((res + j // nb_per_res + d * Q * nb, nb == 0))

            def with_prev(cur_ref, prev_ref, cache, r0, first):
                cur = cur_ref[rows_of(r0), :].astype(BF16)
                if nb_per_res == 1:
                    prev = cache[r0]
                    fresh.append((cache, r0, cur))
                elif first:
                    prev = prev_ref[rows_of(r0 + TQ - d * Q), :].astype(BF16)
                else:
                    prev = cur_ref[rows_of(r0 - d * Q), :].astype(BF16)
                return jnp.concatenate([prev, cur], axis=0)

            fresh = []
            qs = [q_ref[rows_of(r0), :].astype(BF16) for r0, _ in blocks]
            ks = [with_prev(k_ref, kp_ref, k_cache, r0, first) for r0, first in blocks]
            ss = [lax.dot_general(q, k, NT_DIMS, preferred_element_type=F32)
                  + bias_buf[1 if first else 0] for q, k, (_, first) in zip(qs, ks, blocks)]
            ms = [jnp.max(s, axis=-1, keepdims=True) for s in ss]
            ps = [jnp.exp(s - m).astype(BF16) for s, m in zip(ss, ms)]
            vs = [with_prev(v_ref, vp_ref, v_cache, r0, first) for r0, first in blocks]
            accs = [jnp.dot(p, jnp.concatenate([v, ones_v], axis=1), preferred_element_type=F32)
                    for p, v in zip(ps, vs)]
            for cache, r0, cur in fresh:
                cache[r0] = cur
            out_rows = [rows_of(r0) for r0, _ in blocks]
            ls = [acc[:, HEAD_DIM:] for acc in accs]
            os_ = [acc[:, :HEAD_DIM] * (1.0 / l) for acc, l in zip(accs, ls)]
            lses = [m + jnp.log(l) for m, l in zip(ms, ls)]
            if g is not None:
                for rows, o, lse in zip(out_rows, os_, lses):
                    o_buf[g, rows, :] = o
                    c_buf[g, rows, :] = lse
            else:
                cs = [[c_buf[i, rows, :] for i in range(n_strided)] for rows in out_rows]
                c_maxs = [functools.reduce(jnp.maximum, c, lse) for c, lse in zip(cs, lses)]
                ws = [[jnp.exp(x - c_max) for x in [lse] + c]
                      for c, lse, c_max in zip(cs, lses, c_maxs)]
                nums = [w[0] * o + sum(w[i + 1] * o_buf[i, rows, :] for i in range(n_strided))
                        for w, o, rows in zip(ws, os_, out_rows)]
                dens = [functools.reduce(lambda a, b: a + b, w) for w in ws]
                merged = [num / den for num, den in zip(nums, dens)]
                for rows, o in zip(out_rows, merged):
                    o_ref[rows, :] = _rms(o, gn).astype(o_ref.dtype)

        def loop(lo, hi, body):
            def step(it, carry):
                body(it)
                return carry
            lax.fori_loop(lo, hi, step, 0)

        if nb_per_res > G:
            for res in range(d):
                group(res, 0, True)
                loop(1, nb_per_res // G, lambda it: group(res, it * G, False))
        elif nb_per_res == G:
            loop(0, d, lambda res: group(res, 0, True))
        else:
            per_group = G // nb_per_res
            loop(0, d // per_group, lambda it: group(it * per_group, 0, True))

    for g, d in enumerate(strided):
        run_branch(d, g)
    run_branch(1, None)


def _attn(z, g_norm, batch, seq):
    TQ = ATTN_TILE
    nt = seq // TQ
    nh = N_HEADS
    n_strided = len(DILATIONS) - 1

    def col(off, back=0):
        return pl.BlockSpec((TQ, HEAD_DIM),
                            lambda b, h, t: (b * nt + jnp.maximum(t - back, 0), off + h))

    return pl.pallas_call(
        _attn_kernel,
        grid=(batch, nh, nt),
        in_specs=[col(4 * nh), col(5 * nh), col(5 * nh, 1), col(6 * nh), col(6 * nh, 1),
                  pl.BlockSpec((1, HEAD_DIM), lambda b, h, t: (0, 0))],
        out_specs=pl.BlockSpec((TQ, HEAD_DIM), lambda b, h, t: (b * nt + t, h)),
        out_shape=jax.ShapeDtypeStruct((batch * seq, GROUP_WIDTH), BF16),
        scratch_shapes=[pltpu.VMEM((n_strided, TQ, HEAD_DIM), F32),
                        pltpu.VMEM((n_strided, TQ, HEAD_DIM), F32),
                        pltpu.VMEM((2, ATTN_BLOCK, 2 * ATTN_BLOCK), F32),
                        pltpu.VMEM((max(DILATIONS), ATTN_BLOCK, HEAD_DIM), BF16),
                        pltpu.VMEM((max(DILATIONS), ATTN_BLOCK, HEAD_DIM), BF16)],
        compiler_params=_params(("parallel", "parallel", "arbitrary")),
        name="dilated_attn",
    )(z, z, z, z, z, g_norm)


def _out_proj_kernel(oh_ref, oa_ref, w_ref, h_ref, g_ref, h_out_ref, u_out_ref):
    half = oh_ref.shape[1]
    acc = jnp.dot(oh_ref[...], w_ref[pl.ds(0, half), :], preferred_element_type=F32)
    acc = acc + jnp.dot(oa_ref[...], w_ref[pl.ds(half, half), :], preferred_element_type=F32)
    h = h_ref[...] + acc
    h_out_ref[...] = h
    u_out_ref[...] = _rms(h, g_ref[...]).astype(u_out_ref.dtype)


def _out_proj(oh, oa, w_rows, layer, h, gain, tm=512):
    m, d = h.shape
    half = oh.shape[1]
    row = lambda i: (i, 0)
    fixed = lambda i: (0, 0)
    return pl.pallas_call(
        _out_proj_kernel,
        grid=(m // tm,),
        in_specs=[pl.BlockSpec((tm, half), row), pl.BlockSpec((tm, half), row),
                  pl.BlockSpec((2 * half, d), lambda i: (layer, 0)), pl.BlockSpec((tm, d), row),
                  pl.BlockSpec((1, d), fixed)],
        out_specs=[pl.BlockSpec((tm, d), row), pl.BlockSpec((tm, d), row)],
        out_shape=[jax.ShapeDtypeStruct((m, d), F32), jax.ShapeDtypeStruct((m, d), BF16)],
        compiler_params=_params(("parallel",)),
        name="out_proj",
    )(oh, oa, w_rows, h, gain)


def _mlp_kernel(u_ref, w1_ref, w2_ref, h_ref, o_ref):
    f = pl.program_id(1)

    def step(base_ref):
        a = jnp.maximum(jnp.dot(u_ref[...], w1_ref[...], preferred_element_type=F32), 0.0)
        o_ref[...] = base_ref[...] + jnp.dot((a * a).astype(BF16), w2_ref[...],
                                             preferred_element_type=F32)

    @pl.when(f == 0)
    def _():
        step(h_ref)

    @pl.when(f > 0)
    def _():
        step(o_ref)


MLP_TF = 512


def _mlp(u, w1_rows, w2_rows, layer, h, tm=1024, tf=MLP_TF):
    m, d = h.shape
    nf = w1_rows.shape[1] // tf
    return pl.pallas_call(
        _mlp_kernel,
        grid=(m // tm, nf),
        in_specs=[pl.BlockSpec((tm, d), lambda i, f: (i, 0)),
                  pl.BlockSpec((d, tf), lambda i, f: (layer, f)),
                  pl.BlockSpec((tf, d), lambda i, f: (layer * nf + f, 0)),
                  pl.BlockSpec((tm, d), lambda i, f: (i, 0))],
        out_specs=pl.BlockSpec((tm, d), lambda i, f: (i, 0)),
        out_shape=jax.ShapeDtypeStruct((m, d), F32),
        compiler_params=_params(("parallel", "arbitrary")),
        name="mlp",
    )(u, w1_rows, w2_rows, h)


def _ple_kernel(last, h_ref, p_ref, wg_ref, wp_ref, gp_ref, gn_ref, *out_refs):
    h = h_ref[...]
    u = _rms(h, gp_ref[...]).astype(BF16)
    zg = jnp.dot(u, wg_ref[...], preferred_element_type=F32)
    gate = 1.0 / (1.0 + jnp.exp(-zg))
    pe = jnp.dot(p_ref[...].astype(BF16), wp_ref[...], preferred_element_type=F32)
    h = h + pe * gate
    nxt = _rms(h, gn_ref[...])
    if last:
        out_refs[0][...] = nxt
    else:
        out_refs[0][...] = h
        out_refs[1][...] = nxt.astype(out_refs[1].dtype)


def _ple(h, p, layer, wg_rows, wp_rows, g_ple, g_next, last, tm=512):
    m, d = h.shape
    pd = p.shape[1]
    row = lambda i: (i, 0)
    fixed = lambda i: (0, 0)
    of_layer = lambda i: (layer, 0)
    p_row = lambda i: (layer * (m // tm) + i, 0)
    if last:
        out_specs = [pl.BlockSpec((tm, d), row)]
        out_shape = [jax.ShapeDtypeStruct((m, d), F32)]
    else:
        out_specs = [pl.BlockSpec((tm, d), row), pl.BlockSpec((tm, d), row)]
        out_shape = [jax.ShapeDtypeStruct((m, d), F32), jax.ShapeDtypeStruct((m, d), BF16)]
    return pl.pallas_call(
        functools.partial(_ple_kernel, last),
        grid=(m // tm,),
        in_specs=[pl.BlockSpec((tm, d), row), pl.BlockSpec((tm, pd), p_row),
                  pl.BlockSpec((d, d), of_layer), pl.BlockSpec((pd, d), of_layer),
                  pl.BlockSpec((1, d), fixed), pl.BlockSpec((1, d), fixed)],
        out_specs=out_specs,
        out_shape=out_shape,
        compiler_params=_params(("parallel",)),
        name="ple",
    )(h, p, wg_rows, wp_rows, g_ple, g_next)


def kernel(x, p, positions, norm1, w_in, lb_param, hgrn_norm, attn_norm, w_out, norm2, w1, w2,
           ple_norm, w_pg, w_pp, final_norm):
    batch, seq, d = x.shape
    depth, _, in_cols = w_in.shape
    m = batch * seq
    assert seq % ATTN_TILE == 0 and d == 2 * GROUP_WIDTH

    w_in_r = w_in.reshape(depth * d, in_cols)
    w1_r, w2_r = _rows_bf16(w1), _rows_bf16(w2)
    w_out_r, w_pg_r, w_pp_r = _rows_bf16(w_out), _rows_bf16(w_pg), _rows_bf16(w_pp)

    cos, sin = _rope_tables(positions)
    h = x.reshape(m, d)
    p_rows = p.reshape(depth * m, p.shape[-1])
    u = _norm(h, norm1[0].reshape(1, d), BF16)
    out = None
    for i in range(depth):
        last = i == depth - 1
        z = _in_proj(u, w_in_r, i, cos, sin)
        o_hgrn = _hgrn(z, lb_param, hgrn_norm[i].reshape(1, HEAD_DIM), i, batch, seq)
        o_attn = _attn(z, attn_norm[i].reshape(1, HEAD_DIM), batch, seq)
        h, u2 = _out_proj(o_hgrn, o_attn, w_out_r, i, h, norm2[i].reshape(1, d))
        h = _mlp(u2, w1_r, w2_r, i, h)
        g_next = final_norm if last else norm1[i + 1]
        res = _ple(h, p_rows, i, w_pg_r, w_pp_r, ple_norm[i].reshape(1, d),
                   g_next.reshape(1, d), last)
        if last:
            out = res[0]
        else:
            h, u = res
    return out.reshape(batch, seq, d)
```

```python
import functools

import jax
import jax.numpy as jnp
from jax import lax
from jax.experimental import pallas as pl
from jax.experimental.pallas import tpu as pltpu

F32 = jnp.float32
BF16 = jnp.bfloat16

HEAD_DIM = 128
N_HEADS = 8
GROUP_WIDTH = N_HEADS * HEAD_DIM
ROT_DIM = HEAD_DIM // 4
ROPE_THETA = 500000.0
NORM_EPS = 1e-6
QUERY_SCALE = HEAD_DIM ** -0.5 * 1.4426950408889634
HGRN_CHUNK = 64
HGRN_GROUP = 16
HGRN_SUB = 16
HGRN_MAX_HALF_DECAY = 60.0
ATTN_BLOCK = 128
ATTN_TILE = 2048
ATTN_GROUP = 8
DILATIONS = (1, 4, 16)
VMEM_LIMIT = 56 * 1024 * 1024

NT_DIMS = (((1,), (1,)), ((), ()))
TN_DIMS = (((0,), (0,)), ((), ()))


def _params(semantics):
    return pltpu.CompilerParams(dimension_semantics=semantics, vmem_limit_bytes=VMEM_LIMIT)


def _rms(x, gain):
    ms = jnp.mean(x * x, axis=-1, keepdims=True)
    return x * lax.rsqrt(ms + NORM_EPS) * gain


def _silu(x):
    return x / (1.0 + jnp.exp(-x))


def _norm_kernel(x_ref, g_ref, o_ref):
    o_ref[...] = _rms(x_ref[...], g_ref[...]).astype(o_ref.dtype)


def _norm(x, gain, out_dtype, tm=512):
    m, d = x.shape
    return pl.pallas_call(
        _norm_kernel,
        grid=(m // tm,),
        in_specs=[pl.BlockSpec((tm, d), lambda i: (i, 0)),
                  pl.BlockSpec((1, d), lambda i: (0, 0))],
        out_specs=pl.BlockSpec((tm, d), lambda i: (i, 0)),
        out_shape=jax.ShapeDtypeStruct((m, d), out_dtype),
        compiler_params=_params(("parallel",)),
        name="rmsnorm",
    )(x, gain)


def _rope_kernel(pos_ref, invf_ref, cos_ref, sin_ref):
    ang = pos_ref[...] * invf_ref[...]
    lane = lax.broadcasted_iota(jnp.int32, ang.shape, 1)
    s = jnp.sin(ang)
    cos_ref[...] = jnp.cos(ang)
    sin_ref[...] = jnp.where(lane < HEAD_DIM // 2, -s, s)


def _rope_tables(positions, tr=1024):
    n = positions.size
    pos = positions.astype(F32).reshape(n, 1)
    half = ROT_DIM // 2
    inv = 1.0 / (ROPE_THETA ** (jnp.arange(0, ROT_DIM, 2, dtype=F32) / ROT_DIM))
    gap = jnp.zeros((HEAD_DIM // 2 - half,), F32)
    invf = jnp.concatenate([inv, gap, inv, gap]).reshape(1, HEAD_DIM)
    return pl.pallas_call(
        _rope_kernel,
        grid=(n // tr,),
        in_specs=[pl.BlockSpec((tr, 1), lambda i: (i, 0)),
                  pl.BlockSpec((1, HEAD_DIM), lambda i: (0, 0))],
        out_specs=[pl.BlockSpec((tr, HEAD_DIM), lambda i: (i, 0))] * 2,
        out_shape=[jax.ShapeDtypeStruct((n, HEAD_DIM), F32)] * 2,
        compiler_params=_params(("parallel",)),
        name="rope_tables",
    )(pos, invf)


def _rows_bf16(w):
    return w.astype(BF16).reshape(-1, w.shape[-1])


def _pair_rotary_lanes(w):
    half = ROT_DIM // 2
    lane = lax.broadcasted_iota(jnp.int32, w.shape, 1)
    up = pltpu.roll(w, HEAD_DIM - half, axis=1)
    down = pltpu.roll(w, HEAD_DIM // 2 - half, axis=1)
    moved = jnp.where(lane < HEAD_DIM // 2, up, down)
    keep = jnp.logical_or(lane < half, lane >= HEAD_DIM // 2 + half)
    return jnp.where(keep, w, moved)


def _rotary(x, cos, sin_signed):
    return x * cos + pltpu.roll(x, HEAD_DIM // 2, axis=1) * sin_signed


def _in_proj_kernel(q_tile, k_tile, x_ref, w_ref, cos_ref, sin_ref, o_ref, wb_ref):
    j = pl.program_id(0)
    first_row_tile = pl.program_id(1) == 0
    rotated = jnp.logical_or(j == q_tile, j == k_tile)
    plain = jnp.logical_not(rotated)
    heads = [slice(h * HEAD_DIM, (h + 1) * HEAD_DIM) for h in range(o_ref.shape[1] // HEAD_DIM)]

    @pl.when(jnp.logical_and(first_row_tile, plain))
    def _():
        wb_ref[...] = w_ref[...].astype(BF16)

    @pl.when(jnp.logical_and(first_row_tile, rotated))
    def _():
        for cols in heads:
            wb_ref[:, cols] = _pair_rotary_lanes(w_ref[:, cols]).astype(BF16)

    @pl.when(plain)
    def _():
        o_ref[...] = jnp.dot(x_ref[...], wb_ref[...], preferred_element_type=F32)

    @pl.when(rotated)
    def _():
        z = jnp.dot(x_ref[...], wb_ref[...], preferred_element_type=F32)
        scale = jnp.where(j == q_tile, QUERY_SCALE, 1.0)
        cos = cos_ref[...] * scale
        sin = sin_ref[...] * scale
        for cols in heads:
            o_ref[:, cols] = _rotary(z[:, cols], cos, sin)


def _in_proj(u, w_rows, layer, cos, sin, tm=1024, tn=GROUP_WIDTH):
    m, k = u.shape
    n = w_rows.shape[1]
    tab = pl.BlockSpec((tm, HEAD_DIM), lambda j, i: (i, 0))
    return pl.pallas_call(
        functools.partial(_in_proj_kernel, 4, 5),
        grid=(n // tn, m // tm),
        in_specs=[pl.BlockSpec((tm, k), lambda j, i: (i, 0)),
                  pl.BlockSpec((k, tn), lambda j, i: (layer, j)), tab, tab],
        out_specs=pl.BlockSpec((tm, tn), lambda j, i: (i, j)),
        out_shape=jax.ShapeDtypeStruct((m, n), F32),
        scratch_shapes=[pltpu.VMEM((k, tn), BF16)],
        compiler_params=_params(("parallel", "arbitrary")),
        name="in_proj",
    )(u, w_rows, cos, sin)


def _hgrn_kernel(layer, zq_ref, zf_ref, zi_ref, zg_ref, lbp_ref, gn_ref, o_ref,
                 st_ref, q_s, k_s, b_s):
    C, SB = HGRN_CHUNK, HGRN_SUB
    tb = zq_ref.shape[0]
    n_chunks = tb // C
    mid = C // 2 - 1

    @pl.when(pl.program_id(2) == 0)
    def _():
        st_ref[...] = jnp.zeros_like(st_ref)

    lbp = lbp_ref[...]
    e = jnp.exp(lbp - jnp.max(lbp, axis=0, keepdims=True))
    sm = e / jnp.sum(e, axis=0, keepdims=True)
    lb = jnp.zeros((1, HEAD_DIM), F32)
    for j in range(1, layer + 1):
        lb = lb + sm[j:j + 1, :]
    lb_pos = lb > 0.0
    log1m_lb = jnp.log1p(-lb)
    one_m_lb = 1.0 - lb
    gn = gn_ref[...]

    row = lax.broadcasted_iota(jnp.int32, (C, C), 0)
    col = lax.broadcasted_iota(jnp.int32, (C, C), 1)
    causal = row >= col
    tril = jnp.where(causal, 1.0, 0.0).astype(BF16)

    def log_decay(rows):
        zf = zf_ref[rows, :]
        en = jnp.exp(-jnp.abs(zf))
        one_p = 1.0 + en
        rcp = 1.0 / one_p
        small = en * rcp
        nonneg = zf >= 0.0
        k_s[rows, :] = one_m_lb * jnp.where(nonneg, small, rcp)
        log_sig = jnp.minimum(zf, 0.0) - jnp.log(one_p)
        if layer == 0:
            return log_sig
        f = lb + one_m_lb * jnp.where(nonneg, rcp, small)
        return jnp.where(lb_pos, jnp.log(f), log1m_lb + log_sig)

    worst = jnp.zeros((1, HEAD_DIM), F32)
    for g0 in range(0, n_chunks, HGRN_GROUP):
        chunks = [pl.ds(c * C, C) for c in range(g0, min(g0 + HGRN_GROUP, n_chunks))]
        lfs = [log_decay(rows) for rows in chunks]
        his = [lf.astype(BF16) for lf in lfs]
        los = [(lf - hi.astype(F32)).astype(BF16) for lf, hi in zip(lfs, his)]
        bbs = [jnp.dot(tril, jnp.concatenate([hi, lo], axis=1), preferred_element_type=F32)
               for hi, lo in zip(his, los)]
        for rows, bb in zip(chunks, bbs):
            b = bb[:, :HEAD_DIM] + bb[:, HEAD_DIM:]
            b_s[rows, :] = b
            worst = jnp.maximum(worst, jnp.maximum(-b[mid:mid + 1, :],
                                                   b[mid:mid + 1, :] - b[C - 1:C, :]))
        for rows in chunks:
            q_s[rows, :] = _silu(zq_ref[rows, :])
    factorisable = jnp.max(worst) <= HGRN_MAX_HALF_DECAY

    def finish(o, rows):
        o = _rms(o, gn) * _silu(zg_ref[rows, :])
        o_ref[rows, :] = o.astype(o_ref.dtype)

    def state_step(st, vb, kk, b):
        b_end = b[C - 1:C, :]
        kd = (kk * jnp.exp(b_end - b)).astype(BF16)
        return st * jnp.exp(b_end) + lax.dot_general(vb, kd, TN_DIMS, preferred_element_type=F32)

    @pl.when(factorisable)
    def _():
        st = st_ref[...]
        for g0 in range(0, n_chunks, HGRN_GROUP):
            chunks = [pl.ds(c * C, C) for c in range(g0, min(g0 + HGRN_GROUP, n_chunks))]
            bs = [b_s[rows, :] for rows in chunks]
            rs = [b[mid:mid + 1, :] for b in bs]
            q_mids = [q_s[rows, :] * jnp.exp(b - r) for rows, b, r in zip(chunks, bs, rs)]
            k_mids = [k_s[rows, :] * jnp.exp(r - b) for rows, b, r in zip(chunks, bs, rs)]
            vbs = [zi_ref[rows, :].astype(BF16) for rows in chunks]
            scores = [lax.dot_general(qm.astype(BF16), km.astype(BF16), NT_DIMS,
                                      preferred_element_type=F32)
                      for qm, km in zip(q_mids, k_mids)]
            kds = [(km * jnp.exp(b[C - 1:C, :] - r)).astype(BF16)
                   for km, b, r in zip(k_mids, bs, rs)]
            upds = [lax.dot_general(vb, kd, TN_DIMS, preferred_element_type=F32)
                    for vb, kd in zip(vbs, kds)]
            qes = [(qm * jnp.exp(r)).astype(BF16) for qm, r in zip(q_mids, rs)]
            intras = [jnp.dot(jnp.where(causal, a, 0.0).astype(BF16), vb,
                              preferred_element_type=F32) for a, vb in zip(scores, vbs)]
            outs = []
            for qe, b, upd, intra in zip(qes, bs, upds, intras):
                outs.append(intra + lax.dot_general(qe, st.astype(BF16), NT_DIMS,
                                                    preferred_element_type=F32))
                st = st * jnp.exp(b[C - 1:C, :]) + upd
            for o, rows in zip(outs, chunks):
                finish(o, rows)
        st_ref[...] = st

    @pl.when(jnp.logical_not(factorisable))
    def _():
        ones = jnp.ones((HEAD_DIM, HEAD_DIM), BF16)
        sub_t = lax.broadcasted_iota(jnp.int32, (SB, HEAD_DIM), 0)

        def chunk(c, carry):
            r0 = pl.multiple_of(c * C, C)
            rows = pl.ds(r0, C)
            q, kk, b = q_s[rows, :], k_s[rows, :], b_s[rows, :]
            v = zi_ref[rows, :]
            vb = v.astype(BF16)
            st = st_ref[...]
            o_inter = lax.dot_general((q * jnp.exp(b)).astype(BF16), st.astype(BF16), NT_DIMS,
                                      preferred_element_type=F32)
            outs = []
            for i in range(C // SB):
                lo_r = i * SB
                q_i = q[lo_r:lo_r + SB, :]
                b_i = b[lo_r:lo_r + SB, :]
                o_i = o_inter[lo_r:lo_r + SB, :]
                if i > 0:
                    bref = b[lo_r - 1:lo_r, :]
                    qs = (q_i * jnp.exp(b_i - bref)).astype(BF16)
                    ks = (kk[:lo_r, :] * jnp.exp(bref - b[:lo_r, :])).astype(BF16)
                    a = lax.dot_general(qs, ks, NT_DIMS, preferred_element_type=F32)
                    o_i = o_i + jnp.dot(a.astype(BF16), vb[:lo_r, :], preferred_element_type=F32)
                slabs = []
                for s in range(SB):
                    b_row = b_s[pl.ds(r0 + lo_r + s, 1), :]
                    k_row = k_s[pl.ds(r0 + lo_r + s, 1), :]
                    w = q_i * k_row * jnp.exp(jnp.minimum(b_i - b_row, 0.0))
                    slabs.append(jnp.where(sub_t >= s, w, 0.0).astype(BF16))
                red = jnp.dot(jnp.concatenate(slabs, axis=0), ones, preferred_element_type=F32)
                for s in range(SB):
                    o_i = o_i + red[s * SB:(s + 1) * SB, :] * v[lo_r + s:lo_r + s + 1, :]
                outs.append(o_i)
            st_ref[...] = state_step(st, vb, kk, b)
            finish(jnp.concatenate(outs, axis=0), rows)
            return carry

        lax.fori_loop(0, n_chunks, chunk, 0)


def _hgrn(z, lb_param, g_norm, layer, batch, seq, tb=4096):
    nt = seq // tb
    nh = N_HEADS

    def col(off):
        return pl.BlockSpec((tb, HEAD_DIM), lambda b, h, t, off=off: (b * nt + t, off + h))

    return pl.pallas_call(
        functools.partial(_hgrn_kernel, layer),
        grid=(batch, nh, nt),
        in_specs=[col(0), col(nh), col(2 * nh), col(3 * nh),
                  pl.BlockSpec((lb_param.shape[0], HEAD_DIM), lambda b, h, t: (0, h)),
                  pl.BlockSpec((1, HEAD_DIM), lambda b, h, t: (0, 0))],
        out_specs=pl.BlockSpec((tb, HEAD_DIM), lambda b, h, t: (b * nt + t, h)),
        out_shape=jax.ShapeDtypeStruct((batch * seq, GROUP_WIDTH), BF16),
        scratch_shapes=[pltpu.VMEM((HEAD_DIM, HEAD_DIM), F32),
                        pltpu.VMEM((tb, HEAD_DIM), F32),
                        pltpu.VMEM((tb, HEAD_DIM), F32),
                        pltpu.VMEM((tb, HEAD_DIM), F32)],
        compiler_params=_params(("parallel", "parallel", "arbitrary")),
        name="hgrn2",
    )(z, z, z, z, lb_param, g_norm)


def _attn_kernel(q_ref, k_ref, kp_ref, v_ref, vp_ref, gn_ref, o_ref,
                 o_buf, c_buf, bias_buf, k_cache, v_cache):
    TQ, Q, G = ATTN_TILE, ATTN_BLOCK, ATTN_GROUP
    t = pl.program_id(2)

    @pl.when(t == 0)
    def _():
        k_cache[...] = jnp.zeros_like(k_cache)
        v_cache[...] = jnp.zeros_like(v_cache)

    qi = lax.broadcasted_iota(jnp.int32, (Q, 2 * Q), 0)
    kj = lax.broadcasted_iota(jnp.int32, (Q, 2 * Q), 1)
    dist = kj - qi
    band = jnp.where((dist >= 0) & (dist <= Q), 0.0, -jnp.inf)
    first_key = jnp.where(t > 0, 0, Q)
    bias_buf[0] = band
    bias_buf[1] = jnp.where(kj >= first_key, band, -jnp.inf)
    ones_v = jnp.ones((2 * Q, HEAD_DIM), BF16)
    gn = gn_ref[...]
    strided = [d for d in DILATIONS if d > 1]
    n_strided = len(strided)

    def run_branch(d, g):
        nb_per_res = TQ // (Q * d)

        def rows_of(start):
            if d > 1:
                return pl.ds(start, Q, stride=d)
            return pl.ds(start if isinstance(start, int) else pl.multiple_of(start, Q), Q)

        def group(res, nb0, at_start):
            blocks = []
            for j in range(G):
                if nb_per_res >= G:
                    blocks.append((res + d * Q * (nb0 + j), at_start and j == 0))
                else:
                    nb = j % nb_per_res
                    blocks.append((res + j // nb_per_res + d * Q * nb, nb == 0))

            def with_prev(cur_ref, prev_ref, cache, r0, first):
                cur = cur_ref[rows_of(r0), :].astype(BF16)
                if nb_per_res == 1:
                    prev = cache[r0]
                    fresh.append((cache, r0, cur))
                elif first:
                    prev = prev_ref[rows_of(r0 + TQ - d * Q), :].astype(BF16)
                else:
                    prev = cur_ref[rows_of(r0 - d * Q), :].astype(BF16)
                return jnp.concatenate([prev, cur], axis=0)

            fresh = []
            qs = [q_ref[rows_of(r0), :].astype(BF16) for r0, _ in blocks]
            ks = [with_prev(k_ref, kp_ref, k_cache, r0, first) for r0, first in blocks]
            ss = [lax.dot_general(q, k, NT_DIMS, preferred_element_type=F32)
                  + bias_buf[1 if first else 0] for q, k, (_, first) in zip(qs, ks, blocks)]
            ms = [jnp.max(s, axis=-1, keepdims=True) for s in ss]
            ps = [jnp.exp2(s - m).astype(BF16) for s, m in zip(ss, ms)]
            vs = [with_prev(v_ref, vp_ref, v_cache, r0, first) for r0, first in blocks]
            accs = [jnp.dot(p, jnp.concatenate([v, ones_v], axis=1), preferred_element_type=F32)
                    for p, v in zip(ps, vs)]
            for cache, r0, cur in fresh:
                cache[r0] = cur
            out_rows = [rows_of(r0) for r0, _ in blocks]
            ls = [acc[:, HEAD_DIM:] for acc in accs]
            os_ = [acc[:, :HEAD_DIM] * (1.0 / l) for acc, l in zip(accs, ls)]
            lses = [m + jnp.log2(l) for m, l in zip(ms, ls)]
            if g is not None:
                for rows, o, lse in zip(out_rows, os_, lses):
                    o_buf[g, rows, :] = o
                    c_buf[g, rows, :] = lse
            else:
                cs = [[c_buf[i, rows, :] for i in range(n_strided)] for rows in out_rows]
                c_maxs = [functools.reduce(jnp.maximum, c, lse) for c, lse in zip(cs, lses)]
                ws = [[jnp.exp2(x - c_max) for x in [lse] + c]
                      for c, lse, c_max in zip(cs, lses, c_maxs)]
                nums = [w[0] * o + sum(w[i + 1] * o_buf[i, rows, :] for i in range(n_strided))
                        for w, o, rows in zip(ws, os_, out_rows)]
                dens = [functools.reduce(lambda a, b: a + b, w) for w in ws]
                merged = [num / den for num, den in zip(nums, dens)]
                for rows, o in zip(out_rows, merged):
                    o_ref[rows, :] = _rms(o, gn).astype(o_ref.dtype)

        def loop(lo, hi, body):
            def step(it, carry):
                body(it)
                return carry
            lax.fori_loop(lo, hi, step, 0)

        if nb_per_res > G:
            for res in range(d):
                group(res, 0, True)
                loop(1, nb_per_res // G, lambda it: group(res, it * G, False))
        elif nb_per_res == G:
            loop(0, d, lambda res: group(res, 0, True))
        else:
            per_group = G // nb_per_res
            loop(0, d // per_group, lambda it: group(it * per_group, 0, True))

    for g, d in enumerate(strided):
        run_branch(d, g)
    run_branch(1, None)


def _attn(z, g_norm, batch, seq):
    TQ = ATTN_TILE
    nt = seq // TQ
    nh = N_HEADS
    n_strided = len(DILATIONS) - 1

    def col(off, back=0):
        return pl.BlockSpec((TQ, HEAD_DIM),
                            lambda b, h, t: (b * nt + jnp.maximum(t - back, 0), off + h))

    return pl.pallas_call(
        _attn_kernel,
        grid=(batch, nh, nt),
        in_specs=[col(4 * nh), col(5 * nh), col(5 * nh, 1), col(6 * nh), col(6 * nh, 1),
                  pl.BlockSpec((1, HEAD_DIM), lambda b, h, t: (0, 0))],
        out_specs=pl.BlockSpec((TQ, HEAD_DIM), lambda b, h, t: (b * nt + t, h)),
        out_shape=jax.ShapeDtypeStruct((batch * seq, GROUP_WIDTH), BF16),
        scratch_shapes=[pltpu.VMEM((n_strided, TQ, HEAD_DIM), F32),
                        pltpu.VMEM((n_strided, TQ, HEAD_DIM), F32),
                        pltpu.VMEM((2, ATTN_BLOCK, 2 * ATTN_BLOCK), F32),
                        pltpu.VMEM((max(DILATIONS), ATTN_BLOCK, HEAD_DIM), BF16),
                        pltpu.VMEM((max(DILATIONS), ATTN_BLOCK, HEAD_DIM), BF16)],
        compiler_params=_params(("parallel", "parallel", "arbitrary")),
        name="dilated_attn",
    )(z, z, z, z, z, g_norm)


def _out_proj_kernel(oh_ref, oa_ref, w_ref, h_ref, g_ref, h_out_ref, u_out_ref):
    half = oh_ref.shape[1]
    acc = jnp.dot(oh_ref[...], w_ref[pl.ds(0, half), :], preferred_element_type=F32)
    acc = acc + jnp.dot(oa_ref[...], w_ref[pl.ds(half, half), :], preferred_element_type=F32)
    h = h_ref[...] + acc
    h_out_ref[...] = h
    u_out_ref[...] = _rms(h, g_ref[...]).astype(u_out_ref.dtype)


def _out_proj(oh, oa, w_rows, layer, h, gain, tm=512):
    m, d = h.shape
    half = oh.shape[1]
    row = lambda i: (i, 0)
    fixed = lambda i: (0, 0)
    return pl.pallas_call(
        _out_proj_kernel,
        grid=(m // tm,),
        in_specs=[pl.BlockSpec((tm, half), row), pl.BlockSpec((tm, half), row),
                  pl.BlockSpec((2 * half, d), lambda i: (layer, 0)), pl.BlockSpec((tm, d), row),
                  pl.BlockSpec((1, d), fixed)],
        out_specs=[pl.BlockSpec((tm, d), row), pl.BlockSpec((tm, d), row)],
        out_shape=[jax.ShapeDtypeStruct((m, d), F32), jax.ShapeDtypeStruct((m, d), BF16)],
        compiler_params=_params(("parallel",)),
        name="out_proj",
    )(oh, oa, w_rows, h, gain)


def _mlp_kernel(u_ref, w1_ref, w2_ref, h_ref, o_ref):
    f = pl.program_id(1)

    def step(base_ref):
        a = jnp.maximum(jnp.dot(u_ref[...], w1_ref[...], preferred_element_type=F32), 0.0)
        o_ref[...] = base_ref[...] + jnp.dot((a * a).astype(BF16), w2_ref[...],
                                             preferred_element_type=F32)

    @pl.when(f == 0)
    def _():
        step(h_ref)

    @pl.when(f > 0)
    def _():
        step(o_ref)


MLP_TF = 512


def _mlp(u, w1_rows, w2_rows, layer, h, tm=1024, tf=MLP_TF):
    m, d = h.shape
    nf = w1_rows.shape[1] // tf
    return pl.pallas_call(
        _mlp_kernel,
        grid=(m // tm, nf),
        in_specs=[pl.BlockSpec((tm, d), lambda i, f: (i, 0)),
                  pl.BlockSpec((d, tf), lambda i, f: (layer, f)),
                  pl.BlockSpec((tf, d), lambda i, f: (layer * nf + f, 0)),
                  pl.BlockSpec((tm, d), lambda i, f: (i, 0))],
        out_specs=pl.BlockSpec((tm, d), lambda i, f: (i, 0)),
        out_shape=jax.ShapeDtypeStruct((m, d), F32),
        compiler_params=_params(("parallel", "arbitrary")),
        name="mlp",
    )(u, w1_rows, w2_rows, h)


def _ple_kernel(last, h_ref, p_ref, wg_ref, wp_ref, gp_ref, gn_ref, *out_refs):
    h = h_ref[...]
    u = _rms(h, gp_ref[...]).astype(BF16)
    zg = jnp.dot(u, wg_ref[...], preferred_element_type=F32)
    gate = 1.0 / (1.0 + jnp.exp(-zg))
    pe = jnp.dot(p_ref[...].astype(BF16), wp_ref[...], preferred_element_type=F32)
    h = h + pe * gate
    nxt = _rms(h, gn_ref[...])
    if last:
        out_refs[0][...] = nxt
    else:
        out_refs[0][...] = h
        out_refs[1][...] = nxt.astype(out_refs[1].dtype)


def _ple(h, p, layer, wg_rows, wp_rows, g_ple, g_next, last, tm=512):
    m, d = h.shape
    pd = p.shape[1]
    row = lambda i: (i, 0)
    fixed = lambda i: (0, 0)
    of_layer = lambda i: (layer, 0)
    p_row = lambda i: (layer * (m // tm) + i, 0)
    if last:
        out_specs = [pl.BlockSpec((tm, d), row)]
        out_shape = [jax.ShapeDtypeStruct((m, d), F32)]
    else:
        out_specs = [pl.BlockSpec((tm, d), row), pl.BlockSpec((tm, d), row)]
        out_shape = [jax.ShapeDtypeStruct((m, d), F32), jax.ShapeDtypeStruct((m, d), BF16)]
    return pl.pallas_call(
        functools.partial(_ple_kernel, last),
        grid=(m // tm,),
        in_specs=[pl.BlockSpec((tm, d), row), pl.BlockSpec((tm, pd), p_row),
                  pl.BlockSpec((d, d), of_layer), pl.BlockSpec((pd, d), of_layer),
                  pl.BlockSpec((1, d), fixed), pl.BlockSpec((1, d), fixed)],
        out_specs=out_specs,
        out_shape=out_shape,
        compiler_params=_params(("parallel",)),
        name="ple",
    )(h, p, wg_rows, wp_rows, g_ple, g_next)


def kernel(x, p, positions, norm1, w_in, lb_param, hgrn_norm, attn_norm, w_out, norm2, w1, w2,
           ple_norm, w_pg, w_pp, final_norm):
    batch, seq, d = x.shape
    depth, _, in_cols = w_in.shape
    m = batch * seq
    assert seq % ATTN_TILE == 0 and d == 2 * GROUP_WIDTH

    w_in_r = w_in.reshape(depth * d, in_cols)
    w1_r, w2_r = _rows_bf16(w1), _rows_bf16(w2)
    w_out_r, w_pg_r, w_pp_r = _rows_bf16(w_out), _rows_bf16(w_pg), _rows_bf16(w_pp)

    cos, sin = _rope_tables(positions)
    h = x.reshape(m, d)
    p_rows = p.reshape(depth * m, p.shape[-1])
    u = _norm(h, norm1[0].reshape(1, d), BF16)
    out = None
    for i in range(depth):
        last = i == depth - 1
        z = _in_proj(u, w_in_r, i, cos, sin)
        o_hgrn = _hgrn(z, lb_param, hgrn_norm[i].reshape(1, HEAD_DIM), i, batch, seq)
        o_attn = _attn(z, attn_norm[i].reshape(1, HEAD_DIM), batch, seq)
        h, u2 = _out_proj(o_hgrn, o_attn, w_out_r, i, h, norm2[i].reshape(1, d))
        h = _mlp(u2, w1_r, w2_r, i, h)
        g_next = final_norm if last else norm1[i + 1]
        res = _ple(h, p_rows, i, w_pg_r, w_pp_r, ple_norm[i].reshape(1, d),
                   g_next.reshape(1, d), last)
        if last:
            out = res[0]
        else:
            h, u = res
    return out.reshape(batch, seq, d)
```

```python
import functools

import jax
import jax.numpy as jnp
from jax import lax
from jax.experimental import pallas as pl
from jax.experimental.pallas import tpu as pltpu

F32 = jnp.float32
BF16 = jnp.bfloat16

HEAD_DIM = 128
N_HEADS = 8
GROUP_WIDTH = N_HEADS * HEAD_DIM
ROT_DIM = HEAD_DIM // 4
ROPE_THETA = 500000.0
NORM_EPS = 1e-6
QUERY_SCALE = HEAD_DIM ** -0.5 * 1.4426950408889634
HGRN_CHUNK = 64
HGRN_GROUP = 16
HGRN_SUB = 16
HGRN_MAX_HALF_DECAY = 60.0
ATTN_BLOCK = 128
ATTN_TILE = 2048
ATTN_GROUP = 8
DILATIONS = (1, 4, 16)
VMEM_LIMIT = 56 * 1024 * 1024

NT_DIMS = (((1,), (1,)), ((), ()))
TN_DIMS = (((0,), (0,)), ((), ()))


def _params(semantics):
    return pltpu.CompilerParams(dimension_semantics=semantics, vmem_limit_bytes=VMEM_LIMIT)


def _rms(x, gain):
    ms = jnp.mean(x * x, axis=-1, keepdims=True)
    return x * lax.rsqrt(ms + NORM_EPS) * gain


def _silu(x):
    return x / (1.0 + jnp.exp(-x))


def _norm_kernel(x_ref, g_ref, o_ref):
    o_ref[...] = _rms(x_ref[...], g_ref[...]).astype(o_ref.dtype)


def _norm(x, gain, out_dtype, tm=512):
    m, d = x.shape
    return pl.pallas_call(
        _norm_kernel,
        grid=(m // tm,),
        in_specs=[pl.BlockSpec((tm, d), lambda i: (i, 0)),
                  pl.BlockSpec((1, d), lambda i: (0, 0))],
        out_specs=pl.BlockSpec((tm, d), lambda i: (i, 0)),
        out_shape=jax.ShapeDtypeStruct((m, d), out_dtype),
        compiler_params=_params(("parallel",)),
        name="rmsnorm",
    )(x, gain)


def _rope_kernel(pos_ref, invf_ref, cos_ref, sin_ref):
    ang = pos_ref[...] * invf_ref[...]
    lane = lax.broadcasted_iota(jnp.int32, ang.shape, 1)
    s = jnp.sin(ang)
    cos_ref[...] = jnp.cos(ang)
    sin_ref[...] = jnp.where(lane < HEAD_DIM // 2, -s, s)


def _rope_tables(positions, tr=1024):
    n = positions.size
    pos = positions.astype(F32).reshape(n, 1)
    half = ROT_DIM // 2
    inv = 1.0 / (ROPE_THETA ** (jnp.arange(0, ROT_DIM, 2, dtype=F32) / ROT_DIM))
    gap = jnp.zeros((HEAD_DIM // 2 - half,), F32)
    invf = jnp.concatenate([inv, gap, inv, gap]).reshape(1, HEAD_DIM)
    return pl.pallas_call(
        _rope_kernel,
        grid=(n // tr,),
        in_specs=[pl.BlockSpec((tr, 1), lambda i: (i, 0)),
                  pl.BlockSpec((1, HEAD_DIM), lambda i: (0, 0))],
        out_specs=[pl.BlockSpec((tr, HEAD_DIM), lambda i: (i, 0))] * 2,
        out_shape=[jax.ShapeDtypeStruct((n, HEAD_DIM), F32)] * 2,
        compiler_params=_params(("parallel",)),
        name="rope_tables",
    )(pos, invf)


def _rows_bf16(w):
    return w.astype(BF16).reshape(-1, w.shape[-1])


def _pair_rotary_lanes(w):
    half = ROT_DIM // 2
    lane = lax.broadcasted_iota(jnp.int32, w.shape, 1)
    up = pltpu.roll(w, HEAD_DIM - half, axis=1)
    down = pltpu.roll(w, HEAD_DIM // 2 - half, axis=1)
    moved = jnp.where(lane < HEAD_DIM // 2, up, down)
    keep = jnp.logical_or(lane < half, lane >= HEAD_DIM // 2 + half)
    return jnp.where(keep, w, moved)


def _rotary(x, cos, sin_signed):
    return x * cos + pltpu.roll(x, HEAD_DIM // 2, axis=1) * sin_signed


def _in_proj_kernel(q_tile, k_tile, x_ref, w_ref, cos_ref, sin_ref, o_ref, wb_ref):
    j = pl.program_id(0)
    first_row_tile = pl.program_id(1) == 0
    rotated = jnp.logical_or(j == q_tile, j == k_tile)
    plain = jnp.logical_not(rotated)
    heads = [slice(h * HEAD_DIM, (h + 1) * HEAD_DIM) for h in range(o_ref.shape[1] // HEAD_DIM)]

    @pl.when(jnp.logical_and(first_row_tile, plain))
    def _():
        wb_ref[...] = w_ref[...].astype(BF16)

    @pl.when(jnp.logical_and(first_row_tile, rotated))
    def _():
        for cols in heads:
            wb_ref[:, cols] = _pair_rotary_lanes(w_ref[:, cols]).astype(BF16)

    @pl.when(plain)
    def _():
        o_ref[...] = jnp.dot(x_ref[...], wb_ref[...], preferred_element_type=F32)

    @pl.when(rotated)
    def _():
        z = jnp.dot(x_ref[...], wb_ref[...], preferred_element_type=F32)
        scale = jnp.where(j == q_tile, QUERY_SCALE, 1.0)
        cos = cos_ref[...] * scale
        sin = sin_ref[...] * scale
        for cols in heads:
            o_ref[:, cols] = _rotary(z[:, cols], cos, sin)


def _in_proj(u, w_rows, layer, cos, sin, tm=1024, tn=GROUP_WIDTH):
    m, k = u.shape
    n = w_rows.shape[1]
    tab = pl.BlockSpec((tm, HEAD_DIM), lambda j, i: (i, 0))
    return pl.pallas_call(
        functools.partial(_in_proj_kernel, 4, 5),
        grid=(n // tn, m // tm),
        in_specs=[pl.BlockSpec((tm, k), lambda j, i: (i, 0)),
                  pl.BlockSpec((k, tn), lambda j, i: (layer, j)), tab, tab],
        out_specs=pl.BlockSpec((tm, tn), lambda j, i: (i, j)),
        out_shape=jax.ShapeDtypeStruct((m, n), F32),
        scratch_shapes=[pltpu.VMEM((k, tn), BF16)],
        compiler_params=_params(("parallel", "arbitrary")),
        name="in_proj",
    )(u, w_rows, cos, sin)


def _hgrn_kernel(layer, zq_ref, zf_ref, zi_ref, zg_ref, lbp_ref, gn_ref, o_ref,
                 st_ref, q_s, k_s, b_s):
    C, SB = HGRN_CHUNK, HGRN_SUB
    tb = zq_ref.shape[0]
    n_chunks = tb // C
    mid = C // 2 - 1

    @pl.when(pl.program_id(2) == 0)
    def _():
        st_ref[...] = jnp.zeros_like(st_ref)

    lbp = lbp_ref[...]
    e = jnp.exp(lbp - jnp.max(lbp, axis=0, keepdims=True))
    sm = e / jnp.sum(e, axis=0, keepdims=True)
    lb = jnp.zeros((1, HEAD_DIM), F32)
    for j in range(1, layer + 1):
        lb = lb + sm[j:j + 1, :]
    lb_pos = lb > 0.0
    log1m_lb = jnp.log1p(-lb)
    one_m_lb = 1.0 - lb
    gn = gn_ref[...]

    row = lax.broadcasted_iota(jnp.int32, (C, C), 0)
    col = lax.broadcasted_iota(jnp.int32, (C, C), 1)
    causal = row >= col
    tril = jnp.where(causal, 1.0, 0.0).astype(BF16)

    def log_decay(rows):
        zf = zf_ref[rows, :]
        en = jnp.exp(-jnp.abs(zf))
        one_p = 1.0 + en
        rcp = 1.0 / one_p
        small = en * rcp
        nonneg = zf >= 0.0
        k_s[rows, :] = one_m_lb * jnp.where(nonneg, small, rcp)
        log_sig = jnp.minimum(zf, 0.0) - jnp.log(one_p)
        if layer == 0:
            return log_sig
        f = lb + one_m_lb * jnp.where(nonneg, rcp, small)
        return jnp.where(lb_pos, jnp.log(f), log1m_lb + log_sig)

    worst = jnp.zeros((1, HEAD_DIM), F32)
    for g0 in range(0, n_chunks, HGRN_GROUP):
        chunks = [pl.ds(c * C, C) for c in range(g0, min(g0 + HGRN_GROUP, n_chunks))]
        lfs = [log_decay(rows) for rows in chunks]
        his = [lf.astype(BF16) for lf in lfs]
        los = [(lf - hi.astype(F32)).astype(BF16) for lf, hi in zip(lfs, his)]
        bbs = [jnp.dot(tril, jnp.concatenate([hi, lo], axis=1), preferred_element_type=F32)
               for hi, lo in zip(his, los)]
        for rows, bb in zip(chunks, bbs):
            b = bb[:, :HEAD_DIM] + bb[:, HEAD_DIM:]
            b_s[rows, :] = b
            worst = jnp.maximum(worst, jnp.maximum(-b[mid:mid + 1, :],
                                                   b[mid:mid + 1, :] - b[C - 1:C, :]))
        for rows in chunks:
            q_s[rows, :] = _silu(zq_ref[rows, :])
    factorisable = jnp.max(worst) <= HGRN_MAX_HALF_DECAY

    def finish(o, rows):
        o = _rms(o, gn) * _silu(zg_ref[rows, :])
        o_ref[rows, :] = o.astype(o_ref.dtype)

    def state_step(st, vb, kk, b):
        b_end = b[C - 1:C, :]
        kd = (kk * jnp.exp(b_end - b)).astype(BF16)
        return st * jnp.exp(b_end) + lax.dot_general(vb, kd, TN_DIMS, preferred_element_type=F32)

    @pl.when(factorisable)
    def _():
        st = st_ref[...]
        for g0 in range(0, n_chunks, HGRN_GROUP):
            chunks = [pl.ds(c * C, C) for c in range(g0, min(g0 + HGRN_GROUP, n_chunks))]
            bs = [b_s[rows, :] for rows in chunks]
            rs = [b[mid:mid + 1, :] for b in bs]
            q_mids = [q_s[rows, :] * jnp.exp(b - r) for rows, b, r in zip(chunks, bs, rs)]
            k_mids = [k_s[rows, :] * jnp.exp(r - b) for rows, b, r in zip(chunks, bs, rs)]
            vbs = [zi_ref[rows, :].astype(BF16) for rows in chunks]
            scores = [lax.dot_general(qm.astype(BF16), km.astype(BF16), NT_DIMS,
                                      preferred_element_type=F32)
                      for qm, km in zip(q_mids, k_mids)]
            kds = [(km * jnp.exp(b[C - 1:C, :] - r)).astype(BF16)
                   for km, b, r in zip(k_mids, bs, rs)]
            upds = [lax.dot_general(vb, kd, TN_DIMS, preferred_element_type=F32)
                    for vb, kd in zip(vbs, kds)]
            qes = [(qm * jnp.exp(r)).astype(BF16) for qm, r in zip(q_mids, rs)]
            intras = [jnp.dot(jnp.where(causal, a, 0.0).astype(BF16), vb,
                              preferred_element_type=F32) for a, vb in zip(scores, vbs)]
            outs = []
            for qe, b, upd, intra in zip(qes, bs, upds, intras):
                outs.append(intra + lax.dot_general(qe, st.astype(BF16), NT_DIMS,
                                                    preferred_element_type=F32))
                st = st * jnp.exp(b[C - 1:C, :]) + upd
            for o, rows in zip(outs, chunks):
                finish(o, rows)
        st_ref[...] = st

    @pl.when(jnp.logical_not(factorisable))
    def _():
        ones = jnp.ones((HEAD_DIM, HEAD_DIM), BF16)
        sub_t = lax.broadcasted_iota(jnp.int32, (SB, HEAD_DIM), 0)

        def chunk(c, carry):
            r0 = pl.multiple_of(c * C, C)
            rows = pl.ds(r0, C)
            q, kk, b = q_s[rows, :], k_s[rows, :], b_s[rows, :]
            v = zi_ref[rows, :]
            vb = v.astype(BF16)
            st = st_ref[...]
            o_inter = lax.dot_general((q * jnp.exp(b)).astype(BF16), st.astype(BF16), NT_DIMS,
                                      preferred_element_type=F32)
            outs = []
            for i in range(C // SB):
                lo_r = i * SB
                q_i = q[lo_r:lo_r + SB, :]
                b_i = b[lo_r:lo_r + SB, :]
                o_i = o_inter[lo_r:lo_r + SB, :]
                if i > 0:
                    bref = b[lo_r - 1:lo_r, :]
                    qs = (q_i * jnp.exp(b_i - bref)).astype(BF16)
                    ks = (kk[:lo_r, :] * jnp.exp(bref - b[:lo_r, :])).astype(BF16)
                    a = lax.dot_general(qs, ks, NT_DIMS, preferred_element_type=F32)
                    o_i = o_i + jnp.dot(a.astype(BF16), vb[:lo_r, :], preferred_element_type=F32)
                slabs = []
                for s in range(SB):
                    b_row = b_s[pl.ds(r0 + lo_r + s, 1), :]
                    k_row = k_s[pl.ds(r0 + lo_r + s, 1), :]
                    w = q_i * k_row * jnp.exp(jnp.minimum(b_i - b_row, 0.0))
                    slabs.append(jnp.where(sub_t >= s, w, 0.0).astype(BF16))
                red = jnp.dot(jnp.concatenate(slabs, axis=0), ones, preferred_element_type=F32)
                for s in range(SB):
                    o_i = o_i + red[s * SB:(s + 1) * SB, :] * v[lo_r + s:lo_r + s + 1, :]
                outs.append(o_i)
            st_ref[...] = state_step(st, vb, kk, b)
            finish(jnp.concatenate(outs, axis=0), rows)
            return carry

        lax.fori_loop(0, n_chunks, chunk, 0)


def _hgrn(z, lb_param, g_norm, layer, batch, seq, tb=4096):
    nt = seq // tb
    nh = N_HEADS

    def col(off):
        return pl.BlockSpec((tb, HEAD_DIM), lambda b, h, t, off=off: (b * nt + t, off + h))

    return pl.pallas_call(
        functools.partial(_hgrn_kernel, layer),
        grid=(batch, nh, nt),
        in_specs=[col(0), col(nh), col(2 * nh), col(3 * nh),
                  pl.BlockSpec((lb_param.shape[0], HEAD_DIM), lambda b, h, t: (0, h)),
                  pl.BlockSpec((1, HEAD_DIM), lambda b, h, t: (0, 0))],
        out_specs=pl.BlockSpec((tb, HEAD_DIM), lambda b, h, t: (b * nt + t, h)),
        out_shape=jax.ShapeDtypeStruct((batch * seq, GROUP_WIDTH), BF16),
        scratch_shapes=[pltpu.VMEM((HEAD_DIM, HEAD_DIM), F32),
                        pltpu.VMEM((tb, HEAD_DIM), F32),
                        pltpu.VMEM((tb, HEAD_DIM), F32),
                        pltpu.VMEM((tb, HEAD_DIM), F32)],
        compiler_params=_params(("parallel", "parallel", "arbitrary")),
        name="hgrn2",
    )(z, z, z, z, lb_param, g_norm)


def _attn_kernel(q_ref, k_ref, kp_ref, v_ref, vp_ref, gn_ref, o_ref,
                 o_mid, c_mid, o_low, c_low, bias_buf, k_cache, v_cache):
    TQ, Q, G = ATTN_TILE, ATTN_BLOCK, ATTN_GROUP
    t = pl.program_id(2)
    o_bufs, c_bufs = (o_mid, o_low), (c_mid, c_low)

    @pl.when(t == 0)
    def _():
        k_cache[...] = jnp.zeros_like(k_cache)
        v_cache[...] = jnp.zeros_like(v_cache)

    qi = lax.broadcasted_iota(jnp.int32, (Q, 2 * Q), 0)
    kj = lax.broadcasted_iota(jnp.int32, (Q, 2 * Q), 1)
    dist = kj - qi
    band = jnp.where((dist >= 0) & (dist <= Q), 0.0, -jnp.inf)
    first_key = jnp.where(t > 0, 0, Q)
    bias_buf[0] = band
    bias_buf[1] = jnp.where(kj >= first_key, band, -jnp.inf)
    ones_v = jnp.ones((2 * Q, HEAD_DIM), BF16)
    gn = gn_ref[...]
    def run_branch(d, wider, narrower):
        nb_per_res = TQ // (Q * d)

        def contiguous(start):
            return pl.ds(start if isinstance(start, int) else pl.multiple_of(start, Q), Q)

        def rows_of(start):
            return pl.ds(start, Q, stride=d) if d > 1 else contiguous(start)

        def group(res, nb0, at_start):
            ids = []
            for j in range(G):
                if nb_per_res >= G:
                    ids.append((res, nb0 + j, at_start and j == 0))
                else:
                    ids.append((res + j // nb_per_res, j % nb_per_res, j % nb_per_res == 0))
            blocks = [(res_b + d * Q * nb_b, first) for res_b, nb_b, first in ids]

            def with_prev(cur_ref, prev_ref, cache, r0, first):
                cur = cur_ref[rows_of(r0), :].astype(BF16)
                if nb_per_res == 1:
                    prev = cache[r0]
                    fresh.append((cache, r0, cur))
                elif first:
                    prev = prev_ref[rows_of(r0 + TQ - d * Q), :].astype(BF16)
                else:
                    prev = cur_ref[rows_of(r0 - d * Q), :].astype(BF16)
                return jnp.concatenate([prev, cur], axis=0)

            fresh = []
            qs = [q_ref[rows_of(r0), :].astype(BF16) for r0, _ in blocks]
            ks = [with_prev(k_ref, kp_ref, k_cache, r0, first) for r0, first in blocks]
            ss = [lax.dot_general(q, k, NT_DIMS, preferred_element_type=F32)
                  + bias_buf[1 if first else 0] for q, k, (_, first) in zip(qs, ks, blocks)]
            ms = [jnp.max(s, axis=-1, keepdims=True) for s in ss]
            ps = [jnp.exp2(s - m).astype(BF16) for s, m in zip(ss, ms)]
            vs = [with_prev(v_ref, vp_ref, v_cache, r0, first) for r0, first in blocks]
            accs = [jnp.dot(p, jnp.concatenate([v, ones_v], axis=1), preferred_element_type=F32)
                    for p, v in zip(ps, vs)]
            for cache, r0, cur in fresh:
                cache[r0] = cur
            ls = [acc[:, HEAD_DIM:] for acc in accs]
            nums = [acc[:, :HEAD_DIM] for acc in accs]
            if wider is None:
                os_ = [num * (1.0 / l) for num, l in zip(nums, ls)]
                lses = [m + jnp.log2(l) for m, l in zip(ms, ls)]
            else:
                o_in, c_in = wider
                spans = [(res_b, contiguous(Q * nb_b)) for res_b, nb_b, _ in ids]
                c_ws = [c_in[res_b, rows, :] for res_b, rows in spans]
                deltas = [m - c_w for m, c_w in zip(ms, c_ws)]
                smalls = [jnp.exp2(-jnp.abs(delta)) for delta in deltas]
                w_own = [jnp.where(delta >= 0.0, 1.0, e) for delta, e in zip(deltas, smalls)]
                w_in = [jnp.where(delta >= 0.0, e, 1.0) for delta, e in zip(deltas, smalls)]
                dens = [wo * l + wi for wo, l, wi in zip(w_own, ls, w_in)]
                os_ = [(wo * num + wi * o_in[res_b, rows, :]) * (1.0 / den)
                       for wo, num, wi, (res_b, rows), den in zip(w_own, nums, w_in, spans, dens)]
                if narrower is not None:
                    lses = [jnp.maximum(m, c_w) + jnp.log2(den)
                            for m, c_w, den in zip(ms, c_ws, dens)]
            if narrower is not None:
                d_next, o_out, c_out = narrower
                ratio = d // d_next
                for (res_b, nb_b, _), o, lse in zip(ids, os_, lses):
                    rows = pl.ds(res_b // d_next + ratio * Q * nb_b, Q, stride=ratio)
                    o_out[res_b % d_next, rows, :] = o
                    c_out[res_b % d_next, rows, :] = lse
            else:
                for (r0, _), o in zip(blocks, os_):
                    o_ref[rows_of(r0), :] = _rms(o, gn).astype(o_ref.dtype)

        def loop(lo, hi, body):
            def step(it, carry):
                body(it)
                return carry
            lax.fori_loop(lo, hi, step, 0)

        if nb_per_res > G:
            for res in range(d):
                group(res, 0, True)
                loop(1, nb_per_res // G, lambda it: group(res, it * G, False))
        elif nb_per_res == G:
            loop(0, d, lambda res: group(res, 0, True))
        else:
            per_group = G // nb_per_res
            loop(0, d // per_group, lambda it: group(it * per_group, 0, True))

    order = sorted(DILATIONS, reverse=True)
    for k, d in enumerate(order):
        wider = (o_bufs[k - 1], c_bufs[k - 1]) if k > 0 else None
        narrower = (order[k + 1], o_bufs[k], c_bufs[k]) if k + 1 < len(order) else None
        run_branch(d, wider, narrower)


def _attn(z, g_norm, batch, seq):
    TQ = ATTN_TILE
    nt = seq // TQ
    nh = N_HEADS
    _, d_mid, d_low = sorted(DILATIONS, reverse=True)

    def col(off, back=0):
        return pl.BlockSpec((TQ, HEAD_DIM),
                            lambda b, h, t: (b * nt + jnp.maximum(t - back, 0), off + h))

    return pl.pallas_call(
        _attn_kernel,
        grid=(batch, nh, nt),
        in_specs=[col(4 * nh), col(5 * nh), col(5 * nh, 1), col(6 * nh), col(6 * nh, 1),
                  pl.BlockSpec((1, HEAD_DIM), lambda b, h, t: (0, 0))],
        out_specs=pl.BlockSpec((TQ, HEAD_DIM), lambda b, h, t: (b * nt + t, h)),
        out_shape=jax.ShapeDtypeStruct((batch * seq, GROUP_WIDTH), BF16),
        scratch_shapes=[pltpu.VMEM((d_mid, TQ // d_mid, HEAD_DIM), F32),
                        pltpu.VMEM((d_mid, TQ // d_mid, HEAD_DIM), F32),
                        pltpu.VMEM((d_low, TQ // d_low, HEAD_DIM), F32),
                        pltpu.VMEM((d_low, TQ // d_low, HEAD_DIM), F32),
                        pltpu.VMEM((2, ATTN_BLOCK, 2 * ATTN_BLOCK), F32),
                        pltpu.VMEM((max(DILATIONS), ATTN_BLOCK, HEAD_DIM), BF16),
                        pltpu.VMEM((max(DILATIONS), ATTN_BLOCK, HEAD_DIM), BF16)],
        compiler_params=_params(("parallel", "parallel", "arbitrary")),
        name="dilated_attn",
    )(z, z, z, z, z, g_norm)


def _out_proj_kernel(oh_ref, oa_ref, w_ref, h_ref, g_ref, h_out_ref, u_out_ref):
    half = oh_ref.shape[1]
    acc = jnp.dot(oh_ref[...], w_ref[pl.ds(0, half), :], preferred_element_type=F32)
    acc = acc + jnp.dot(oa_ref[...], w_ref[pl.ds(half, half), :], preferred_element_type=F32)
    h = h_ref[...] + acc
    h_out_ref[...] = h
    u_out_ref[...] = _rms(h, g_ref[...]).astype(u_out_ref.dtype)


def _out_proj(oh, oa, w_rows, layer, h, gain, tm=512):
    m, d = h.shape
    half = oh.shape[1]
    row = lambda i: (i, 0)
    fixed = lambda i: (0, 0)
    return pl.pallas_call(
        _out_proj_kernel,
        grid=(m // tm,),
        in_specs=[pl.BlockSpec((tm, half), row), pl.BlockSpec((tm, half), row),
                  pl.BlockSpec((2 * half, d), lambda i: (layer, 0)), pl.BlockSpec((tm, d), row),
                  pl.BlockSpec((1, d), fixed)],
        out_specs=[pl.BlockSpec((tm, d), row), pl.BlockSpec((tm, d), row)],
        out_shape=[jax.ShapeDtypeStruct((m, d), F32), jax.ShapeDtypeStruct((m, d), BF16)],
        compiler_params=_params(("parallel",)),
        name="out_proj",
    )(oh, oa, w_rows, h, gain)


def _mlp_kernel(u_ref, w1_ref, w2_ref, h_ref, o_ref):
    f = pl.program_id(1)

    def step(base_ref):
        a = jnp.maximum(jnp.dot(u_ref[...], w1_ref[...], preferred_element_type=F32), 0.0)
        o_ref[...] = base_ref[...] + jnp.dot((a * a).astype(BF16), w2_ref[...],
                                             preferred_element_type=F32)

    @pl.when(f == 0)
    def _():
        step(h_ref)

    @pl.when(f > 0)
    def _():
        step(o_ref)


MLP_TF = 512


def _mlp(u, w1_rows, w2_rows, layer, h, tm=1024, tf=MLP_TF):
    m, d = h.shape
    nf = w1_rows.shape[1] // tf
    return pl.pallas_call(
        _mlp_kernel,
        grid=(m // tm, nf),
        in_specs=[pl.BlockSpec((tm, d), lambda i, f: (i, 0)),
                  pl.BlockSpec((d, tf), lambda i, f: (layer, f)),
                  pl.BlockSpec((tf, d), lambda i, f: (layer * nf + f, 0)),
                  pl.BlockSpec((tm, d), lambda i, f: (i, 0))],
        out_specs=pl.BlockSpec((tm, d), lambda i, f: (i, 0)),
        out_shape=jax.ShapeDtypeStruct((m, d), F32),
        compiler_params=_params(("parallel", "arbitrary")),
        name="mlp",
    )(u, w1_rows, w2_rows, h)


def _ple_kernel(last, h_ref, p_ref, wg_ref, wp_ref, gp_ref, gn_ref, *out_refs):
    h = h_ref[...]
    u = _rms(h, gp_ref[...]).astype(BF16)
    zg = jnp.dot(u, wg_ref[...], preferred_element_type=F32)
    gate = 1.0 / (1.0 + jnp.exp(-zg))
    pe = jnp.dot(p_ref[...].astype(BF16), wp_ref[...], preferred_element_type=F32)
    h = h + pe * gate
    nxt = _rms(h, gn_ref[...])
    if last:
        out_refs[0][...] = nxt
    else:
        out_refs[0][...] = h
        out_refs[1][...] = nxt.astype(out_refs[1].dtype)


def _ple(h, p, layer, wg_rows, wp_rows, g_ple, g_next, last, tm=512):
    m, d = h.shape
    pd = p.shape[1]
    row = lambda i: (i, 0)
    fixed = lambda i: (0, 0)
    of_layer = lambda i: (layer, 0)
    p_row = lambda i: (layer * (m // tm) + i, 0)
    if last:
        out_specs = [pl.BlockSpec((tm, d), row)]
        out_shape = [jax.ShapeDtypeStruct((m, d), F32)]
    else:
        out_specs = [pl.BlockSpec((tm, d), row), pl.BlockSpec((tm, d), row)]
        out_shape = [jax.ShapeDtypeStruct((m, d), F32), jax.ShapeDtypeStruct((m, d), BF16)]
    return pl.pallas_call(
        functools.partial(_ple_kernel, last),
        grid=(m // tm,),
        in_specs=[pl.BlockSpec((tm, d), row), pl.BlockSpec((tm, pd), p_row),
                  pl.BlockSpec((d, d), of_layer), pl.BlockSpec((pd, d), of_layer),
                  pl.BlockSpec((1, d), fixed), pl.BlockSpec((1, d), fixed)],
        out_specs=out_specs,
        out_shape=out_shape,
        compiler_params=_params(("parallel",)),
        name="ple",
    )(h, p, wg_rows, wp_rows, g_ple, g_next)


def kernel(x, p, positions, norm1, w_in, lb_param, hgrn_norm, attn_norm, w_out, norm2, w1, w2,
           ple_norm, w_pg, w_pp, final_norm):
    batch, seq, d = x.shape
    depth, _, in_cols = w_in.shape
    m = batch * seq
    assert seq % ATTN_TILE == 0 and d == 2 * GROUP_WIDTH

    w_in_r = w_in.reshape(depth * d, in_cols)
    w1_r, w2_r = _rows_bf16(w1), _rows_bf16(w2)
    w_out_r, w_pg_r, w_pp_r = _rows_bf16(w_out), _rows_bf16(w_pg), _rows_bf16(w_pp)

    cos, sin = _rope_tables(positions)
    h = x.reshape(m, d)
    p_rows = p.reshape(depth * m, p.shape[-1])
    u = _norm(h, norm1[0].reshape(1, d), BF16)
    out = None
    for i in range(depth):
        last = i == depth - 1
        z = _in_proj(u, w_in_r, i, cos, sin)
        o_hgrn = _hgrn(z, lb_param, hgrn_norm[i].reshape(1, HEAD_DIM), i, batch, seq)
        o_attn = _attn(z, attn_norm[i].reshape(1, HEAD_DIM), batch, seq)
        h, u2 = _out_proj(o_hgrn, o_attn, w_out_r, i, h, norm2[i].reshape(1, d))
        h = _mlp(u2, w1_r, w2_r, i, h)
        g_next = final_norm if last else norm1[i + 1]
        res = _ple(h, p_rows, i, w_pg_r, w_pp_r, ple_norm[i].reshape(1, d),
                   g_next.reshape(1, d), last)
        if last:
            out = res[0]
        else:
            h, u = res
    return out.reshape(batch, seq, d)
```

```python
import functools

import jax
import jax.numpy as jnp
from jax import lax
from jax.experimental import pallas as pl
from jax.experimental.pallas import tpu as pltpu

F32 = jnp.float32
BF16 = jnp.bfloat16

HEAD_DIM = 128
N_HEADS = 8
GROUP_WIDTH = N_HEADS * HEAD_DIM
ROT_DIM = HEAD_DIM // 4
ROPE_THETA = 500000.0
NORM_EPS = 1e-6
QUERY_SCALE = HEAD_DIM ** -0.5 * 1.4426950408889634
HGRN_CHUNK = 64
HGRN_GROUP = 16
HGRN_SUB = 16
HGRN_MAX_HALF_DECAY = 60.0
ATTN_BLOCK = 128
ATTN_TILE = 2048
ATTN_GROUP = 8
DILATIONS = (1, 4, 16)
VMEM_LIMIT = 56 * 1024 * 1024

NT_DIMS = (((1,), (1,)), ((), ()))
TN_DIMS = (((0,), (0,)), ((), ()))


def _params(semantics):
    return pltpu.CompilerParams(dimension_semantics=semantics, vmem_limit_bytes=VMEM_LIMIT)


def _rms(x, gain):
    ms = jnp.mean(x * x, axis=-1, keepdims=True)
    return x * lax.rsqrt(ms + NORM_EPS) * gain


def _silu(x):
    return x / (1.0 + jnp.exp(-x))


def _norm_kernel(x_ref, g_ref, o_ref):
    o_ref[...] = _rms(x_ref[...], g_ref[...]).astype(o_ref.dtype)


def _norm(x, gain, out_dtype, tm=512):
    m, d = x.shape
    return pl.pallas_call(
        _norm_kernel,
        grid=(m // tm,),
        in_specs=[pl.BlockSpec((tm, d), lambda i: (i, 0)),
                  pl.BlockSpec((1, d), lambda i: (0, 0))],
        out_specs=pl.BlockSpec((tm, d), lambda i: (i, 0)),
        out_shape=jax.ShapeDtypeStruct((m, d), out_dtype),
        compiler_params=_params(("parallel",)),
        name="rmsnorm",
    )(x, gain)


def _rope_kernel(pos_ref, invf_ref, cos_ref, sin_ref):
    ang = pos_ref[...] * invf_ref[...]
    lane = lax.broadcasted_iota(jnp.int32, ang.shape, 1)
    s = jnp.sin(ang)
    cos_ref[...] = jnp.cos(ang)
    sin_ref[...] = jnp.where(lane < HEAD_DIM // 2, -s, s)


def _rope_tables(positions, tr=1024):
    n = positions.size
    pos = positions.astype(F32).reshape(n, 1)
    half = ROT_DIM // 2
    inv = 1.0 / (ROPE_THETA ** (jnp.arange(0, ROT_DIM, 2, dtype=F32) / ROT_DIM))
    gap = jnp.zeros((HEAD_DIM // 2 - half,), F32)
    invf = jnp.concatenate([inv, gap, inv, gap]).reshape(1, HEAD_DIM)
    return pl.pallas_call(
        _rope_kernel,
        grid=(n // tr,),
        in_specs=[pl.BlockSpec((tr, 1), lambda i: (i, 0)),
                  pl.BlockSpec((1, HEAD_DIM), lambda i: (0, 0))],
        out_specs=[pl.BlockSpec((tr, HEAD_DIM), lambda i: (i, 0))] * 2,
        out_shape=[jax.ShapeDtypeStruct((n, HEAD_DIM), F32)] * 2,
        compiler_params=_params(("parallel",)),
        name="rope_tables",
    )(pos, invf)


def _rows_bf16(w):
    return w.astype(BF16).reshape(-1, w.shape[-1])


def _pair_rotary_lanes(w):
    half = ROT_DIM // 2
    lane = lax.broadcasted_iota(jnp.int32, w.shape, 1)
    up = pltpu.roll(w, HEAD_DIM - half, axis=1)
    down = pltpu.roll(w, HEAD_DIM // 2 - half, axis=1)
    moved = jnp.where(lane < HEAD_DIM // 2, up, down)
    keep = jnp.logical_or(lane < half, lane >= HEAD_DIM // 2 + half)
    return jnp.where(keep, w, moved)


def _rotary(x, cos, sin_signed):
    return x * cos + pltpu.roll(x, HEAD_DIM // 2, axis=1) * sin_signed


def _in_proj_kernel(q_tile, k_tile, x_ref, w_ref, cos_ref, sin_ref, o_ref, wb_ref):
    j = pl.program_id(0)
    first_row_tile = pl.program_id(1) == 0
    rotated = jnp.logical_or(j == q_tile, j == k_tile)
    plain = jnp.logical_not(rotated)
    heads = [slice(h * HEAD_DIM, (h + 1) * HEAD_DIM) for h in range(o_ref.shape[1] // HEAD_DIM)]

    @pl.when(jnp.logical_and(first_row_tile, plain))
    def _():
        wb_ref[...] = w_ref[...].astype(BF16)

    @pl.when(jnp.logical_and(first_row_tile, rotated))
    def _():
        for cols in heads:
            wb_ref[:, cols] = _pair_rotary_lanes(w_ref[:, cols]).astype(BF16)

    @pl.when(plain)
    def _():
        o_ref[...] = jnp.dot(x_ref[...], wb_ref[...], preferred_element_type=F32)

    @pl.when(rotated)
    def _():
        z = jnp.dot(x_ref[...], wb_ref[...], preferred_element_type=F32)
        scale = jnp.where(j == q_tile, QUERY_SCALE, 1.0)
        cos = cos_ref[...] * scale
        sin = sin_ref[...] * scale
        for cols in heads:
            o_ref[:, cols] = _rotary(z[:, cols], cos, sin)


def _in_proj(u, w_rows, layer, cos, sin, tm=1024, tn=GROUP_WIDTH):
    m, k = u.shape
    n = w_rows.shape[1]
    tab = pl.BlockSpec((tm, HEAD_DIM), lambda j, i: (i, 0))
    return pl.pallas_call(
        functools.partial(_in_proj_kernel, 4, 5),
        grid=(n // tn, m // tm),
        in_specs=[pl.BlockSpec((tm, k), lambda j, i: (i, 0)),
                  pl.BlockSpec((k, tn), lambda j, i: (layer, j)), tab, tab],
        out_specs=pl.BlockSpec((tm, tn), lambda j, i: (i, j)),
        out_shape=jax.ShapeDtypeStruct((m, n), F32),
        scratch_shapes=[pltpu.VMEM((k, tn), BF16)],
        compiler_params=_params(("parallel", "arbitrary")),
        name="in_proj",
    )(u, w_rows, cos, sin)


def _hgrn_kernel(layer, zq_ref, zf_ref, zi_ref, zg_ref, lbp_ref, gn_ref, o_ref,
                 st_ref, q_s, k_s, b_s):
    C, SB = HGRN_CHUNK, HGRN_SUB
    tb = zq_ref.shape[0]
    n_chunks = tb // C
    mid = C // 2 - 1

    @pl.when(pl.program_id(2) == 0)
    def _():
        st_ref[...] = jnp.zeros_like(st_ref)

    lbp = lbp_ref[...]
    e = jnp.exp(lbp - jnp.max(lbp, axis=0, keepdims=True))
    sm = e / jnp.sum(e, axis=0, keepdims=True)
    lb = jnp.zeros((1, HEAD_DIM), F32)
    for j in range(1, layer + 1):
        lb = lb + sm[j:j + 1, :]
    lb_pos = lb > 0.0
    log1m_lb = jnp.log1p(-lb)
    one_m_lb = 1.0 - lb
    gn = gn_ref[...]

    row = lax.broadcasted_iota(jnp.int32, (C, C), 0)
    col = lax.broadcasted_iota(jnp.int32, (C, C), 1)
    causal = row >= col
    tril = jnp.where(causal, 1.0, 0.0).astype(BF16)

    def log_decay(rows):
        zf = zf_ref[rows, :]
        en = jnp.exp(-jnp.abs(zf))
        one_p = 1.0 + en
        rcp = 1.0 / one_p
        small = en * rcp
        nonneg = zf >= 0.0
        k_s[rows, :] = one_m_lb * jnp.where(nonneg, small, rcp)
        log_sig = jnp.minimum(zf, 0.0) - jnp.log(one_p)
        if layer == 0:
            return log_sig
        f = lb + one_m_lb * jnp.where(nonneg, rcp, small)
        return jnp.where(lb_pos, jnp.log(f), log1m_lb + log_sig)

    worst = jnp.zeros((1, HEAD_DIM), F32)
    for g0 in range(0, n_chunks, HGRN_GROUP):
        chunks = [pl.ds(c * C, C) for c in range(g0, min(g0 + HGRN_GROUP, n_chunks))]
        lfs = [log_decay(rows) for rows in chunks]
        his = [lf.astype(BF16) for lf in lfs]
        los = [(lf - hi.astype(F32)).astype(BF16) for lf, hi in zip(lfs, his)]
        bbs = [jnp.dot(tril, jnp.concatenate([hi, lo], axis=1), preferred_element_type=F32)
               for hi, lo in zip(his, los)]
        for rows, bb in zip(chunks, bbs):
            b = bb[:, :HEAD_DIM] + bb[:, HEAD_DIM:]
            b_s[rows, :] = b
            worst = jnp.maximum(worst, jnp.maximum(-b[mid:mid + 1, :],
                                                   b[mid:mid + 1, :] - b[C - 1:C, :]))
        for rows in chunks:
            q_s[rows, :] = _silu(zq_ref[rows, :])
    factorisable = jnp.max(worst) <= HGRN_MAX_HALF_DECAY

    def finish(o, rows):
        o = _rms(o, gn) * _silu(zg_ref[rows, :])
        o_ref[rows, :] = o.astype(o_ref.dtype)

    def state_step(st, vb, kk, b):
        b_end = b[C - 1:C, :]
        kd = (kk * jnp.exp(b_end - b)).astype(BF16)
        return st * jnp.exp(b_end) + lax.dot_general(vb, kd, TN_DIMS, preferred_element_type=F32)

    @pl.when(factorisable)
    def _():
        st = st_ref[...]
        for g0 in range(0, n_chunks, HGRN_GROUP):
            chunks = [pl.ds(c * C, C) for c in range(g0, min(g0 + HGRN_GROUP, n_chunks))]
            bs = [b_s[rows, :] for rows in chunks]
            rs = [b[mid:mid + 1, :] for b in bs]
            q_mids = [q_s[rows, :] * jnp.exp(b - r) for rows, b, r in zip(chunks, bs, rs)]
            k_mids = [k_s[rows, :] * jnp.exp(r - b) for rows, b, r in zip(chunks, bs, rs)]
            vbs = [zi_ref[rows, :].astype(BF16) for rows in chunks]
            scores = [lax.dot_general(qm.astype(BF16), km.astype(BF16), NT_DIMS,
                                      preferred_element_type=F32)
                      for qm, km in zip(q_mids, k_mids)]
            kds = [(km * jnp.exp(b[C - 1:C, :] - r)).astype(BF16)
                   for km, b, r in zip(k_mids, bs, rs)]
            upds = [lax.dot_general(vb, kd, TN_DIMS, preferred_element_type=F32)
                    for vb, kd in zip(vbs, kds)]
            qes = [(qm * jnp.exp(r)).astype(BF16) for qm, r in zip(q_mids, rs)]
            intras = [jnp.dot(jnp.where(causal, a, 0.0).astype(BF16), vb,
                              preferred_element_type=F32) for a, vb in zip(scores, vbs)]
            outs = []
            for qe, b, upd, intra in zip(qes, bs, upds, intras):
                outs.append(intra + lax.dot_general(qe, st.astype(BF16), NT_DIMS,
                                                    preferred_element_type=F32))
                st = st * jnp.exp(b[C - 1:C, :]) + upd
            for o, rows in zip(outs, chunks):
                finish(o, rows)
        st_ref[...] = st

    @pl.when(jnp.logical_not(factorisable))
    def _():
        ones = jnp.ones((HEAD_DIM, HEAD_DIM), BF16)
        sub_t = lax.broadcasted_iota(jnp.int32, (SB, HEAD_DIM), 0)

        def chunk(c, carry):
            r0 = pl.multiple_of(c * C, C)
            rows = pl.ds(r0, C)
            q, kk, b = q_s[rows, :], k_s[rows, :], b_s[rows, :]
            v = zi_ref[rows, :]
            vb = v.astype(BF16)
            st = st_ref[...]
            o_inter = lax.dot_general((q * jnp.exp(b)).astype(BF16), st.astype(BF16), NT_DIMS,
                                      preferred_element_type=F32)
            outs = []
            for i in range(C // SB):
                lo_r = i * SB
                q_i = q[lo_r:lo_r + SB, :]
                b_i = b[lo_r:lo_r + SB, :]
                o_i = o_inter[lo_r:lo_r + SB, :]
                if i > 0:
                    bref = b[lo_r - 1:lo_r, :]
                    qs = (q_i * jnp.exp(b_i - bref)).astype(BF16)
                    ks = (kk[:lo_r, :] * jnp.exp(bref - b[:lo_r, :])).astype(BF16)
                    a = lax.dot_general(qs, ks, NT_DIMS, preferred_element_type=F32)
                    o_i = o_i + jnp.dot(a.astype(BF16), vb[:lo_r, :], preferred_element_type=F32)
                slabs = []
                for s in range(SB):
                    b_row = b_s[pl.ds(r0 + lo_r + s, 1), :]
                    k_row = k_s[pl.ds(r0 + lo_r + s, 1), :]
                    w = q_i * k_row * jnp.exp(jnp.minimum(b_i - b_row, 0.0))
                    slabs.append(jnp.where(sub_t >= s, w, 0.0).astype(BF16))
                red = jnp.dot(jnp.concatenate(slabs, axis=0), ones, preferred_element_type=F32)
                for s in range(SB):
                    o_i = o_i + red[s * SB:(s + 1) * SB, :] * v[lo_r + s:lo_r + s + 1, :]
                outs.append(o_i)
            st_ref[...] = state_step(st, vb, kk, b)
            finish(jnp.concatenate(outs, axis=0), rows)
            return carry

        lax.fori_loop(0, n_chunks, chunk, 0)


def _hgrn(z, lb_param, g_norm, layer, batch, seq, tb=4096):
    nt = seq // tb
    nh = N_HEADS

    def col(off):
        return pl.BlockSpec((tb, HEAD_DIM), lambda b, h, t, off=off: (b * nt + t, off + h))

    return pl.pallas_call(
        functools.partial(_hgrn_kernel, layer),
        grid=(batch, nh, nt),
        in_specs=[col(0), col(nh), col(2 * nh), col(3 * nh),
                  pl.BlockSpec((lb_param.shape[0], HEAD_DIM), lambda b, h, t: (0, h)),
                  pl.BlockSpec((1, HEAD_DIM), lambda b, h, t: (0, 0))],
        out_specs=pl.BlockSpec((tb, HEAD_DIM), lambda b, h, t: (b * nt + t, h)),
        out_shape=jax.ShapeDtypeStruct((batch * seq, GROUP_WIDTH), BF16),
        scratch_shapes=[pltpu.VMEM((HEAD_DIM, HEAD_DIM), F32),
                        pltpu.VMEM((tb, HEAD_DIM), F32),
                        pltpu.VMEM((tb, HEAD_DIM), F32),
                        pltpu.VMEM((tb, HEAD_DIM), F32)],
        compiler_params=_params(("parallel", "parallel", "arbitrary")),
        name="hgrn2",
    )(z, z, z, z, lb_param, g_norm)


def _attn_kernel(q_ref, k_ref, kp_ref, v_ref, vp_ref, gn_ref, o_ref,
                 o_mid, c_mid, o_low, c_low, bias_buf, k_cache, v_cache):
    TQ, Q, G = ATTN_TILE, ATTN_BLOCK, ATTN_GROUP
    t = pl.program_id(2)
    o_bufs, c_bufs = (o_mid, o_low), (c_mid, c_low)

    @pl.when(t == 0)
    def _():
        k_cache[...] = jnp.zeros_like(k_cache)
        v_cache[...] = jnp.zeros_like(v_cache)

    qi = lax.broadcasted_iota(jnp.int32, (Q, 2 * Q), 0)
    kj = lax.broadcasted_iota(jnp.int32, (Q, 2 * Q), 1)
    dist = kj - qi
    band = jnp.where((dist >= 0) & (dist <= Q), 0.0, -jnp.inf)
    first_key = jnp.where(t > 0, 0, Q)
    bias_buf[0] = band
    bias_buf[1] = jnp.where(kj >= first_key, band, -jnp.inf)
    ones_v = jnp.ones((2 * Q, HEAD_DIM), BF16)
    gn = gn_ref[...]
    def run_branch(d, wider, narrower):
        nb_per_res = TQ // (Q * d)

        def contiguous(start):
            return pl.ds(start if isinstance(start, int) else pl.multiple_of(start, Q), Q)

        def rows_of(start):
            return pl.ds(start, Q, stride=d) if d > 1 else contiguous(start)

        def group(res, nb0, at_start):
            ids = []
            for j in range(G):
                if nb_per_res >= G:
                    ids.append((res, nb0 + j, at_start and j == 0))
                else:
                    ids.append((res + j // nb_per_res, j % nb_per_res, j % nb_per_res == 0))
            blocks = [(res_b + d * Q * nb_b, first) for res_b, nb_b, first in ids]

            def with_prev(cur_ref, prev_ref, cache, r0, first):
                cur = cur_ref[rows_of(r0), :].astype(BF16)
                if nb_per_res == 1:
                    prev = cache[r0]
                    fresh.append((cache, r0, cur))
                elif first:
                    prev = prev_ref[rows_of(r0 + TQ - d * Q), :].astype(BF16)
                else:
                    prev = cur_ref[rows_of(r0 - d * Q), :].astype(BF16)
                return jnp.concatenate([prev, cur], axis=0)

            fresh = []
            qs = [q_ref[rows_of(r0), :].astype(BF16) for r0, _ in blocks]
            ks = [with_prev(k_ref, kp_ref, k_cache, r0, first) for r0, first in blocks]
            ss = [lax.dot_general(q, k, NT_DIMS, preferred_element_type=F32)
                  + bias_buf[1 if first else 0] for q, k, (_, first) in zip(qs, ks, blocks)]
            ms = [jnp.max(s, axis=-1, keepdims=True) for s in ss]
            ps = [jnp.exp2(s - m).astype(BF16) for s, m in zip(ss, ms)]
            vs = [with_prev(v_ref, vp_ref, v_cache, r0, first) for r0, first in blocks]
            accs = [jnp.dot(p, jnp.concatenate([v, ones_v], axis=1), preferred_element_type=F32)
                    for p, v in zip(ps, vs)]
            for cache, r0, cur in fresh:
                cache[r0] = cur
            ls = [acc[:, HEAD_DIM:] for acc in accs]
            nums = [acc[:, :HEAD_DIM] for acc in accs]
            if wider is None:
                os_ = [num * (1.0 / l) for num, l in zip(nums, ls)]
                lses = [m + jnp.log2(l) for m, l in zip(ms, ls)]
            else:
                o_in, c_in = wider
                spans = [(res_b, contiguous(Q * nb_b)) for res_b, nb_b, _ in ids]
                c_ws = [c_in[res_b, rows, :] for res_b, rows in spans]
                deltas = [m - c_w for m, c_w in zip(ms, c_ws)]
                smalls = [jnp.exp2(-jnp.abs(delta)) for delta in deltas]
                w_own = [jnp.where(delta >= 0.0, 1.0, e) for delta, e in zip(deltas, smalls)]
                w_in = [jnp.where(delta >= 0.0, e, 1.0) for delta, e in zip(deltas, smalls)]
                dens = [wo * l + wi for wo, l, wi in zip(w_own, ls, w_in)]
                os_ = [(wo * num + wi * o_in[res_b, rows, :]) * (1.0 / den)
                       for wo, num, wi, (res_b, rows), den in zip(w_own, nums, w_in, spans, dens)]
                if narrower is not None:
                    lses = [jnp.maximum(m, c_w) + jnp.log2(den)
                            for m, c_w, den in zip(ms, c_ws, dens)]
            if narrower is not None:
                d_next, o_out, c_out = narrower
                ratio = d // d_next
                for (res_b, nb_b, _), o, lse in zip(ids, os_, lses):
                    rows = pl.ds(res_b // d_next + ratio * Q * nb_b, Q, stride=ratio)
                    o_out[res_b % d_next, rows, :] = o
                    c_out[res_b % d_next, rows, :] = lse
            else:
                for (r0, _), o in zip(blocks, os_):
                    o_ref[rows_of(r0), :] = _rms(o, gn).astype(o_ref.dtype)

        def loop(lo, hi, body):
            def step(it, carry):
                body(it)
                return carry
            lax.fori_loop(lo, hi, step, 0)

        if nb_per_res > G:
            for res in range(d):
                group(res, 0, True)
                loop(1, nb_per_res // G, lambda it: group(res, it * G, False))
        elif nb_per_res == G:
            loop(0, d, lambda res: group(res, 0, True))
        else:
            per_group = G // nb_per_res
            loop(0, d // per_group, lambda it: group(it * per_group, 0, True))

    order = sorted(DILATIONS, reverse=True)
    for k, d in enumerate(order):
        wider = (o_bufs[k - 1], c_bufs[k - 1]) if k > 0 else None
        narrower = (order[k + 1], o_bufs[k], c_bufs[k]) if k + 1 < len(order) else None
        run_branch(d, wider, narrower)


def _attn(z, g_norm, batch, seq):
    TQ = ATTN_TILE
    nt = seq // TQ
    nh = N_HEADS
    _, d_mid, d_low = sorted(DILATIONS, reverse=True)

    def col(off, back=0):
        return pl.BlockSpec((TQ, HEAD_DIM),
                            lambda b, h, t: (b * nt + jnp.maximum(t - back, 0), off + h))

    return pl.pallas_call(
        _attn_kernel,
        grid=(batch, nh, nt),
        in_specs=[col(4 * nh), col(5 * nh), col(5 * nh, 1), col(6 * nh), col(6 * nh, 1),
                  pl.BlockSpec((1, HEAD_DIM), lambda b, h, t: (0, 0))],
        out_specs=pl.BlockSpec((TQ, HEAD_DIM), lambda b, h, t: (b * nt + t, h)),
        out_shape=jax.ShapeDtypeStruct((batch * seq, GROUP_WIDTH), BF16),
        scratch_shapes=[pltpu.VMEM((d_mid, TQ // d_mid, HEAD_DIM), F32),
                        pltpu.VMEM((d_mid, TQ // d_mid, HEAD_DIM), F32),
                        pltpu.VMEM((d_low, TQ // d_low, HEAD_DIM), F32),
                        pltpu.VMEM((d_low, TQ // d_low, HEAD_DIM), F32),
                        pltpu.VMEM((2, ATTN_BLOCK, 2 * ATTN_BLOCK), F32),
                        pltpu.VMEM((max(DILATIONS), ATTN_BLOCK, HEAD_DIM), BF16),
                        pltpu.VMEM((max(DILATIONS), ATTN_BLOCK, HEAD_DIM), BF16)],
        compiler_params=_params(("parallel", "parallel", "arbitrary")),
        name="dilated_attn",
    )(z, z, z, z, z, g_norm)


def _out_proj_kernel(oh_ref, oa_ref, w_ref, h_ref, g_ref, w1_ref, w2_ref,
                     h_out_ref, u_out_ref, w1b_ref, w2b_ref):
    half = oh_ref.shape[1]
    acc = jnp.dot(oh_ref[...], w_ref[pl.ds(0, half), :], preferred_element_type=F32)
    acc = acc + jnp.dot(oa_ref[...], w_ref[pl.ds(half, half), :], preferred_element_type=F32)
    h = h_ref[...] + acc
    h_out_ref[...] = h
    u_out_ref[...] = _rms(h, g_ref[...]).astype(u_out_ref.dtype)
    w1b_ref[...] = w1_ref[...].astype(BF16)
    w2b_ref[...] = w2_ref[...].astype(BF16)


def _out_proj(oh, oa, w_rows, layer, h, gain, w1_rows, w2_rows, tm=512):
    m, d = h.shape
    half = oh.shape[1]
    steps = m // tm
    dff = w1_rows.shape[1]
    row = lambda i: (i, 0)
    fixed = lambda i: (0, 0)
    slab = lambda i: (layer * steps + i, 0)
    return pl.pallas_call(
        _out_proj_kernel,
        grid=(steps,),
        in_specs=[pl.BlockSpec((tm, half), row), pl.BlockSpec((tm, half), row),
                  pl.BlockSpec((2 * half, d), lambda i: (layer, 0)), pl.BlockSpec((tm, d), row),
                  pl.BlockSpec((1, d), fixed),
                  pl.BlockSpec((d // steps, dff), slab), pl.BlockSpec((dff // steps, d), slab)],
        out_specs=[pl.BlockSpec((tm, d), row), pl.BlockSpec((tm, d), row),
                   pl.BlockSpec((d // steps, dff), row), pl.BlockSpec((dff // steps, d), row)],
        out_shape=[jax.ShapeDtypeStruct((m, d), F32), jax.ShapeDtypeStruct((m, d), BF16),
                   jax.ShapeDtypeStruct((d, dff), BF16), jax.ShapeDtypeStruct((dff, d), BF16)],
        compiler_params=_params(("parallel",)),
        name="out_proj",
    )(oh, oa, w_rows, h, gain, w1_rows, w2_rows)


def _mlp_kernel(u_ref, w1_ref, w2_ref, h_ref, o_ref):
    f = pl.program_id(1)

    def step(base_ref):
        a = jnp.maximum(jnp.dot(u_ref[...], w1_ref[...], preferred_element_type=F32), 0.0)
        o_ref[...] = base_ref[...] + jnp.dot((a * a).astype(BF16), w2_ref[...],
                                             preferred_element_type=F32)

    @pl.when(f == 0)
    def _():
        step(h_ref)

    @pl.when(f > 0)
    def _():
        step(o_ref)


MLP_TF = 512


def _mlp(u, w1_rows, w2_rows, layer, h, tm=1024, tf=MLP_TF):
    m, d = h.shape
    nf = w1_rows.shape[1] // tf
    return pl.pallas_call(
        _mlp_kernel,
        grid=(m // tm, nf),
        in_specs=[pl.BlockSpec((tm, d), lambda i, f: (i, 0)),
                  pl.BlockSpec((d, tf), lambda i, f: (layer, f)),
                  pl.BlockSpec((tf, d), lambda i, f: (layer * nf + f, 0)),
                  pl.BlockSpec((tm, d), lambda i, f: (i, 0))],
        out_specs=pl.BlockSpec((tm, d), lambda i, f: (i, 0)),
        out_shape=jax.ShapeDtypeStruct((m, d), F32),
        compiler_params=_params(("parallel", "arbitrary")),
        name="mlp",
    )(u, w1_rows, w2_rows, h)


def _ple_kernel(last, h_ref, p_ref, wg_ref, wp_ref, gp_ref, gn_ref, *out_refs):
    h = h_ref[...]
    u = _rms(h, gp_ref[...]).astype(BF16)
    zg = jnp.dot(u, wg_ref[...], preferred_element_type=F32)
    gate = 1.0 / (1.0 + jnp.exp(-zg))
    pe = jnp.dot(p_ref[...].astype(BF16), wp_ref[...], preferred_element_type=F32)
    h = h + pe * gate
    nxt = _rms(h, gn_ref[...])
    if last:
        out_refs[0][...] = nxt
    else:
        out_refs[0][...] = h
        out_refs[1][...] = nxt.astype(out_refs[1].dtype)


def _ple(h, p, layer, wg_rows, wp_rows, g_ple, g_next, last, tm=512):
    m, d = h.shape
    pd = p.shape[1]
    row = lambda i: (i, 0)
    fixed = lambda i: (0, 0)
    of_layer = lambda i: (layer, 0)
    p_row = lambda i: (layer * (m // tm) + i, 0)
    if last:
        out_specs = [pl.BlockSpec((tm, d), row)]
        out_shape = [jax.ShapeDtypeStruct((m, d), F32)]
    else:
        out_specs = [pl.BlockSpec((tm, d), row), pl.BlockSpec((tm, d), row)]
        out_shape = [jax.ShapeDtypeStruct((m, d), F32), jax.ShapeDtypeStruct((m, d), BF16)]
    return pl.pallas_call(
        functools.partial(_ple_kernel, last),
        grid=(m // tm,),
        in_specs=[pl.BlockSpec((tm, d), row), pl.BlockSpec((tm, pd), p_row),
                  pl.BlockSpec((d, d), of_layer), pl.BlockSpec((pd, d), of_layer),
                  pl.BlockSpec((1, d), fixed), pl.BlockSpec((1, d), fixed)],
        out_specs=out_specs,
        out_shape=out_shape,
        compiler_params=_params(("parallel",)),
        name="ple",
    )(h, p, wg_rows, wp_rows, g_ple, g_next)


def kernel(x, p, positions, norm1, w_in, lb_param, hgrn_norm, attn_norm, w_out, norm2, w1, w2,
           ple_norm, w_pg, w_pp, final_norm):
    batch, seq, d = x.shape
    depth, _, in_cols = w_in.shape
    m = batch * seq
    assert seq % ATTN_TILE == 0 and d == 2 * GROUP_WIDTH

    w_in_r = w_in.reshape(depth * d, in_cols)
    w1_r, w2_r = w1.reshape(depth * d, -1), w2.reshape(-1, d)
    w_out_r, w_pg_r, w_pp_r = _rows_bf16(w_out), _rows_bf16(w_pg), _rows_bf16(w_pp)

    cos, sin = _rope_tables(positions)
    h = x.reshape(m, d)
    p_rows = p.reshape(depth * m, p.shape[-1])
    u = _norm(h, norm1[0].reshape(1, d), BF16)
    out = None
    for i in range(depth):
        last = i == depth - 1
        z = _in_proj(u, w_in_r, i, cos, sin)
        o_hgrn = _hgrn(z, lb_param, hgrn_norm[i].reshape(1, HEAD_DIM), i, batch, seq)
        o_attn = _attn(z, attn_norm[i].reshape(1, HEAD_DIM), batch, seq)
        h, u2, w1_b, w2_b = _out_proj(o_hgrn, o_attn, w_out_r, i, h, norm2[i].reshape(1, d),
                                      w1_r, w2_r)
        h = _mlp(u2, w1_b, w2_b, 0, h)
        g_next = final_norm if last else norm1[i + 1]
        res = _ple(h, p_rows, i, w_pg_r, w_pp_r, ple_norm[i].reshape(1, d),
                   g_next.reshape(1, d), last)
        if last:
            out = res[0]
        else:
            h, u = res
    return out.reshape(batch, seq, d)
```

```python
import functools

import jax
import jax.numpy as jnp
from jax import lax
from jax.experimental import pallas as pl
from jax.experimental.pallas import tpu as pltpu

F32 = jnp.float32
BF16 = jnp.bfloat16

HEAD_DIM = 128
N_HEADS = 8
GROUP_WIDTH = N_HEADS * HEAD_DIM
ROT_DIM = HEAD_DIM // 4
ROPE_THETA = 500000.0
NORM_EPS = 1e-6
LOG2E = 1.4426950408889634
QUERY_SCALE = HEAD_DIM ** -0.5 * LOG2E
HGRN_CHUNK = 64
HGRN_GROUP = 16
HGRN_SUB = 16
HGRN_MAX_HALF_DECAY = 60.0
ATTN_BLOCK = 128
ATTN_TILE = 2048
ATTN_GROUP = 16
DILATIONS = (1, 4, 16)
VMEM_LIMIT = 56 * 1024 * 1024

NT_DIMS = (((1,), (1,)), ((), ()))
TN_DIMS = (((0,), (0,)), ((), ()))


def _params(semantics):
    return pltpu.CompilerParams(dimension_semantics=semantics, vmem_limit_bytes=VMEM_LIMIT)


def _rms(x, gain):
    ms = jnp.mean(x * x, axis=-1, keepdims=True)
    return x * lax.rsqrt(ms + NORM_EPS) * gain


def _silu(x):
    return x / (1.0 + jnp.exp(-x))


def _norm_kernel(x_ref, g_ref, o_ref):
    o_ref[...] = _rms(x_ref[...], g_ref[...]).astype(o_ref.dtype)


def _norm(x, gain, out_dtype, tm=512):
    m, d = x.shape
    return pl.pallas_call(
        _norm_kernel,
        grid=(m // tm,),
        in_specs=[pl.BlockSpec((tm, d), lambda i: (i, 0)),
                  pl.BlockSpec((1, d), lambda i: (0, 0))],
        out_specs=pl.BlockSpec((tm, d), lambda i: (i, 0)),
        out_shape=jax.ShapeDtypeStruct((m, d), out_dtype),
        compiler_params=_params(("parallel",)),
        name="rmsnorm",
    )(x, gain)


def _rope_kernel(pos_ref, invf_ref, cos_ref, sin_ref):
    ang = pos_ref[...] * invf_ref[...]
    lane = lax.broadcasted_iota(jnp.int32, ang.shape, 1)
    s = jnp.sin(ang)
    cos_ref[...] = jnp.cos(ang)
    sin_ref[...] = jnp.where(lane < HEAD_DIM // 2, -s, s)


def _rope_tables(positions, tr=1024):
    n = positions.size
    pos = positions.astype(F32).reshape(n, 1)
    half = ROT_DIM // 2
    inv = 1.0 / (ROPE_THETA ** (jnp.arange(0, ROT_DIM, 2, dtype=F32) / ROT_DIM))
    gap = jnp.zeros((HEAD_DIM // 2 - half,), F32)
    invf = jnp.concatenate([inv, gap, inv, gap]).reshape(1, HEAD_DIM)
    return pl.pallas_call(
        _rope_kernel,
        grid=(n // tr,),
        in_specs=[pl.BlockSpec((tr, 1), lambda i: (i, 0)),
                  pl.BlockSpec((1, HEAD_DIM), lambda i: (0, 0))],
        out_specs=[pl.BlockSpec((tr, HEAD_DIM), lambda i: (i, 0))] * 2,
        out_shape=[jax.ShapeDtypeStruct((n, HEAD_DIM), F32)] * 2,
        compiler_params=_params(("parallel",)),
        name="rope_tables",
    )(pos, invf)


def _rows_bf16(w):
    return w.astype(BF16).reshape(-1, w.shape[-1])


def _pair_rotary_lanes(w):
    half = ROT_DIM // 2
    lane = lax.broadcasted_iota(jnp.int32, w.shape, 1)
    up = pltpu.roll(w, HEAD_DIM - half, axis=1)
    down = pltpu.roll(w, HEAD_DIM // 2 - half, axis=1)
    moved = jnp.where(lane < HEAD_DIM // 2, up, down)
    keep = jnp.logical_or(lane < half, lane >= HEAD_DIM // 2 + half)
    return jnp.where(keep, w, moved)


def _rotary(x, cos, sin_signed):
    return x * cos + pltpu.roll(x, HEAD_DIM // 2, axis=1) * sin_signed


def _in_proj_kernel(q_tile, k_tile, x_ref, w_ref, cos_ref, sin_ref, o_ref, wb_ref):
    j = pl.program_id(0)
    first_row_tile = pl.program_id(1) == 0
    rotated = jnp.logical_or(j == q_tile, j == k_tile)
    plain = jnp.logical_not(rotated)
    heads = [slice(h * HEAD_DIM, (h + 1) * HEAD_DIM) for h in range(o_ref.shape[1] // HEAD_DIM)]

    @pl.when(jnp.logical_and(first_row_tile, plain))
    def _():
        wb_ref[...] = w_ref[...].astype(BF16)

    @pl.when(jnp.logical_and(first_row_tile, rotated))
    def _():
        for cols in heads:
            wb_ref[:, cols] = _pair_rotary_lanes(w_ref[:, cols]).astype(BF16)

    @pl.when(plain)
    def _():
        o_ref[...] = jnp.dot(x_ref[...], wb_ref[...], preferred_element_type=F32)

    @pl.when(rotated)
    def _():
        z = jnp.dot(x_ref[...], wb_ref[...], preferred_element_type=F32)
        scale = jnp.where(j == q_tile, QUERY_SCALE, 1.0)
        cos = cos_ref[...] * scale
        sin = sin_ref[...] * scale
        for cols in heads:
            o_ref[:, cols] = _rotary(z[:, cols], cos, sin)


def _in_proj(u, w_rows, layer, cos, sin, tm=1024, tn=GROUP_WIDTH):
    m, k = u.shape
    n = w_rows.shape[1]
    tab = pl.BlockSpec((tm, HEAD_DIM), lambda j, i: (i, 0))
    return pl.pallas_call(
        functools.partial(_in_proj_kernel, 4, 5),
        grid=(n // tn, m // tm),
        in_specs=[pl.BlockSpec((tm, k), lambda j, i: (i, 0)),
                  pl.BlockSpec((k, tn), lambda j, i: (layer, j)), tab, tab],
        out_specs=pl.BlockSpec((tm, tn), lambda j, i: (i, j)),
        out_shape=jax.ShapeDtypeStruct((m, n), F32),
        scratch_shapes=[pltpu.VMEM((k, tn), BF16)],
        compiler_params=_params(("parallel", "arbitrary")),
        name="in_proj",
    )(u, w_rows, cos, sin)


def _hgrn_kernel(layer, zq_ref, zf_ref, zi_ref, zg_ref, lbp_ref, gn_ref, o_ref,
                 st_ref, q_s, k_s, b_s):
    C, SB = HGRN_CHUNK, HGRN_SUB
    tb = zq_ref.shape[0]
    n_chunks = tb // C
    mid = C // 2 - 1

    @pl.when(pl.program_id(2) == 0)
    def _():
        st_ref[...] = jnp.zeros_like(st_ref)

    lbp = lbp_ref[...]
    e = jnp.exp(lbp - jnp.max(lbp, axis=0, keepdims=True))
    sm = e / jnp.sum(e, axis=0, keepdims=True)
    lb = jnp.zeros((1, HEAD_DIM), F32)
    for j in range(1, layer + 1):
        lb = lb + sm[j:j + 1, :]
    lb_pos = lb > 0.0
    log1m_lb = jnp.log1p(-lb)
    one_m_lb = 1.0 - lb
    gn = gn_ref[...]

    row = lax.broadcasted_iota(jnp.int32, (C, C), 0)
    col = lax.broadcasted_iota(jnp.int32, (C, C), 1)
    causal = row >= col
    tril = jnp.where(causal, 1.0, 0.0).astype(BF16)

    def log_decay(rows):
        zf = zf_ref[rows, :]
        en = jnp.exp(-jnp.abs(zf))
        one_p = 1.0 + en
        rcp = 1.0 / one_p
        small = en * rcp
        nonneg = zf >= 0.0
        k_s[rows, :] = one_m_lb * jnp.where(nonneg, small, rcp)
        log_sig = jnp.minimum(zf, 0.0) - jnp.log(one_p)
        if layer == 0:
            return log_sig
        f = lb + one_m_lb * jnp.where(nonneg, rcp, small)
        return jnp.where(lb_pos, jnp.log(f), log1m_lb + log_sig)

    worst = jnp.zeros((1, HEAD_DIM), F32)
    for g0 in range(0, n_chunks, HGRN_GROUP):
        chunks = [pl.ds(c * C, C) for c in range(g0, min(g0 + HGRN_GROUP, n_chunks))]
        lfs = [log_decay(rows) for rows in chunks]
        his = [lf.astype(BF16) for lf in lfs]
        los = [(lf - hi.astype(F32)).astype(BF16) for lf, hi in zip(lfs, his)]
        bbs = [jnp.dot(tril, jnp.concatenate([hi, lo], axis=1), preferred_element_type=F32)
               for hi, lo in zip(his, los)]
        for rows, bb in zip(chunks, bbs):
            b = bb[:, :HEAD_DIM] + bb[:, HEAD_DIM:]
            b_s[rows, :] = b
            worst = jnp.maximum(worst, jnp.maximum(-b[mid:mid + 1, :],
                                                   b[mid:mid + 1, :] - b[C - 1:C, :]))
        for rows in chunks:
            q_s[rows, :] = _silu(zq_ref[rows, :])
    factorisable = jnp.max(worst) <= HGRN_MAX_HALF_DECAY

    def finish(o, rows):
        o = _rms(o, gn) * _silu(zg_ref[rows, :])
        o_ref[rows, :] = o.astype(o_ref.dtype)

    def state_step(st, vb, kk, b):
        b_end = b[C - 1:C, :]
        kd = (kk * jnp.exp(b_end - b)).astype(BF16)
        return st * jnp.exp(b_end) + lax.dot_general(vb, kd, TN_DIMS, preferred_element_type=F32)

    @pl.when(factorisable)
    def _():
        st = st_ref[...]
        for g0 in range(0, n_chunks, HGRN_GROUP):
            chunks = [pl.ds(c * C, C) for c in range(g0, min(g0 + HGRN_GROUP, n_chunks))]
            bs = [b_s[rows, :] for rows in chunks]
            rs = [b[mid:mid + 1, :] for b in bs]
            ts = [(b - r) * LOG2E for b, r in zip(bs, rs)]
            q_mids = [q_s[rows, :] * jnp.exp2(t) for rows, t in zip(chunks, ts)]
            k_mids = [k_s[rows, :] * jnp.exp2(-t) for rows, t in zip(chunks, ts)]
            vbs = [zi_ref[rows, :].astype(BF16) for rows in chunks]
            scores = [lax.dot_general(qm.astype(BF16), km.astype(BF16), NT_DIMS,
                                      preferred_element_type=F32)
                      for qm, km in zip(q_mids, k_mids)]
            kds = [(km * jnp.exp(b[C - 1:C, :] - r)).astype(BF16)
                   for km, b, r in zip(k_mids, bs, rs)]
            upds = [lax.dot_general(vb, kd, TN_DIMS, preferred_element_type=F32)
                    for vb, kd in zip(vbs, kds)]
            qes = [(qm * jnp.exp(r)).astype(BF16) for qm, r in zip(q_mids, rs)]
            intras = [jnp.dot(jnp.where(causal, a, 0.0).astype(BF16), vb,
                              preferred_element_type=F32) for a, vb in zip(scores, vbs)]
            outs = []
            for qe, b, upd, intra in zip(qes, bs, upds, intras):
                outs.append(intra + lax.dot_general(qe, st.astype(BF16), NT_DIMS,
                                                    preferred_element_type=F32))
                st = st * jnp.exp(b[C - 1:C, :]) + upd
            for o, rows in zip(outs, chunks):
                finish(o, rows)
        st_ref[...] = st

    @pl.when(jnp.logical_not(factorisable))
    def _():
        ones = jnp.ones((HEAD_DIM, HEAD_DIM), BF16)
        sub_t = lax.broadcasted_iota(jnp.int32, (SB, HEAD_DIM), 0)

        def chunk(c, carry):
            r0 = pl.multiple_of(c * C, C)
            rows = pl.ds(r0, C)
            q, kk, b = q_s[rows, :], k_s[rows, :], b_s[rows, :]
            v = zi_ref[rows, :]
            vb = v.astype(BF16)
            st = st_ref[...]
            o_inter = lax.dot_general((q * jnp.exp(b)).astype(BF16), st.astype(BF16), NT_DIMS,
                                      preferred_element_type=F32)
            outs = []
            for i in range(C // SB):
                lo_r = i * SB
                q_i = q[lo_r:lo_r + SB, :]
                b_i = b[lo_r:lo_r + SB, :]
                o_i = o_inter[lo_r:lo_r + SB, :]
                if i > 0:
                    bref = b[lo_r - 1:lo_r, :]
                    qs = (q_i * jnp.exp(b_i - bref)).astype(BF16)
                    ks = (kk[:lo_r, :] * jnp.exp(bref - b[:lo_r, :])).astype(BF16)
                    a = lax.dot_general(qs, ks, NT_DIMS, preferred_element_type=F32)
                    o_i = o_i + jnp.dot(a.astype(BF16), vb[:lo_r, :], preferred_element_type=F32)
                slabs = []
                for s in range(SB):
                    b_row = b_s[pl.ds(r0 + lo_r + s, 1), :]
                    k_row = k_s[pl.ds(r0 + lo_r + s, 1), :]
                    w = q_i * k_row * jnp.exp(jnp.minimum(b_i - b_row, 0.0))
                    slabs.append(jnp.where(sub_t >= s, w, 0.0).astype(BF16))
                red = jnp.dot(jnp.concatenate(slabs, axis=0), ones, preferred_element_type=F32)
                for s in range(SB):
                    o_i = o_i + red[s * SB:(s + 1) * SB, :] * v[lo_r + s:lo_r + s + 1, :]
                outs.append(o_i)
            st_ref[...] = state_step(st, vb, kk, b)
            finish(jnp.concatenate(outs, axis=0), rows)
            return carry

        lax.fori_loop(0, n_chunks, chunk, 0)


def _hgrn(z, lb_param, g_norm, layer, batch, seq, tb=4096):
    nt = seq // tb
    nh = N_HEADS

    def col(off):
        return pl.BlockSpec((tb, HEAD_DIM), lambda b, h, t, off=off: (b * nt + t, off + h))

    return pl.pallas_call(
        functools.partial(_hgrn_kernel, layer),
        grid=(batch, nh, nt),
        in_specs=[col(0), col(nh), col(2 * nh), col(3 * nh),
                  pl.BlockSpec((lb_param.shape[0], HEAD_DIM), lambda b, h, t: (0, h)),
                  pl.BlockSpec((1, HEAD_DIM), lambda b, h, t: (0, 0))],
        out_specs=pl.BlockSpec((tb, HEAD_DIM), lambda b, h, t: (b * nt + t, h)),
        out_shape=jax.ShapeDtypeStruct((batch * seq, GROUP_WIDTH), BF16),
        scratch_shapes=[pltpu.VMEM((HEAD_DIM, HEAD_DIM), F32),
                        pltpu.VMEM((tb, HEAD_DIM), F32),
                        pltpu.VMEM((tb, HEAD_DIM), F32),
                        pltpu.VMEM((tb, HEAD_DIM), F32)],
        compiler_params=_params(("parallel", "parallel", "arbitrary")),
        name="hgrn2",
    )(z, z, z, z, lb_param, g_norm)


def _attn_kernel(q_ref, k_ref, kp_ref, v_ref, vp_ref, gn_ref, o_ref,
                 o_mid, c_mid, o_low, c_low, bias_buf, k_cache, v_cache):
    TQ, Q, G = ATTN_TILE, ATTN_BLOCK, ATTN_GROUP
    t = pl.program_id(2)
    o_bufs, c_bufs = (o_mid, o_low), (c_mid, c_low)

    @pl.when(t == 0)
    def _():
        k_cache[...] = jnp.zeros_like(k_cache)
        v_cache[...] = jnp.zeros_like(v_cache)

    qi = lax.broadcasted_iota(jnp.int32, (Q, 2 * Q), 0)
    kj = lax.broadcasted_iota(jnp.int32, (Q, 2 * Q), 1)
    dist = kj - qi
    band = jnp.where((dist >= 0) & (dist <= Q), 0.0, -jnp.inf)
    first_key = jnp.where(t > 0, 0, Q)
    bias_buf[0] = band
    bias_buf[1] = jnp.where(kj >= first_key, band, -jnp.inf)
    ones_v = jnp.ones((2 * Q, HEAD_DIM), BF16)
    gn = gn_ref[...]
    def run_branch(d, wider, narrower):
        nb_per_res = TQ // (Q * d)

        def contiguous(start):
            return pl.ds(start if isinstance(start, int) else pl.multiple_of(start, Q), Q)

        def rows_of(start):
            return pl.ds(start, Q, stride=d) if d > 1 else contiguous(start)

        def group(res, nb0, at_start):
            ids = []
            for j in range(G):
                if nb_per_res >= G:
                    ids.append((res, nb0 + j, at_start and j == 0))
                else:
                    ids.append((res + j // nb_per_res, j % nb_per_res, j % nb_per_res == 0))
            blocks = [(res_b + d * Q * nb_b, first) for res_b, nb_b, first in ids]

            def with_prev(cur_ref, prev_ref, cache, r0, first):
                cur = cur_ref[rows_of(r0), :].astype(BF16)
                if nb_per_res == 1:
                    prev = cache[r0]
                    fresh.append((cache, r0, cur))
                elif first:
                    prev = prev_ref[rows_of(r0 + TQ - d * Q), :].astype(BF16)
                else:
                    prev = cur_ref[rows_of(r0 - d * Q), :].astype(BF16)
                return jnp.concatenate([prev, cur], axis=0)

            fresh = []
            qs = [q_ref[rows_of(r0), :].astype(BF16) for r0, _ in blocks]
            ks = [with_prev(k_ref, kp_ref, k_cache, r0, first) for r0, first in blocks]
            ss = [lax.dot_general(q, k, NT_DIMS, preferred_element_type=F32)
                  + bias_buf[1 if first else 0] for q, k, (_, first) in zip(qs, ks, blocks)]
            ms = [jnp.max(s, axis=-1, keepdims=True) for s in ss]
            ps = [jnp.exp2(s - m).astype(BF16) for s, m in zip(ss, ms)]
            vs = [with_prev(v_ref, vp_ref, v_cache, r0, first) for r0, first in blocks]
            accs = [jnp.dot(p, jnp.concatenate([v, ones_v], axis=1), preferred_element_type=F32)
                    for p, v in zip(ps, vs)]
            for cache, r0, cur in fresh:
                cache[r0] = cur
            ls = [acc[:, HEAD_DIM:] for acc in accs]
            nums = [acc[:, :HEAD_DIM] for acc in accs]
            if wider is None:
                os_ = [num * (1.0 / l) for num, l in zip(nums, ls)]
                lses = [m + jnp.log2(l) for m, l in zip(ms, ls)]
            else:
                o_in, c_in = wider
                spans = [(res_b, contiguous(Q * nb_b)) for res_b, nb_b, _ in ids]
                c_ws = [c_in[res_b, rows, :] for res_b, rows in spans]
                deltas = [m - c_w for m, c_w in zip(ms, c_ws)]
                smalls = [jnp.exp2(-jnp.abs(delta)) for delta in deltas]
                w_own = [jnp.where(delta >= 0.0, 1.0, e) for delta, e in zip(deltas, smalls)]
                w_in = [jnp.where(delta >= 0.0, e, 1.0) for delta, e in zip(deltas, smalls)]
                dens = [wo * l + wi for wo, l, wi in zip(w_own, ls, w_in)]
                os_ = [(wo * num + wi * o_in[res_b, rows, :]) * (1.0 / den)
                       for wo, num, wi, (res_b, rows), den in zip(w_own, nums, w_in, spans, dens)]
                if narrower is not None:
                    lses = [jnp.maximum(m, c_w) + jnp.log2(den)
                            for m, c_w, den in zip(ms, c_ws, dens)]
            if narrower is not None:
                d_next, o_out, c_out = narrower
                ratio = d // d_next
                for (res_b, nb_b, _), o, lse in zip(ids, os_, lses):
                    rows = pl.ds(res_b // d_next + ratio * Q * nb_b, Q, stride=ratio)
                    o_out[res_b % d_next, rows, :] = o
                    c_out[res_b % d_next, rows, :] = lse
            else:
                for (r0, _), o in zip(blocks, os_):
                    o_ref[rows_of(r0), :] = _rms(o, gn).astype(o_ref.dtype)

        def loop(lo, hi, body):
            def step(it, carry):
                body(it)
                return carry
            lax.fori_loop(lo, hi, step, 0)

        if nb_per_res > G:
            for res in range(d):
                group(res, 0, True)
                loop(1, nb_per_res // G, lambda it: group(res, it * G, False))
        elif nb_per_res == G:
            loop(0, d, lambda res: group(res, 0, True))
        else:
            per_group = G // nb_per_res
            loop(0, d // per_group, lambda it: group(it * per_group, 0, True))

    order = sorted(DILATIONS, reverse=True)
    for k, d in enumerate(order):
        wider = (o_bufs[k - 1], c_bufs[k - 1]) if k > 0 else None
        narrower = (order[k + 1], o_bufs[k], c_bufs[k]) if k + 1 < len(order) else None
        run_branch(d, wider, narrower)


def _attn(z, g_norm, batch, seq):
    TQ = ATTN_TILE
    nt = seq // TQ
    nh = N_HEADS
    _, d_mid, d_low = sorted(DILATIONS, reverse=True)

    def col(off, back=0):
        return pl.BlockSpec((TQ, HEAD_DIM),
                            lambda b, h, t: (b * nt + jnp.maximum(t - back, 0), off + h))

    return pl.pallas_call(
        _attn_kernel,
        grid=(batch, nh, nt),
        in_specs=[col(4 * nh), col(5 * nh), col(5 * nh, 1), col(6 * nh), col(6 * nh, 1),
                  pl.BlockSpec((1, HEAD_DIM), lambda b, h, t: (0, 0))],
        out_specs=pl.BlockSpec((TQ, HEAD_DIM), lambda b, h, t: (b * nt + t, h)),
        out_shape=jax.ShapeDtypeStruct((batch * seq, GROUP_WIDTH), BF16),
        scratch_shapes=[pltpu.VMEM((d_mid, TQ // d_mid, HEAD_DIM), F32),
                        pltpu.VMEM((d_mid, TQ // d_mid, HEAD_DIM), F32),
                        pltpu.VMEM((d_low, TQ // d_low, HEAD_DIM), F32),
                        pltpu.VMEM((d_low, TQ // d_low, HEAD_DIM), F32),
                        pltpu.VMEM((2, ATTN_BLOCK, 2 * ATTN_BLOCK), F32),
                        pltpu.VMEM((max(DILATIONS), ATTN_BLOCK, HEAD_DIM), BF16),
                        pltpu.VMEM((max(DILATIONS), ATTN_BLOCK, HEAD_DIM), BF16)],
        compiler_params=_params(("parallel", "parallel", "arbitrary")),
        name="dilated_attn",
    )(z, z, z, z, z, g_norm)


def _out_proj_kernel(oh_ref, oa_ref, w_ref, h_ref, g_ref, w1_ref, w2_ref,
                     h_out_ref, u_out_ref, w1b_ref, w2b_ref):
    half = oh_ref.shape[1]
    acc = jnp.dot(oh_ref[...], w_ref[pl.ds(0, half), :], preferred_element_type=F32)
    acc = acc + jnp.dot(oa_ref[...], w_ref[pl.ds(half, half), :], preferred_element_type=F32)
    h = h_ref[...] + acc
    h_out_ref[...] = h
    u_out_ref[...] = _rms(h, g_ref[...]).astype(u_out_ref.dtype)
    w1b_ref[...] = w1_ref[...].astype(BF16)
    w2b_ref[...] = w2_ref[...].astype(BF16)


def _out_proj(oh, oa, w_rows, layer, h, gain, w1_rows, w2_rows, tm=512):
    m, d = h.shape
    half = oh.shape[1]
    steps = m // tm
    dff = w1_rows.shape[1]
    row = lambda i: (i, 0)
    fixed = lambda i: (0, 0)
    slab = lambda i: (layer * steps + i, 0)
    return pl.pallas_call(
        _out_proj_kernel,
        grid=(steps,),
        in_specs=[pl.BlockSpec((tm, half), row), pl.BlockSpec((tm, half), row),
                  pl.BlockSpec((2 * half, d), lambda i: (layer, 0)), pl.BlockSpec((tm, d), row),
                  pl.BlockSpec((1, d), fixed),
                  pl.BlockSpec((d // steps, dff), slab), pl.BlockSpec((dff // steps, d), slab)],
        out_specs=[pl.BlockSpec((tm, d), row), pl.BlockSpec((tm, d), row),
                   pl.BlockSpec((d // steps, dff), row), pl.BlockSpec((dff // steps, d), row)],
        out_shape=[jax.ShapeDtypeStruct((m, d), F32), jax.ShapeDtypeStruct((m, d), BF16),
                   jax.ShapeDtypeStruct((d, dff), BF16), jax.ShapeDtypeStruct((dff, d), BF16)],
        compiler_params=_params(("parallel",)),
        name="out_proj",
    )(oh, oa, w_rows, h, gain, w1_rows, w2_rows)


def _mlp_kernel(u_ref, w1_ref, w2_ref, h_ref, o_ref):
    f = pl.program_id(1)

    def step(base_ref):
        a = jnp.maximum(jnp.dot(u_ref[...], w1_ref[...], preferred_element_type=F32), 0.0)
        o_ref[...] = base_ref[...] + jnp.dot((a * a).astype(BF16), w2_ref[...],
                                             preferred_element_type=F32)

    @pl.when(f == 0)
    def _():
        step(h_ref)

    @pl.when(f > 0)
    def _():
        step(o_ref)


MLP_TF = 512


def _mlp(u, w1_rows, w2_rows, layer, h, tm=1024, tf=MLP_TF):
    m, d = h.shape
    nf = w1_rows.shape[1] // tf
    return pl.pallas_call(
        _mlp_kernel,
        grid=(m // tm, nf),
        in_specs=[pl.BlockSpec((tm, d), lambda i, f: (i, 0)),
                  pl.BlockSpec((d, tf), lambda i, f: (layer, f)),
                  pl.BlockSpec((tf, d), lambda i, f: (layer * nf + f, 0)),
                  pl.BlockSpec((tm, d), lambda i, f: (i, 0))],
        out_specs=pl.BlockSpec((tm, d), lambda i, f: (i, 0)),
        out_shape=jax.ShapeDtypeStruct((m, d), F32),
        compiler_params=_params(("parallel", "arbitrary")),
        name="mlp",
    )(u, w1_rows, w2_rows, h)


def _ple_kernel(last, h_ref, p_ref, wg_ref, wp_ref, gp_ref, gn_ref, *out_refs):
    h = h_ref[...]
    u = _rms(h, gp_ref[...]).astype(BF16)
    zg = jnp.dot(u, wg_ref[...], preferred_element_type=F32)
    gate = 1.0 / (1.0 + jnp.exp(-zg))
    pe = jnp.dot(p_ref[...].astype(BF16), wp_ref[...], preferred_element_type=F32)
    h = h + pe * gate
    nxt = _rms(h, gn_ref[...])
    if last:
        out_refs[0][...] = nxt
    else:
        out_refs[0][...] = h
        out_refs[1][...] = nxt.astype(out_refs[1].dtype)


def _ple(h, p, layer, wg_rows, wp_rows, g_ple, g_next, last, tm=512):
    m, d = h.shape
    pd = p.shape[1]
    row = lambda i: (i, 0)
    fixed = lambda i: (0, 0)
    of_layer = lambda i: (layer, 0)
    p_row = lambda i: (layer * (m // tm) + i, 0)
    if last:
        out_specs = [pl.BlockSpec((tm, d), row)]
        out_shape = [jax.ShapeDtypeStruct((m, d), F32)]
    else:
        out_specs = [pl.BlockSpec((tm, d), row), pl.BlockSpec((tm, d), row)]
        out_shape = [jax.ShapeDtypeStruct((m, d), F32), jax.ShapeDtypeStruct((m, d), BF16)]
    return pl.pallas_call(
        functools.partial(_ple_kernel, last),
        grid=(m // tm,),
        in_specs=[pl.BlockSpec((tm, d), row), pl.BlockSpec((tm, pd), p_row),
                  pl.BlockSpec((d, d), of_layer), pl.BlockSpec((pd, d), of_layer),
                  pl.BlockSpec((1, d), fixed), pl.BlockSpec((1, d), fixed)],
        out_specs=out_specs,
        out_shape=out_shape,
        compiler_params=_params(("parallel",)),
        name="ple",
    )(h, p, wg_rows, wp_rows, g_ple, g_next)


def kernel(x, p, positions, norm1, w_in, lb_param, hgrn_norm, attn_norm, w_out, norm2, w1, w2,
           ple_norm, w_pg, w_pp, final_norm):
    batch, seq, d = x.shape
    depth, _, in_cols = w_in.shape
    m = batch * seq
    assert seq % ATTN_TILE == 0 and d == 2 * GROUP_WIDTH

    w_in_r = w_in.reshape(depth * d, in_cols)
    w1_r, w2_r = w1.reshape(depth * d, -1), w2.reshape(-1, d)
    w_out_r, w_pg_r, w_pp_r = _rows_bf16(w_out), _rows_bf16(w_pg), _rows_bf16(w_pp)

    cos, sin = _rope_tables(positions)
    h = x.reshape(m, d)
    p_rows = p.reshape(depth * m, p.shape[-1])
    u = _norm(h, norm1[0].reshape(1, d), BF16)
    out = None
    for i in range(depth):
        last = i == depth - 1
        z = _in_proj(u, w_in_r, i, cos, sin)
        o_hgrn = _hgrn(z, lb_param, hgrn_norm[i].reshape(1, HEAD_DIM), i, batch, seq)
        o_attn = _attn(z, attn_norm[i].reshape(1, HEAD_DIM), batch, seq)
        h, u2, w1_b, w2_b = _out_proj(o_hgrn, o_attn, w_out_r, i, h, norm2[i].reshape(1, d),
                                      w1_r, w2_r)
        h = _mlp(u2, w1_b, w2_b, 0, h)
        g_next = final_norm if last else norm1[i + 1]
        res = _ple(h, p_rows, i, w_pg_r, w_pp_r, ple_norm[i].reshape(1, d),
                   g_next.reshape(1, d), last)
        if last:
            out = res[0]
        else:
            h, u = res
    return out.reshape(batch, seq, d)
```

```python
import functools

import jax
import jax.numpy as jnp
from jax import lax
from jax.experimental import pallas as pl
from jax.experimental.pallas import tpu as pltpu

F32 = jnp.float32
BF16 = jnp.bfloat16

HEAD_DIM = 128
N_HEADS = 8
GROUP_WIDTH = N_HEADS * HEAD_DIM
ROT_DIM = HEAD_DIM // 4
ROPE_THETA = 500000.0
NORM_EPS = 1e-6
LOG2E = 1.4426950408889634
QUERY_SCALE = HEAD_DIM ** -0.5 * LOG2E
HGRN_CHUNK = 64
HGRN_GROUP = 16
HGRN_SUB = 16
HGRN_MAX_HALF_DECAY = 60.0
ATTN_BLOCK = 128
ATTN_TILE = 2048
ATTN_GROUP = 8
DILATIONS = (1, 4, 16)
VMEM_LIMIT = 56 * 1024 * 1024

NT_DIMS = (((1,), (1,)), ((), ()))
TN_DIMS = (((0,), (0,)), ((), ()))


def _params(semantics):
    return pltpu.CompilerParams(dimension_semantics=semantics, vmem_limit_bytes=VMEM_LIMIT)


def _rms(x, gain):
    ms = jnp.mean(x * x, axis=-1, keepdims=True)
    return x * lax.rsqrt(ms + NORM_EPS) * gain


def _silu(x):
    return x / (1.0 + jnp.exp(-x))


def _norm_kernel(x_ref, g_ref, o_ref):
    o_ref[...] = _rms(x_ref[...], g_ref[...]).astype(o_ref.dtype)


def _norm(x, gain, out_dtype, tm=512):
    m, d = x.shape
    return pl.pallas_call(
        _norm_kernel,
        grid=(m // tm,),
        in_specs=[pl.BlockSpec((tm, d), lambda i: (i, 0)),
                  pl.BlockSpec((1, d), lambda i: (0, 0))],
        out_specs=pl.BlockSpec((tm, d), lambda i: (i, 0)),
        out_shape=jax.ShapeDtypeStruct((m, d), out_dtype),
        compiler_params=_params(("parallel",)),
        name="rmsnorm",
    )(x, gain)


def _rope_kernel(pos_ref, invf_ref, cos_ref, sin_ref):
    ang = pos_ref[...] * invf_ref[...]
    lane = lax.broadcasted_iota(jnp.int32, ang.shape, 1)
    s = jnp.sin(ang)
    cos_ref[...] = jnp.cos(ang)
    sin_ref[...] = jnp.where(lane < HEAD_DIM // 2, -s, s)


def _rope_tables(positions, tr=1024):
    n = positions.size
    pos = positions.astype(F32).reshape(n, 1)
    half = ROT_DIM // 2
    inv = 1.0 / (ROPE_THETA ** (jnp.arange(0, ROT_DIM, 2, dtype=F32) / ROT_DIM))
    gap = jnp.zeros((HEAD_DIM // 2 - half,), F32)
    invf = jnp.concatenate([inv, gap, inv, gap]).reshape(1, HEAD_DIM)
    return pl.pallas_call(
        _rope_kernel,
        grid=(n // tr,),
        in_specs=[pl.BlockSpec((tr, 1), lambda i: (i, 0)),
                  pl.BlockSpec((1, HEAD_DIM), lambda i: (0, 0))],
        out_specs=[pl.BlockSpec((tr, HEAD_DIM), lambda i: (i, 0))] * 2,
        out_shape=[jax.ShapeDtypeStruct((n, HEAD_DIM), F32)] * 2,
        compiler_params=_params(("parallel",)),
        name="rope_tables",
    )(pos, invf)


def _rows_bf16(w):
    return w.astype(BF16).reshape(-1, w.shape[-1])


def _pair_rotary_lanes(w):
    half = ROT_DIM // 2
    lane = lax.broadcasted_iota(jnp.int32, w.shape, 1)
    up = pltpu.roll(w, HEAD_DIM - half, axis=1)
    down = pltpu.roll(w, HEAD_DIM // 2 - half, axis=1)
    moved = jnp.where(lane < HEAD_DIM // 2, up, down)
    keep = jnp.logical_or(lane < half, lane >= HEAD_DIM // 2 + half)
    return jnp.where(keep, w, moved)


def _rotary(x, cos, sin_signed):
    return x * cos + pltpu.roll(x, HEAD_DIM // 2, axis=1) * sin_signed


def _in_proj_kernel(q_tile, k_tile, x_ref, w_ref, cos_ref, sin_ref, o_ref, wb_ref):
    j = pl.program_id(0)
    first_row_tile = pl.program_id(1) == 0
    rotated = jnp.logical_or(j == q_tile, j == k_tile)
    plain = jnp.logical_not(rotated)
    heads = [slice(h * HEAD_DIM, (h + 1) * HEAD_DIM) for h in range(o_ref.shape[1] // HEAD_DIM)]

    @pl.when(jnp.logical_and(first_row_tile, plain))
    def _():
        wb_ref[...] = w_ref[...].astype(BF16)

    @pl.when(jnp.logical_and(first_row_tile, rotated))
    def _():
        for cols in heads:
            wb_ref[:, cols] = _pair_rotary_lanes(w_ref[:, cols]).astype(BF16)

    @pl.when(plain)
    def _():
        o_ref[...] = jnp.dot(x_ref[...], wb_ref[...], preferred_element_type=F32)

    @pl.when(rotated)
    def _():
        z = jnp.dot(x_ref[...], wb_ref[...], preferred_element_type=F32)
        scale = jnp.where(j == q_tile, QUERY_SCALE, 1.0)
        cos = cos_ref[...] * scale
        sin = sin_ref[...] * scale
        for cols in heads:
            o_ref[:, cols] = _rotary(z[:, cols], cos, sin)


def _in_proj(u, w_rows, layer, cos, sin, tm=1024, tn=GROUP_WIDTH):
    m, k = u.shape
    n = w_rows.shape[1]
    tab = pl.BlockSpec((tm, HEAD_DIM), lambda j, i: (i, 0))
    return pl.pallas_call(
        functools.partial(_in_proj_kernel, 4, 5),
        grid=(n // tn, m // tm),
        in_specs=[pl.BlockSpec((tm, k), lambda j, i: (i, 0)),
                  pl.BlockSpec((k, tn), lambda j, i: (layer, j)), tab, tab],
        out_specs=pl.BlockSpec((tm, tn), lambda j, i: (i, j)),
        out_shape=jax.ShapeDtypeStruct((m, n), F32),
        scratch_shapes=[pltpu.VMEM((k, tn), BF16)],
        compiler_params=_params(("parallel", "arbitrary")),
        name="in_proj",
    )(u, w_rows, cos, sin)


def _hgrn_kernel(layer, zq_ref, zf_ref, zi_ref, zg_ref, lbp_ref, gn_ref, o_ref,
                 st_ref, q_s, k_s, b_s):
    C, SB = HGRN_CHUNK, HGRN_SUB
    tb = zq_ref.shape[0]
    n_chunks = tb // C
    mid = C // 2 - 1

    @pl.when(pl.program_id(2) == 0)
    def _():
        st_ref[...] = jnp.zeros_like(st_ref)

    lbp = lbp_ref[...]
    e = jnp.exp(lbp - jnp.max(lbp, axis=0, keepdims=True))
    sm = e / jnp.sum(e, axis=0, keepdims=True)
    lb = jnp.zeros((1, HEAD_DIM), F32)
    for j in range(1, layer + 1):
        lb = lb + sm[j:j + 1, :]
    lb_pos = lb > 0.0
    log1m_lb = jnp.log1p(-lb)
    one_m_lb = 1.0 - lb
    gn = gn_ref[...]

    row = lax.broadcasted_iota(jnp.int32, (C, C), 0)
    col = lax.broadcasted_iota(jnp.int32, (C, C), 1)
    causal = row >= col
    tril = jnp.where(causal, 1.0, 0.0).astype(BF16)

    def log_decay(rows):
        zf = zf_ref[rows, :]
        en = jnp.exp(-jnp.abs(zf))
        one_p = 1.0 + en
        rcp = 1.0 / one_p
        small = en * rcp
        nonneg = zf >= 0.0
        k_s[rows, :] = one_m_lb * jnp.where(nonneg, small, rcp)
        log_sig = jnp.minimum(zf, 0.0) - jnp.log(one_p)
        if layer == 0:
            return log_sig
        f = lb + one_m_lb * jnp.where(nonneg, rcp, small)
        return jnp.where(lb_pos, jnp.log(f), log1m_lb + log_sig)

    worst = jnp.zeros((1, HEAD_DIM), F32)
    for g0 in range(0, n_chunks, HGRN_GROUP):
        chunks = [pl.ds(c * C, C) for c in range(g0, min(g0 + HGRN_GROUP, n_chunks))]
        lfs = [log_decay(rows) for rows in chunks]
        his = [lf.astype(BF16) for lf in lfs]
        los = [(lf - hi.astype(F32)).astype(BF16) for lf, hi in zip(lfs, his)]
        bbs = [jnp.dot(tril, jnp.concatenate([hi, lo], axis=1), preferred_element_type=F32)
               for hi, lo in zip(his, los)]
        for rows, bb in zip(chunks, bbs):
            b = bb[:, :HEAD_DIM] + bb[:, HEAD_DIM:]
            b_s[rows, :] = b
            worst = jnp.maximum(worst, jnp.maximum(-b[mid:mid + 1, :],
                                                   b[mid:mid + 1, :] - b[C - 1:C, :]))
        for rows in chunks:
            q_s[rows, :] = _silu(zq_ref[rows, :])
    factorisable = jnp.max(worst) <= HGRN_MAX_HALF_DECAY

    def finish(o, rows):
        o = _rms(o, gn) * _silu(zg_ref[rows, :])
        o_ref[rows, :] = o.astype(o_ref.dtype)

    def state_step(st, vb, kk, b):
        b_end = b[C - 1:C, :]
        kd = (kk * jnp.exp(b_end - b)).astype(BF16)
        return st * jnp.exp(b_end) + lax.dot_general(vb, kd, TN_DIMS, preferred_element_type=F32)

    @pl.when(factorisable)
    def _():
        st = st_ref[...]
        for g0 in range(0, n_chunks, HGRN_GROUP):
            chunks = [pl.ds(c * C, C) for c in range(g0, min(g0 + HGRN_GROUP, n_chunks))]
            bs = [b_s[rows, :] for rows in chunks]
            rs = [b[mid:mid + 1, :] for b in bs]
            ts = [(b - r) * LOG2E for b, r in zip(bs, rs)]
            q_mids = [q_s[rows, :] * jnp.exp2(t) for rows, t in zip(chunks, ts)]
            k_mids = [k_s[rows, :] * jnp.exp2(-t) for rows, t in zip(chunks, ts)]
            vbs = [zi_ref[rows, :].astype(BF16) for rows in chunks]
            scores = [lax.dot_general(qm.astype(BF16), km.astype(BF16), NT_DIMS,
                                      preferred_element_type=F32)
                      for qm, km in zip(q_mids, k_mids)]
            kds = [(km * jnp.exp(b[C - 1:C, :] - r)).astype(BF16)
                   for km, b, r in zip(k_mids, bs, rs)]
            upds = [lax.dot_general(vb, kd, TN_DIMS, preferred_element_type=F32)
                    for vb, kd in zip(vbs, kds)]
            qes = [(qm * jnp.exp(r)).astype(BF16) for qm, r in zip(q_mids, rs)]
            intras = [jnp.dot(jnp.where(causal, a, 0.0).astype(BF16), vb,
                              preferred_element_type=F32) for a, vb in zip(scores, vbs)]
            outs = []
            for qe, b, upd, intra in zip(qes, bs, upds, intras):
                outs.append(intra + lax.dot_general(qe, st.astype(BF16), NT_DIMS,
                                                    preferred_element_type=F32))
                st = st * jnp.exp(b[C - 1:C, :]) + upd
            for o, rows in zip(outs, chunks):
                finish(o, rows)
        st_ref[...] = st

    @pl.when(jnp.logical_not(factorisable))
    def _():
        ones = jnp.ones((HEAD_DIM, HEAD_DIM), BF16)
        sub_t = lax.broadcasted_iota(jnp.int32, (SB, HEAD_DIM), 0)

        def chunk(c, carry):
            r0 = pl.multiple_of(c * C, C)
            rows = pl.ds(r0, C)
            q, kk, b = q_s[rows, :], k_s[rows, :], b_s[rows, :]
            v = zi_ref[rows, :]
            vb = v.astype(BF16)
            st = st_ref[...]
            o_inter = lax.dot_general((q * jnp.exp(b)).astype(BF16), st.astype(BF16), NT_DIMS,
                                      preferred_element_type=F32)
            outs = []
            for i in range(C // SB):
                lo_r = i * SB
                q_i = q[lo_r:lo_r + SB, :]
                b_i = b[lo_r:lo_r + SB, :]
                o_i = o_inter[lo_r:lo_r + SB, :]
                if i > 0:
                    bref = b[lo_r - 1:lo_r, :]
                    qs = (q_i * jnp.exp(b_i - bref)).astype(BF16)
                    ks = (kk[:lo_r, :] * jnp.exp(bref - b[:lo_r, :])).astype(BF16)
                    a = lax.dot_general(qs, ks, NT_DIMS, preferred_element_type=F32)
                    o_i = o_i + jnp.dot(a.astype(BF16), vb[:lo_r, :], preferred_element_type=F32)
                slabs = []
                for s in range(SB):
                    b_row = b_s[pl.ds(r0 + lo_r + s, 1), :]
                    k_row = k_s[pl.ds(r0 + lo_r + s, 1), :]
                    w = q_i * k_row * jnp.exp(jnp.minimum(b_i - b_row, 0.0))
                    slabs.append(jnp.where(sub_t >= s, w, 0.0).astype(BF16))
                red = jnp.dot(jnp.concatenate(slabs, axis=0), ones, preferred_element_type=F32)
                for s in range(SB):
                    o_i = o_i + red[s * SB:(s + 1) * SB, :] * v[lo_r + s:lo_r + s + 1, :]
                outs.append(o_i)
            st_ref[...] = state_step(st, vb, kk, b)
            finish(jnp.concatenate(outs, axis=0), rows)
            return carry

        lax.fori_loop(0, n_chunks, chunk, 0)


def _hgrn(z, lb_param, g_norm, layer, batch, seq, tb=4096):
    nt = seq // tb
    nh = N_HEADS

    def col(off):
        return pl.BlockSpec((tb, HEAD_DIM), lambda b, h, t, off=off: (b * nt + t, off + h))

    return pl.pallas_call(
        functools.partial(_hgrn_kernel, layer),
        grid=(batch, nh, nt),
        in_specs=[col(0), col(nh), col(2 * nh), col(3 * nh),
                  pl.BlockSpec((lb_param.shape[0], HEAD_DIM), lambda b, h, t: (0, h)),
                  pl.BlockSpec((1, HEAD_DIM), lambda b, h, t: (0, 0))],
        out_specs=pl.BlockSpec((tb, HEAD_DIM), lambda b, h, t: (b * nt + t, h)),
        out_shape=jax.ShapeDtypeStruct((batch * seq, GROUP_WIDTH), BF16),
        scratch_shapes=[pltpu.VMEM((HEAD_DIM, HEAD_DIM), F32),
                        pltpu.VMEM((tb, HEAD_DIM), F32),
                        pltpu.VMEM((tb, HEAD_DIM), F32),
                        pltpu.VMEM((tb, HEAD_DIM), F32)],
        compiler_params=_params(("parallel", "parallel", "arbitrary")),
        name="hgrn2",
    )(z, z, z, z, lb_param, g_norm)


def _attn_kernel(q_ref, k_ref, kp_ref, v_ref, vp_ref, gn_ref, o_ref,
                 o_mid, c_mid, o_low, c_low, bias_buf, k_cache, v_cache):
    TQ, Q, G = ATTN_TILE, ATTN_BLOCK, ATTN_GROUP
    t = pl.program_id(2)
    o_bufs, c_bufs = (o_mid, o_low), (c_mid, c_low)

    @pl.when(t == 0)
    def _():
        k_cache[...] = jnp.zeros_like(k_cache)
        v_cache[...] = jnp.zeros_like(v_cache)

    qi = lax.broadcasted_iota(jnp.int32, (Q, 2 * Q), 0)
    kj = lax.broadcasted_iota(jnp.int32, (Q, 2 * Q), 1)
    dist = kj - qi
    band = jnp.where((dist >= 0) & (dist <= Q), 0.0, -jnp.inf)
    first_key = jnp.where(t > 0, 0, Q)
    bias_buf[0] = band
    bias_buf[1] = jnp.where(kj >= first_key, band, -jnp.inf)
    ones_v = jnp.ones((2 * Q, HEAD_DIM), BF16)
    gn = gn_ref[...]
    def run_branch(d, wider, narrower):
        nb_per_res = TQ // (Q * d)

        def contiguous(start):
            return pl.ds(start if isinstance(start, int) else pl.multiple_of(start, Q), Q)

        def rows_of(start):
            return pl.ds(start, Q, stride=d) if d > 1 else contiguous(start)

        def group(res, nb0, at_start):
            ids = []
            for j in range(G):
                if nb_per_res >= G:
                    ids.append((res, nb0 + j, at_start and j == 0))
                else:
                    ids.append((res + j // nb_per_res, j % nb_per_res, j % nb_per_res == 0))
            blocks = [(res_b + d * Q * nb_b, first) for res_b, nb_b, first in ids]

            def with_prev(cur_ref, prev_ref, cache, r0, first):
                cur = cur_ref[rows_of(r0), :].astype(BF16)
                if nb_per_res == 1:
                    prev = cache[r0]
                    fresh.append((cache, r0, cur))
                elif first:
                    prev = prev_ref[rows_of(r0 + TQ - d * Q), :].astype(BF16)
                else:
                    prev = cur_ref[rows_of(r0 - d * Q), :].astype(BF16)
                return jnp.concatenate([prev, cur], axis=0)

            fresh = []
            qs = [q_ref[rows_of(r0), :].astype(BF16) for r0, _ in blocks]
            ks = [with_prev(k_ref, kp_ref, k_cache, r0, first) for r0, first in blocks]
            ss = [lax.dot_general(q, k, NT_DIMS, preferred_element_type=F32)
                  + bias_buf[1 if first else 0] for q, k, (_, first) in zip(qs, ks, blocks)]
            ms = [jnp.max(s, axis=-1, keepdims=True) for s in ss]
            ps = [jnp.exp2(s - m).astype(BF16) for s, m in zip(ss, ms)]
            vs = [with_prev(v_ref, vp_ref, v_cache, r0, first) for r0, first in blocks]
            accs = [jnp.dot(p, jnp.concatenate([v, ones_v], axis=1), preferred_element_type=F32)
                    for p, v in zip(ps, vs)]
            for cache, r0, cur in fresh:
                cache[r0] = cur
            ls = [acc[:, HEAD_DIM:] for acc in accs]
            nums = [acc[:, :HEAD_DIM] for acc in accs]
            if wider is None:
                os_ = [num * (1.0 / l) for num, l in zip(nums, ls)]
                lses = [m + jnp.log2(l) for m, l in zip(ms, ls)]
            else:
                o_in, c_in = wider
                spans = [(res_b, contiguous(Q * nb_b)) for res_b, nb_b, _ in ids]
                c_ws = [c_in[res_b, rows, :] for res_b, rows in spans]
                deltas = [m - c_w for m, c_w in zip(ms, c_ws)]
                smalls = [jnp.exp2(-jnp.abs(delta)) for delta in deltas]
                w_own = [jnp.where(delta >= 0.0, 1.0, e) for delta, e in zip(deltas, smalls)]
                w_in = [jnp.where(delta >= 0.0, e, 1.0) for delta, e in zip(deltas, smalls)]
                dens = [wo * l + wi for wo, l, wi in zip(w_own, ls, w_in)]
                os_ = [(wo * num + wi * o_in[res_b, rows, :]) * (1.0 / den)
                       for wo, num, wi, (res_b, rows), den in zip(w_own, nums, w_in, spans, dens)]
                if narrower is not None:
                    lses = [jnp.maximum(m, c_w) + jnp.log2(den)
                            for m, c_w, den in zip(ms, c_ws, dens)]
            if narrower is not None:
                d_next, o_out, c_out = narrower
                ratio = d // d_next
                for (res_b, nb_b, _), o, lse in zip(ids, os_, lses):
                    rows = pl.ds(res_b // d_next + ratio * Q * nb_b, Q, stride=ratio)
                    o_out[res_b % d_next, rows, :] = o
                    c_out[res_b % d_next, rows, :] = lse
            else:
                for (r0, _), o in zip(blocks, os_):
                    o_ref[rows_of(r0), :] = _rms(o, gn).astype(o_ref.dtype)

        def loop(lo, hi, body):
            for it in range(lo, hi):
                body(it)

        if nb_per_res > G:
            for res in range(d):
                group(res, 0, True)
                loop(1, nb_per_res // G, lambda it: group(res, it * G, False))
        elif nb_per_res == G:
            loop(0, d, lambda res: group(res, 0, True))
        else:
            per_group = G // nb_per_res
            loop(0, d // per_group, lambda it: group(it * per_group, 0, True))

    order = sorted(DILATIONS, reverse=True)
    for k, d in enumerate(order):
        wider = (o_bufs[k - 1], c_bufs[k - 1]) if k > 0 else None
        narrower = (order[k + 1], o_bufs[k], c_bufs[k]) if k + 1 < len(order) else None
        run_branch(d, wider, narrower)


def _attn(z, g_norm, batch, seq):
    TQ = ATTN_TILE
    nt = seq // TQ
    nh = N_HEADS
    _, d_mid, d_low = sorted(DILATIONS, reverse=True)

    def col(off, back=0):
        return pl.BlockSpec((TQ, HEAD_DIM),
                            lambda b, h, t: (b * nt + jnp.maximum(t - back, 0), off + h))

    return pl.pallas_call(
        _attn_kernel,
        grid=(batch, nh, nt),
        in_specs=[col(4 * nh), col(5 * nh), col(5 * nh, 1), col(6 * nh), col(6 * nh, 1),
                  pl.BlockSpec((1, HEAD_DIM), lambda b, h, t: (0, 0))],
        out_specs=pl.BlockSpec((TQ, HEAD_DIM), lambda b, h, t: (b * nt + t, h)),
        out_shape=jax.ShapeDtypeStruct((batch * seq, GROUP_WIDTH), BF16),
        scratch_shapes=[pltpu.VMEM((d_mid, TQ // d_mid, HEAD_DIM), F32),
                        pltpu.VMEM((d_mid, TQ // d_mid, HEAD_DIM), F32),
                        pltpu.VMEM((d_low, TQ // d_low, HEAD_DIM), F32),
                        pltpu.VMEM((d_low, TQ // d_low, HEAD_DIM), F32),
                        pltpu.VMEM((2, ATTN_BLOCK, 2 * ATTN_BLOCK), F32),
                        pltpu.VMEM((max(DILATIONS), ATTN_BLOCK, HEAD_DIM), BF16),
                        pltpu.VMEM((max(DILATIONS), ATTN_BLOCK, HEAD_DIM), BF16)],
        compiler_params=_params(("parallel", "parallel", "arbitrary")),
        name="dilated_attn",
    )(z, z, z, z, z, g_norm)


def _out_proj_kernel(oh_ref, oa_ref, w_ref, h_ref, g_ref, w1_ref, w2_ref,
                     h_out_ref, u_out_ref, w1b_ref, w2b_ref):
    half = oh_ref.shape[1]
    acc = jnp.dot(oh_ref[...], w_ref[pl.ds(0, half), :], preferred_element_type=F32)
    acc = acc + jnp.dot(oa_ref[...], w_ref[pl.ds(half, half), :], preferred_element_type=F32)
    h = h_ref[...] + acc
    h_out_ref[...] = h
    u_out_ref[...] = _rms(h, g_ref[...]).astype(u_out_ref.dtype)
    w1b_ref[...] = w1_ref[...].astype(BF16)
    w2b_ref[...] = w2_ref[...].astype(BF16)


def _out_proj(oh, oa, w_rows, layer, h, gain, w1_rows, w2_rows, tm=512):
    m, d = h.shape
    half = oh.shape[1]
    steps = m // tm
    dff = w1_rows.shape[1]
    row = lambda i: (i, 0)
    fixed = lambda i: (0, 0)
    slab = lambda i: (layer * steps + i, 0)
    return pl.pallas_call(
        _out_proj_kernel,
        grid=(steps,),
        in_specs=[pl.BlockSpec((tm, half), row), pl.BlockSpec((tm, half), row),
                  pl.BlockSpec((2 * half, d), lambda i: (layer, 0)), pl.BlockSpec((tm, d), row),
                  pl.BlockSpec((1, d), fixed),
                  pl.BlockSpec((d // steps, dff), slab), pl.BlockSpec((dff // steps, d), slab)],
        out_specs=[pl.BlockSpec((tm, d), row), pl.BlockSpec((tm, d), row),
                   pl.BlockSpec((d // steps, dff), row), pl.BlockSpec((dff // steps, d), row)],
        out_shape=[jax.ShapeDtypeStruct((m, d), F32), jax.ShapeDtypeStruct((m, d), BF16),
                   jax.ShapeDtypeStruct((d, dff), BF16), jax.ShapeDtypeStruct((dff, d), BF16)],
        compiler_params=_params(("parallel",)),
        name="out_proj",
    )(oh, oa, w_rows, h, gain, w1_rows, w2_rows)


def _mlp_kernel(u_ref, w1_ref, w2_ref, h_ref, o_ref):
    f = pl.program_id(1)

    def step(base_ref):
        a = jnp.maximum(jnp.dot(u_ref[...], w1_ref[...], preferred_element_type=F32), 0.0)
        o_ref[...] = base_ref[...] + jnp.dot((a * a).astype(BF16), w2_ref[...],
                                             preferred_element_type=F32)

    @pl.when(f == 0)
    def _():
        step(h_ref)

    @pl.when(f > 0)
    def _():
        step(o_ref)


MLP_TF = 512


def _mlp(u, w1_rows, w2_rows, layer, h, tm=1024, tf=MLP_TF):
    m, d = h.shape
    nf = w1_rows.shape[1] // tf
    return pl.pallas_call(
        _mlp_kernel,
        grid=(m // tm, nf),
        in_specs=[pl.BlockSpec((tm, d), lambda i, f: (i, 0)),
                  pl.BlockSpec((d, tf), lambda i, f: (layer, f)),
                  pl.BlockSpec((tf, d), lambda i, f: (layer * nf + f, 0)),
                  pl.BlockSpec((tm, d), lambda i, f: (i, 0))],
        out_specs=pl.BlockSpec((tm, d), lambda i, f: (i, 0)),
        out_shape=jax.ShapeDtypeStruct((m, d), F32),
        compiler_params=_params(("parallel", "arbitrary")),
        name="mlp",
    )(u, w1_rows, w2_rows, h)


def _ple_kernel(last, h_ref, p_ref, wg_ref, wp_ref, gp_ref, gn_ref, *out_refs):
    h = h_ref[...]
    u = _rms(h, gp_ref[...]).astype(BF16)
    zg = jnp.dot(u, wg_ref[...], preferred_element_type=F32)
    gate = 1.0 / (1.0 + jnp.exp(-zg))
    pe = jnp.dot(p_ref[...].astype(BF16), wp_ref[...], preferred_element_type=F32)
    h = h + pe * gate
    nxt = _rms(h, gn_ref[...])
    if last:
        out_refs[0][...] = nxt
    else:
        out_refs[0][...] = h
        out_refs[1][...] = nxt.astype(out_refs[1].dtype)


def _ple(h, p, layer, wg_rows, wp_rows, g_ple, g_next, last, tm=512):
    m, d = h.shape
    pd = p.shape[1]
    row = lambda i: (i, 0)
    fixed = lambda i: (0, 0)
    of_layer = lambda i: (layer, 0)
    p_row = lambda i: (layer * (m // tm) + i, 0)
    if last:
        out_specs = [pl.BlockSpec((tm, d), row)]
        out_shape = [jax.ShapeDtypeStruct((m, d), F32)]
    else:
        out_specs = [pl.BlockSpec((tm, d), row), pl.BlockSpec((tm, d), row)]
        out_shape = [jax.ShapeDtypeStruct((m, d), F32), jax.ShapeDtypeStruct((m, d), BF16)]
    return pl.pallas_call(
        functools.partial(_ple_kernel, last),
        grid=(m // tm,),
        in_specs=[pl.BlockSpec((tm, d), row), pl.BlockSpec((tm, pd), p_row),
                  pl.BlockSpec((d, d), of_layer), pl.BlockSpec((pd, d), of_layer),
                  pl.BlockSpec((1, d), fixed), pl.BlockSpec((1, d), fixed)],
        out_specs=out_specs,
        out_shape=out_shape,
        compiler_params=_params(("parallel",)),
        name="ple",
    )(h, p, wg_rows, wp_rows, g_ple, g_next)


def kernel(x, p, positions, norm1, w_in, lb_param, hgrn_norm, attn_norm, w_out, norm2, w1, w2,
           ple_norm, w_pg, w_pp, final_norm):
    batch, seq, d = x.shape
    depth, _, in_cols = w_in.shape
    m = batch * seq
    assert seq % ATTN_TILE == 0 and d == 2 * GROUP_WIDTH

    w_in_r = w_in.reshape(depth * d, in_cols)
    w1_r, w2_r = w1.reshape(depth * d, -1), w2.reshape(-1, d)
    w_out_r, w_pg_r, w_pp_r = _rows_bf16(w_out), _rows_bf16(w_pg), _rows_bf16(w_pp)

    cos, sin = _rope_tables(positions)
    h = x.reshape(m, d)
    p_rows = p.reshape(depth * m, p.shape[-1])
    u = _norm(h, norm1[0].reshape(1, d), BF16)
    out = None
    for i in range(depth):
        last = i == depth - 1
        z = _in_proj(u, w_in_r, i, cos, sin)
        o_hgrn = _hgrn(z, lb_param, hgrn_norm[i].reshape(1, HEAD_DIM), i, batch, seq)
        o_attn = _attn(z, attn_norm[i].reshape(1, HEAD_DIM), batch, seq)
        h, u2, w1_b, w2_b = _out_proj(o_hgrn, o_attn, w_out_r, i, h, norm2[i].reshape(1, d),
                                      w1_r, w2_r)
        h = _mlp(u2, w1_b, w2_b, 0, h)
        g_next = final_norm if last else norm1[i + 1]
        res = _ple(h, p_rows, i, w_pg_r, w_pp_r, ple_norm[i].reshape(1, d),
                   g_next.reshape(1, d), last)
        if last:
            out = res[0]
        else:
            h, u = res
    return out.reshape(batch, seq, d)
```

```python
import functools

import jax
import jax.numpy as jnp
from jax import lax
from jax.experimental import pallas as pl
from jax.experimental.pallas import tpu as pltpu

F32 = jnp.float32
BF16 = jnp.bfloat16

HEAD_DIM = 128
N_HEADS = 8
GROUP_WIDTH = N_HEADS * HEAD_DIM
ROT_DIM = HEAD_DIM // 4
ROPE_THETA = 500000.0
NORM_EPS = 1e-6
LOG2E = 1.4426950408889634
QUERY_SCALE = HEAD_DIM ** -0.5 * LOG2E
HGRN_CHUNK = 64
HGRN_GROUP = 16
HGRN_SUB = 16
HGRN_MAX_HALF_DECAY = 60.0
ATTN_BLOCK = 128
ATTN_TILE = 2048
ATTN_GROUP = 8
DILATIONS = (1, 4, 16)
VMEM_LIMIT = 56 * 1024 * 1024

NT_DIMS = (((1,), (1,)), ((), ()))
TN_DIMS = (((0,), (0,)), ((), ()))


def _params(semantics):
    return pltpu.CompilerParams(dimension_semantics=semantics, vmem_limit_bytes=VMEM_LIMIT)


def _rms(x, gain):
    ms = jnp.mean(x * x, axis=-1, keepdims=True)
    return x * lax.rsqrt(ms + NORM_EPS) * gain


def _silu(x):
    return x / (1.0 + jnp.exp(-x))


def _norm_kernel(x_ref, g_ref, o_ref):
    o_ref[...] = _rms(x_ref[...], g_ref[...]).astype(o_ref.dtype)


def _norm(x, gain, out_dtype, tm=512):
    m, d = x.shape
    return pl.pallas_call(
        _norm_kernel,
        grid=(m // tm,),
        in_specs=[pl.BlockSpec((tm, d), lambda i: (i, 0)),
                  pl.BlockSpec((1, d), lambda i: (0, 0))],
        out_specs=pl.BlockSpec((tm, d), lambda i: (i, 0)),
        out_shape=jax.ShapeDtypeStruct((m, d), out_dtype),
        compiler_params=_params(("parallel",)),
        name="rmsnorm",
    )(x, gain)


def _rope_kernel(pos_ref, invf_ref, cos_ref, sin_ref):
    ang = pos_ref[...] * invf_ref[...]
    lane = lax.broadcasted_iota(jnp.int32, ang.shape, 1)
    s = jnp.sin(ang)
    cos_ref[...] = jnp.cos(ang)
    sin_ref[...] = jnp.where(lane < HEAD_DIM // 2, -s, s)


def _rope_tables(positions, tr=1024):
    n = positions.size
    pos = positions.astype(F32).reshape(n, 1)
    half = ROT_DIM // 2
    inv = 1.0 / (ROPE_THETA ** (jnp.arange(0, ROT_DIM, 2, dtype=F32) / ROT_DIM))
    gap = jnp.zeros((HEAD_DIM // 2 - half,), F32)
    invf = jnp.concatenate([inv, gap, inv, gap]).reshape(1, HEAD_DIM)
    return pl.pallas_call(
        _rope_kernel,
        grid=(n // tr,),
        in_specs=[pl.BlockSpec((tr, 1), lambda i: (i, 0)),
                  pl.BlockSpec((1, HEAD_DIM), lambda i: (0, 0))],
        out_specs=[pl.BlockSpec((tr, HEAD_DIM), lambda i: (i, 0))] * 2,
        out_shape=[jax.ShapeDtypeStruct((n, HEAD_DIM), F32)] * 2,
        compiler_params=_params(("parallel",)),
        name="rope_tables",
    )(pos, invf)


def _rows_bf16(w):
    return w.astype(BF16).reshape(-1, w.shape[-1])


def _pair_rotary_lanes(w):
    half = ROT_DIM // 2
    lane = lax.broadcasted_iota(jnp.int32, w.shape, 1)
    up = pltpu.roll(w, HEAD_DIM - half, axis=1)
    down = pltpu.roll(w, HEAD_DIM // 2 - half, axis=1)
    moved = jnp.where(lane < HEAD_DIM // 2, up, down)
    keep = jnp.logical_or(lane < half, lane >= HEAD_DIM // 2 + half)
    return jnp.where(keep, w, moved)


def _rotary(x, cos, sin_signed):
    return x * cos + pltpu.roll(x, HEAD_DIM // 2, axis=1) * sin_signed


def _in_proj_kernel(q_tile, k_tile, x_ref, w_ref, cos_ref, sin_ref, o_ref, wb_ref):
    j = pl.program_id(0)
    first_row_tile = pl.program_id(1) == 0
    rotated = jnp.logical_or(j == q_tile, j == k_tile)
    plain = jnp.logical_not(rotated)
    heads = [slice(h * HEAD_DIM, (h + 1) * HEAD_DIM) for h in range(o_ref.shape[1] // HEAD_DIM)]

    @pl.when(jnp.logical_and(first_row_tile, plain))
    def _():
        wb_ref[...] = w_ref[...].astype(BF16)

    @pl.when(jnp.logical_and(first_row_tile, rotated))
    def _():
        for cols in heads:
            wb_ref[:, cols] = _pair_rotary_lanes(w_ref[:, cols]).astype(BF16)

    @pl.when(plain)
    def _():
        o_ref[...] = jnp.dot(x_ref[...], wb_ref[...], preferred_element_type=F32)

    @pl.when(rotated)
    def _():
        z = jnp.dot(x_ref[...], wb_ref[...], preferred_element_type=F32)
        scale = jnp.where(j == q_tile, QUERY_SCALE, 1.0)
        cos = cos_ref[...] * scale
        sin = sin_ref[...] * scale
        for cols in heads:
            o_ref[:, cols] = _rotary(z[:, cols], cos, sin)


def _in_proj(u, w_rows, layer, cos, sin, tm=1024, tn=GROUP_WIDTH):
    m, k = u.shape
    n = w_rows.shape[1]
    tab = pl.BlockSpec((tm, HEAD_DIM), lambda j, i: (i, 0))
    return pl.pallas_call(
        functools.partial(_in_proj_kernel, 4, 5),
        grid=(n // tn, m // tm),
        in_specs=[pl.BlockSpec((tm, k), lambda j, i: (i, 0)),
                  pl.BlockSpec((k, tn), lambda j, i: (layer, j)), tab, tab],
        out_specs=pl.BlockSpec((tm, tn), lambda j, i: (i, j)),
        out_shape=jax.ShapeDtypeStruct((m, n), F32),
        scratch_shapes=[pltpu.VMEM((k, tn), BF16)],
        compiler_params=_params(("parallel", "arbitrary")),
        name="in_proj",
    )(u, w_rows, cos, sin)


def _hgrn_kernel(layer, zq_ref, zf_ref, zi_ref, zg_ref, lbp_ref, gn_ref, o_ref,
                 st_ref, q_s, k_s, b_s):
    C, SB = HGRN_CHUNK, HGRN_SUB
    tb = zq_ref.shape[0]
    n_chunks = tb // C
    mid = C // 2 - 1

    @pl.when(pl.program_id(2) == 0)
    def _():
        st_ref[...] = jnp.zeros_like(st_ref)

    lbp = lbp_ref[...]
    e = jnp.exp(lbp - jnp.max(lbp, axis=0, keepdims=True))
    sm = e / jnp.sum(e, axis=0, keepdims=True)
    lb = jnp.zeros((1, HEAD_DIM), F32)
    for j in range(1, layer + 1):
        lb = lb + sm[j:j + 1, :]
    lb_pos = lb > 0.0
    log1m_lb = jnp.log1p(-lb)
    one_m_lb = 1.0 - lb
    gn = gn_ref[...]

    row = lax.broadcasted_iota(jnp.int32, (C, C), 0)
    col = lax.broadcasted_iota(jnp.int32, (C, C), 1)
    causal = row >= col
    tril = jnp.where(causal, 1.0, 0.0).astype(BF16)

    def log_decay(rows):
        zf = zf_ref[rows, :]
        en = jnp.exp(-jnp.abs(zf))
        one_p = 1.0 + en
        rcp = 1.0 / one_p
        small = en * rcp
        nonneg = zf >= 0.0
        k_s[rows, :] = one_m_lb * jnp.where(nonneg, small, rcp)
        log_sig = jnp.minimum(zf, 0.0) - jnp.log(one_p)
        if layer == 0:
            return log_sig
        f = lb + one_m_lb * jnp.where(nonneg, rcp, small)
        return jnp.where(lb_pos, jnp.log(f), log1m_lb + log_sig)

    worst = jnp.zeros((1, HEAD_DIM), F32)
    for g0 in range(0, n_chunks, HGRN_GROUP):
        chunks = [pl.ds(c * C, C) for c in range(g0, min(g0 + HGRN_GROUP, n_chunks))]
        lfs = [log_decay(rows) for rows in chunks]
        his = [lf.astype(BF16) for lf in lfs]
        los = [(lf - hi.astype(F32)).astype(BF16) for lf, hi in zip(lfs, his)]
        bbs = [jnp.dot(tril, jnp.concatenate([hi, lo], axis=1), preferred_element_type=F32)
               for hi, lo in zip(his, los)]
        for rows, bb in zip(chunks, bbs):
            b = bb[:, :HEAD_DIM] + bb[:, HEAD_DIM:]
            b_s[rows, :] = b
            worst = jnp.maximum(worst, jnp.maximum(-b[mid:mid + 1, :],
                                                   b[mid:mid + 1, :] - b[C - 1:C, :]))
        for rows in chunks:
            q_s[rows, :] = _silu(zq_ref[rows, :])
    factorisable = jnp.max(worst) <= HGRN_MAX_HALF_DECAY

    def finish(o, rows):
        o = _rms(o, gn) * _silu(zg_ref[rows, :])
        o_ref[rows, :] = o.astype(o_ref.dtype)

    def state_step(st, vb, kk, b):
        b_end = b[C - 1:C, :]
        kd = (kk * jnp.exp(b_end - b)).astype(BF16)
        return st * jnp.exp(b_end) + lax.dot_general(vb, kd, TN_DIMS, preferred_element_type=F32)

    @pl.when(factorisable)
    def _():
        st = st_ref[...]
        for g0 in range(0, n_chunks, HGRN_GROUP):
            chunks = [pl.ds(c * C, C) for c in range(g0, min(g0 + HGRN_GROUP, n_chunks))]
            bs = [b_s[rows, :] for rows in chunks]
            rs = [b[mid:mid + 1, :] for b in bs]
            ts = [(b - r) * LOG2E for b, r in zip(bs, rs)]
            q_mids = [q_s[rows, :] * jnp.exp2(t) for rows, t in zip(chunks, ts)]
            k_mids = [k_s[rows, :] * jnp.exp2(-t) for rows, t in zip(chunks, ts)]
            vbs = [zi_ref[rows, :].astype(BF16) for rows in chunks]
            scores = [lax.dot_general(qm.astype(BF16), km.astype(BF16), NT_DIMS,
                                      preferred_element_type=F32)
                      for qm, km in zip(q_mids, k_mids)]
            kds = [(km * jnp.exp(b[C - 1:C, :] - r)).astype(BF16)
                   for km, b, r in zip(k_mids, bs, rs)]
            upds = [lax.dot_general(vb, kd, TN_DIMS, preferred_element_type=F32)
                    for vb, kd in zip(vbs, kds)]
            qes = [(qm * jnp.exp(r)).astype(BF16) for qm, r in zip(q_mids, rs)]
            intras = [jnp.dot(jnp.where(causal, a, 0.0).astype(BF16), vb,
                              preferred_element_type=F32) for a, vb in zip(scores, vbs)]
            outs = []
            for qe, b, upd, intra in zip(qes, bs, upds, intras):
                outs.append(intra + lax.dot_general(qe, st.astype(BF16), NT_DIMS,
                                                    preferred_element_type=F32))
                st = st * jnp.exp(b[C - 1:C, :]) + upd
            for o, rows in zip(outs, chunks):
                finish(o, rows)
        st_ref[...] = st

    @pl.when(jnp.logical_not(factorisable))
    def _():
        ones = jnp.ones((HEAD_DIM, HEAD_DIM), BF16)
        sub_t = lax.broadcasted_iota(jnp.int32, (SB, HEAD_DIM), 0)

        def chunk(c, carry):
            r0 = pl.multiple_of(c * C, C)
            rows = pl.ds(r0, C)
            q, kk, b = q_s[rows, :], k_s[rows, :], b_s[rows, :]
            v = zi_ref[rows, :]
            vb = v.astype(BF16)
            st = st_ref[...]
            o_inter = lax.dot_general((q * jnp.exp(b)).astype(BF16), st.astype(BF16), NT_DIMS,
                                      preferred_element_type=F32)
            outs = []
            for i in range(C // SB):
                lo_r = i * SB
                q_i = q[lo_r:lo_r + SB, :]
                b_i = b[lo_r:lo_r + SB, :]
                o_i = o_inter[lo_r:lo_r + SB, :]
                if i > 0:
                    bref = b[lo_r - 1:lo_r, :]
                    qs = (q_i * jnp.exp(b_i - bref)).astype(BF16)
                    ks = (kk[:lo_r, :] * jnp.exp(bref - b[:lo_r, :])).astype(BF16)
                    a = lax.dot_general(qs, ks, NT_DIMS, preferred_element_type=F32)
                    o_i = o_i + jnp.dot(a.astype(BF16), vb[:lo_r, :], preferred_element_type=F32)
                slabs = []
                for s in range(SB):
                    b_row = b_s[pl.ds(r0 + lo_r + s, 1), :]
                    k_row = k_s[pl.ds(r0 + lo_r + s, 1), :]
                    w = q_i * k_row * jnp.exp(jnp.minimum(b_i - b_row, 0.0))
                    slabs.append(jnp.where(sub_t >= s, w, 0.0).astype(BF16))
                red = jnp.dot(jnp.concatenate(slabs, axis=0), ones, preferred_element_type=F32)
                for s in range(SB):
                    o_i = o_i + red[s * SB:(s + 1) * SB, :] * v[lo_r + s:lo_r + s + 1, :]
                outs.append(o_i)
            st_ref[...] = state_step(st, vb, kk, b)
            finish(jnp.concatenate(outs, axis=0), rows)
            return carry

        lax.fori_loop(0, n_chunks, chunk, 0)


def _hgrn(z, lb_param, g_norm, layer, batch, seq, tb=4096):
    nt = seq // tb
    nh = N_HEADS

    def col(off):
        return pl.BlockSpec((tb, HEAD_DIM), lambda b, h, t, off=off: (b * nt + t, off + h))

    return pl.pallas_call(
        functools.partial(_hgrn_kernel, layer),
        grid=(batch, nh, nt),
        in_specs=[col(0), col(nh), col(2 * nh), col(3 * nh),
                  pl.BlockSpec((lb_param.shape[0], HEAD_DIM), lambda b, h, t: (0, h)),
                  pl.BlockSpec((1, HEAD_DIM), lambda b, h, t: (0, 0))],
        out_specs=pl.BlockSpec((tb, HEAD_DIM), lambda b, h, t: (b * nt + t, h)),
        out_shape=jax.ShapeDtypeStruct((batch * seq, GROUP_WIDTH), BF16),
        scratch_shapes=[pltpu.VMEM((HEAD_DIM, HEAD_DIM), F32),
                        pltpu.VMEM((tb, HEAD_DIM), F32),
                        pltpu.VMEM((tb, HEAD_DIM), F32),
                        pltpu.VMEM((tb, HEAD_DIM), F32)],
        compiler_params=_params(("parallel", "parallel", "arbitrary")),
        name="hgrn2",
    )(z, z, z, z, lb_param, g_norm)


def _attn_kernel(q_ref, k_ref, kp_ref, v_ref, vp_ref, gn_ref, w1_ref, w2_ref,
                 o_ref, w1b_ref, w2b_ref,
                 o_mid, c_mid, o_low, c_low, bias_buf, k_cache, v_cache):
    TQ, Q, G = ATTN_TILE, ATTN_BLOCK, ATTN_GROUP
    t = pl.program_id(2)
    o_bufs, c_bufs = (o_mid, o_low), (c_mid, c_low)

    w1b_ref[...] = w1_ref[...].astype(BF16)
    w2b_ref[...] = w2_ref[...].astype(BF16)

    @pl.when(t == 0)
    def _():
        k_cache[...] = jnp.zeros_like(k_cache)
        v_cache[...] = jnp.zeros_like(v_cache)

    qi = lax.broadcasted_iota(jnp.int32, (Q, 2 * Q), 0)
    kj = lax.broadcasted_iota(jnp.int32, (Q, 2 * Q), 1)
    dist = kj - qi
    band = jnp.where((dist >= 0) & (dist <= Q), 0.0, -jnp.inf)
    first_key = jnp.where(t > 0, 0, Q)
    bias_buf[0] = band
    bias_buf[1] = jnp.where(kj >= first_key, band, -jnp.inf)
    ones_v = jnp.ones((2 * Q, HEAD_DIM), BF16)
    gn = gn_ref[...]
    def run_branch(d, wider, narrower):
        nb_per_res = TQ // (Q * d)

        def contiguous(start):
            return pl.ds(start if isinstance(start, int) else pl.multiple_of(start, Q), Q)

        def rows_of(start):
            return pl.ds(start, Q, stride=d) if d > 1 else contiguous(start)

        def group(res, nb0, at_start):
            ids = []
            for j in range(G):
                if nb_per_res >= G:
                    ids.append((res, nb0 + j, at_start and j == 0))
                else:
                    ids.append((res + j // nb_per_res, j % nb_per_res, j % nb_per_res == 0))
            blocks = [(res_b + d * Q * nb_b, first) for res_b, nb_b, first in ids]

            def with_prev(cur_ref, prev_ref, cache, r0, first):
                cur = cur_ref[rows_of(r0), :].astype(BF16)
                if nb_per_res == 1:
                    prev = cache[r0]
                    fresh.append((cache, r0, cur))
                elif first:
                    prev = prev_ref[rows_of(r0 + TQ - d * Q), :].astype(BF16)
                else:
                    prev = cur_ref[rows_of(r0 - d * Q), :].astype(BF16)
                return jnp.concatenate([prev, cur], axis=0)

            fresh = []
            qs = [q_ref[rows_of(r0), :].astype(BF16) for r0, _ in blocks]
            ks = [with_prev(k_ref, kp_ref, k_cache, r0, first) for r0, first in blocks]
            ss = [lax.dot_general(q, k, NT_DIMS, preferred_element_type=F32)
                  + bias_buf[1 if first else 0] for q, k, (_, first) in zip(qs, ks, blocks)]
            ms = [jnp.max(s, axis=-1, keepdims=True) for s in ss]
            ps = [jnp.exp2(s - m).astype(BF16) for s, m in zip(ss, ms)]
            vs = [with_prev(v_ref, vp_ref, v_cache, r0, first) for r0, first in blocks]
            accs = [jnp.dot(p, jnp.concatenate([v, ones_v], axis=1), preferred_element_type=F32)
                    for p, v in zip(ps, vs)]
            for cache, r0, cur in fresh:
                cache[r0] = cur
            ls = [acc[:, HEAD_DIM:] for acc in accs]
            nums = [acc[:, :HEAD_DIM] for acc in accs]
            if wider is None:
                os_ = [num * (1.0 / l) for num, l in zip(nums, ls)]
                lses = [m + jnp.log2(l) for m, l in zip(ms, ls)]
            else:
                o_in, c_in = wider
                spans = [(res_b, contiguous(Q * nb_b)) for res_b, nb_b, _ in ids]
                c_ws = [c_in[res_b, rows, :] for res_b, rows in spans]
                deltas = [m - c_w for m, c_w in zip(ms, c_ws)]
                smalls = [jnp.exp2(-jnp.abs(delta)) for delta in deltas]
                w_own = [jnp.where(delta >= 0.0, 1.0, e) for delta, e in zip(deltas, smalls)]
                w_in = [jnp.where(delta >= 0.0, e, 1.0) for delta, e in zip(deltas, smalls)]
                dens = [wo * l + wi for wo, l, wi in zip(w_own, ls, w_in)]
                os_ = [(wo * num + wi * o_in[res_b, rows, :]) * (1.0 / den)
                       for wo, num, wi, (res_b, rows), den in zip(w_own, nums, w_in, spans, dens)]
                if narrower is not None:
                    lses = [jnp.maximum(m, c_w) + jnp.log2(den)
                            for m, c_w, den in zip(ms, c_ws, dens)]
            if narrower is not None:
                d_next, o_out, c_out = narrower
                ratio = d // d_next
                for (res_b, nb_b, _), o, lse in zip(ids, os_, lses):
                    rows = pl.ds(res_b // d_next + ratio * Q * nb_b, Q, stride=ratio)
                    o_out[res_b % d_next, rows, :] = o
                    c_out[res_b % d_next, rows, :] = lse
            else:
                for (r0, _), o in zip(blocks, os_):
                    o_ref[rows_of(r0), :] = _rms(o, gn).astype(o_ref.dtype)

        def loop(lo, hi, body):
            for it in range(lo, hi):
                body(it)

        if nb_per_res > G:
            for res in range(d):
                group(res, 0, True)
                loop(1, nb_per_res // G, lambda it: group(res, it * G, False))
        elif nb_per_res == G:
            loop(0, d, lambda res: group(res, 0, True))
        else:
            per_group = G // nb_per_res
            loop(0, d // per_group, lambda it: group(it * per_group, 0, True))

    order = sorted(DILATIONS, reverse=True)
    for k, d in enumerate(order):
        wider = (o_bufs[k - 1], c_bufs[k - 1]) if k > 0 else None
        narrower = (order[k + 1], o_bufs[k], c_bufs[k]) if k + 1 < len(order) else None
        run_branch(d, wider, narrower)


def _attn(z, g_norm, batch, seq, layer, w1_rows, w2_rows):
    TQ = ATTN_TILE
    nt = seq // TQ
    nh = N_HEADS
    _, d_mid, d_low = sorted(DILATIONS, reverse=True)
    steps = batch * nh * nt
    dff, d = w1_rows.shape[1], w2_rows.shape[1]
    slab1, slab2 = d // steps, dff // steps

    def col(off, back=0):
        return pl.BlockSpec((TQ, HEAD_DIM),
                            lambda b, h, t: (b * nt + jnp.maximum(t - back, 0), off + h))

    step_of = lambda b, h, t: (b * nh + h) * nt + t
    return pl.pallas_call(
        _attn_kernel,
        grid=(batch, nh, nt),
        in_specs=[col(4 * nh), col(5 * nh), col(5 * nh, 1), col(6 * nh), col(6 * nh, 1),
                  pl.BlockSpec((1, HEAD_DIM), lambda b, h, t: (0, 0)),
                  pl.BlockSpec((slab1, dff), lambda b, h, t: (layer * steps + step_of(b, h, t), 0)),
                  pl.BlockSpec((slab2, d), lambda b, h, t: (layer * steps + step_of(b, h, t), 0))],
        out_specs=[pl.BlockSpec((TQ, HEAD_DIM), lambda b, h, t: (b * nt + t, h)),
                   pl.BlockSpec((slab1, dff), lambda b, h, t: (step_of(b, h, t), 0)),
                   pl.BlockSpec((slab2, d), lambda b, h, t: (step_of(b, h, t), 0))],
        out_shape=[jax.ShapeDtypeStruct((batch * seq, GROUP_WIDTH), BF16),
                   jax.ShapeDtypeStruct((d, dff), BF16), jax.ShapeDtypeStruct((dff, d), BF16)],
        scratch_shapes=[pltpu.VMEM((d_mid, TQ // d_mid, HEAD_DIM), F32),
                        pltpu.VMEM((d_mid, TQ // d_mid, HEAD_DIM), F32),
                        pltpu.VMEM((d_low, TQ // d_low, HEAD_DIM), F32),
                        pltpu.VMEM((d_low, TQ // d_low, HEAD_DIM), F32),
                        pltpu.VMEM((2, ATTN_BLOCK, 2 * ATTN_BLOCK), F32),
                        pltpu.VMEM((max(DILATIONS), ATTN_BLOCK, HEAD_DIM), BF16),
                        pltpu.VMEM((max(DILATIONS), ATTN_BLOCK, HEAD_DIM), BF16)],
        compiler_params=_params(("parallel", "parallel", "arbitrary")),
        name="dilated_attn",
    )(z, z, z, z, z, g_norm, w1_rows, w2_rows)


def _out_proj_kernel(oh_ref, oa_ref, w_ref, h_ref, g_ref, h_out_ref, u_out_ref):
    half = oh_ref.shape[1]
    acc = jnp.dot(oh_ref[...], w_ref[pl.ds(0, half), :], preferred_element_type=F32)
    acc = acc + jnp.dot(oa_ref[...], w_ref[pl.ds(half, half), :], preferred_element_type=F32)
    h = h_ref[...] + acc
    h_out_ref[...] = h
    u_out_ref[...] = _rms(h, g_ref[...]).astype(u_out_ref.dtype)


def _out_proj(oh, oa, w_rows, layer, h, gain, tm=512):
    m, d = h.shape
    half = oh.shape[1]
    row = lambda i: (i, 0)
    fixed = lambda i: (0, 0)
    return pl.pallas_call(
        _out_proj_kernel,
        grid=(m // tm,),
        in_specs=[pl.BlockSpec((tm, half), row), pl.BlockSpec((tm, half), row),
                  pl.BlockSpec((2 * half, d), lambda i: (layer, 0)), pl.BlockSpec((tm, d), row),
                  pl.BlockSpec((1, d), fixed)],
        out_specs=[pl.BlockSpec((tm, d), row), pl.BlockSpec((tm, d), row)],
        out_shape=[jax.ShapeDtypeStruct((m, d), F32), jax.ShapeDtypeStruct((m, d), BF16)],
        compiler_params=_params(("parallel",)),
        name="out_proj",
    )(oh, oa, w_rows, h, gain)


def _mlp_kernel(u_ref, w1_ref, w2_ref, h_ref, o_ref):
    f = pl.program_id(1)

    def step(base_ref):
        a = jnp.maximum(jnp.dot(u_ref[...], w1_ref[...], preferred_element_type=F32), 0.0)
        o_ref[...] = base_ref[...] + jnp.dot((a * a).astype(BF16), w2_ref[...],
                                             preferred_element_type=F32)

    @pl.when(f == 0)
    def _():
        step(h_ref)

    @pl.when(f > 0)
    def _():
        step(o_ref)


MLP_TF = 512


def _mlp(u, w1_rows, w2_rows, layer, h, tm=1024, tf=MLP_TF):
    m, d = h.shape
    nf = w1_rows.shape[1] // tf
    return pl.pallas_call(
        _mlp_kernel,
        grid=(m // tm, nf),
        in_specs=[pl.BlockSpec((tm, d), lambda i, f: (i, 0)),
                  pl.BlockSpec((d, tf), lambda i, f: (layer, f)),
                  pl.BlockSpec((tf, d), lambda i, f: (layer * nf + f, 0)),
                  pl.BlockSpec((tm, d), lambda i, f: (i, 0))],
        out_specs=pl.BlockSpec((tm, d), lambda i, f: (i, 0)),
        out_shape=jax.ShapeDtypeStruct((m, d), F32),
        compiler_params=_params(("parallel", "arbitrary")),
        name="mlp",
    )(u, w1_rows, w2_rows, h)


def _ple_kernel(last, h_ref, p_ref, wg_ref, wp_ref, gp_ref, gn_ref, *out_refs):
    h = h_ref[...]
    u = _rms(h, gp_ref[...]).astype(BF16)
    zg = jnp.dot(u, wg_ref[...], preferred_element_type=F32)
    gate = 1.0 / (1.0 + jnp.exp(-zg))
    pe = jnp.dot(p_ref[...].astype(BF16), wp_ref[...], preferred_element_type=F32)
    h = h + pe * gate
    nxt = _rms(h, gn_ref[...])
    if last:
        out_refs[0][...] = nxt
    else:
        out_refs[0][...] = h
        out_refs[1][...] = nxt.astype(out_refs[1].dtype)


def _ple(h, p, layer, wg_rows, wp_rows, g_ple, g_next, last, tm=512):
    m, d = h.shape
    pd = p.shape[1]
    row = lambda i: (i, 0)
    fixed = lambda i: (0, 0)
    of_layer = lambda i: (layer, 0)
    p_row = lambda i: (layer * (m // tm) + i, 0)
    if last:
        out_specs = [pl.BlockSpec((tm, d), row)]
        out_shape = [jax.ShapeDtypeStruct((m, d), F32)]
    else:
        out_specs = [pl.BlockSpec((tm, d), row), pl.BlockSpec((tm, d), row)]
        out_shape = [jax.ShapeDtypeStruct((m, d), F32), jax.ShapeDtypeStruct((m, d), BF16)]
    return pl.pallas_call(
        functools.partial(_ple_kernel, last),
        grid=(m // tm,),
        in_specs=[pl.BlockSpec((tm, d), row), pl.BlockSpec((tm, pd), p_row),
                  pl.BlockSpec((d, d), of_layer), pl.BlockSpec((pd, d), of_layer),
                  pl.BlockSpec((1, d), fixed), pl.BlockSpec((1, d), fixed)],
        out_specs=out_specs,
        out_shape=out_shape,
        compiler_params=_params(("parallel",)),
        name="ple",
    )(h, p, wg_rows, wp_rows, g_ple, g_next)


def kernel(x, p, positions, norm1, w_in, lb_param, hgrn_norm, attn_norm, w_out, norm2, w1, w2,
           ple_norm, w_pg, w_pp, final_norm):
    batch, seq, d = x.shape
    depth, _, in_cols = w_in.shape
    m = batch * seq
    assert seq % ATTN_TILE == 0 and d == 2 * GROUP_WIDTH

    w_in_r = w_in.reshape(depth * d, in_cols)
    w1_r, w2_r = w1.reshape(depth * d, -1), w2.reshape(-1, d)
    w_out_r, w_pg_r, w_pp_r = _rows_bf16(w_out), _rows_bf16(w_pg), _rows_bf16(w_pp)

    cos, sin = _rope_tables(positions)
    h = x.reshape(m, d)
    p_rows = p.reshape(depth * m, p.shape[-1])
    u = _norm(h, norm1[0].reshape(1, d), BF16)
    out = None
    for i in range(depth):
        last = i == depth - 1
        z = _in_proj(u, w_in_r, i, cos, sin)
        o_hgrn = _hgrn(z, lb_param, hgrn_norm[i].reshape(1, HEAD_DIM), i, batch, seq)
        o_attn, w1_b, w2_b = _attn(z, attn_norm[i].reshape(1, HEAD_DIM), batch, seq,
                                   i, w1_r, w2_r)
        h, u2 = _out_proj(o_hgrn, o_attn, w_out_r, i, h, norm2[i].reshape(1, d))
        h = _mlp(u2, w1_b, w2_b, 0, h)
        g_next = final_norm if last else norm1[i + 1]
        res = _ple(h, p_rows, i, w_pg_r, w_pp_r, ple_norm[i].reshape(1, d),
                   g_next.reshape(1, d), last)
        if last:
            out = res[0]
        else:
            h, u = res
    return out.reshape(batch, seq, d)
```

```python
import functools

import jax
import jax.numpy as jnp
from jax import lax
from jax.experimental import pallas as pl
from jax.experimental.pallas import tpu as pltpu

F32 = jnp.float32
BF16 = jnp.bfloat16

HEAD_DIM = 128
N_HEADS = 8
GROUP_WIDTH = N_HEADS * HEAD_DIM
STREAMS = ("hgrn_q", "hgrn_f", "hgrn_i", "hgrn_g", "attn_q", "attn_k", "attn_v")
ROT_DIM = HEAD_DIM // 4
ROPE_THETA = 500000.0
NORM_EPS = 1e-6
LOG2E = 1.4426950408889634
QUERY_SCALE = HEAD_DIM ** -0.5 * LOG2E
HGRN_CHUNK = 64
HGRN_GROUP = 16
HGRN_SUB = 16
HGRN_MAX_HALF_DECAY = 60.0
ATTN_BLOCK = 128
ATTN_TILE = 2048
ATTN_GROUP = 8
DILATIONS = (1, 4, 16)
VMEM_LIMIT = 56 * 1024 * 1024

NT_DIMS = (((1,), (1,)), ((), ()))
TN_DIMS = (((0,), (0,)), ((), ()))


def _params(semantics):
    return pltpu.CompilerParams(dimension_semantics=semantics, vmem_limit_bytes=VMEM_LIMIT)


def _rms(x, gain):
    ms = jnp.mean(x * x, axis=-1, keepdims=True)
    return x * lax.rsqrt(ms + NORM_EPS) * gain


def _silu(x):
    return x / (1.0 + jnp.exp(-x))


def _norm_kernel(x_ref, g_ref, o_ref):
    o_ref[...] = _rms(x_ref[...], g_ref[...]).astype(o_ref.dtype)


def _norm(x, gain, out_dtype, tm=512):
    m, d = x.shape
    return pl.pallas_call(
        _norm_kernel,
        grid=(m // tm,),
        in_specs=[pl.BlockSpec((tm, d), lambda i: (i, 0)),
                  pl.BlockSpec((1, d), lambda i: (0, 0))],
        out_specs=pl.BlockSpec((tm, d), lambda i: (i, 0)),
        out_shape=jax.ShapeDtypeStruct((m, d), out_dtype),
        compiler_params=_params(("parallel",)),
        name="rmsnorm",
    )(x, gain)


def _rope_kernel(pos_ref, invf_ref, cos_ref, sin_ref):
    ang = pos_ref[...] * invf_ref[...]
    lane = lax.broadcasted_iota(jnp.int32, ang.shape, 1)
    s = jnp.sin(ang)
    cos_ref[...] = jnp.cos(ang)
    sin_ref[...] = jnp.where(lane < HEAD_DIM // 2, -s, s)


def _rope_tables(positions, tr=1024):
    n = positions.size
    pos = positions.astype(F32).reshape(n, 1)
    half = ROT_DIM // 2
    inv = 1.0 / (ROPE_THETA ** (jnp.arange(0, ROT_DIM, 2, dtype=F32) / ROT_DIM))
    gap = jnp.zeros((HEAD_DIM // 2 - half,), F32)
    invf = jnp.concatenate([inv, gap, inv, gap]).reshape(1, HEAD_DIM)
    return pl.pallas_call(
        _rope_kernel,
        grid=(n // tr,),
        in_specs=[pl.BlockSpec((tr, 1), lambda i: (i, 0)),
                  pl.BlockSpec((1, HEAD_DIM), lambda i: (0, 0))],
        out_specs=[pl.BlockSpec((tr, HEAD_DIM), lambda i: (i, 0))] * 2,
        out_shape=[jax.ShapeDtypeStruct((n, HEAD_DIM), F32)] * 2,
        compiler_params=_params(("parallel",)),
        name="rope_tables",
    )(pos, invf)


def _rows_bf16(w):
    return w.astype(BF16).reshape(-1, w.shape[-1])


def _pair_rotary_lanes(w):
    half = ROT_DIM // 2
    lane = lax.broadcasted_iota(jnp.int32, w.shape, 1)
    up = pltpu.roll(w, HEAD_DIM - half, axis=1)
    down = pltpu.roll(w, HEAD_DIM // 2 - half, axis=1)
    moved = jnp.where(lane < HEAD_DIM // 2, up, down)
    keep = jnp.logical_or(lane < half, lane >= HEAD_DIM // 2 + half)
    return jnp.where(keep, w, moved)


def _rotary(x, cos, sin_signed):
    return x * cos + pltpu.roll(x, HEAD_DIM // 2, axis=1) * sin_signed


def _in_proj_kernel(q_tile, k_tile, x_ref, w_ref, cos_ref, sin_ref, o_ref, wb_ref):
    j = pl.program_id(0)
    first_row_tile = pl.program_id(1) == 0
    rotated = jnp.logical_or(j == q_tile, j == k_tile)
    plain = jnp.logical_not(rotated)
    heads = [slice(h * HEAD_DIM, (h + 1) * HEAD_DIM) for h in range(o_ref.shape[1] // HEAD_DIM)]

    @pl.when(jnp.logical_and(first_row_tile, plain))
    def _():
        wb_ref[...] = w_ref[...].astype(BF16)

    @pl.when(jnp.logical_and(first_row_tile, rotated))
    def _():
        for cols in heads:
            wb_ref[:, cols] = _pair_rotary_lanes(w_ref[:, cols]).astype(BF16)

    @pl.when(plain)
    def _():
        o_ref[...] = jnp.dot(x_ref[...], wb_ref[...], preferred_element_type=F32)

    @pl.when(rotated)
    def _():
        z = jnp.dot(x_ref[...], wb_ref[...], preferred_element_type=F32)
        scale = jnp.where(j == q_tile, QUERY_SCALE, 1.0)
        cos = cos_ref[...] * scale
        sin = sin_ref[...] * scale
        for cols in heads:
            o_ref[:, cols] = _rotary(z[:, cols], cos, sin)


def _in_proj(u, w_rows, layer, cos, sin, tm=1024, tn=GROUP_WIDTH):
    m, k = u.shape
    n = w_rows.shape[1]
    tab = pl.BlockSpec((tm, HEAD_DIM), lambda j, i: (i, 0))
    assert tn == GROUP_WIDTH
    q_tile, k_tile = STREAMS.index("attn_q"), STREAMS.index("attn_k")
    return pl.pallas_call(
        functools.partial(_in_proj_kernel, q_tile, k_tile),
        grid=(n // tn, m // tm),
        in_specs=[pl.BlockSpec((tm, k), lambda j, i: (i, 0)),
                  pl.BlockSpec((k, tn), lambda j, i: (layer, j)), tab, tab],
        out_specs=pl.BlockSpec((tm, tn), lambda j, i: (i, j)),
        out_shape=jax.ShapeDtypeStruct((m, n), F32),
        scratch_shapes=[pltpu.VMEM((k, tn), BF16)],
        compiler_params=_params(("parallel", "arbitrary")),
        name="in_proj",
    )(u, w_rows, cos, sin)


def _hgrn_kernel(layer, zq_ref, zf_ref, zi_ref, zg_ref, lbp_ref, gn_ref, wo_ref, wg_ref,
                 o_ref, wob_ref, wgb_ref, st_ref, q_s, k_s, b_s):
    C, SB = HGRN_CHUNK, HGRN_SUB
    tb = zq_ref.shape[0]
    n_chunks = tb // C
    mid = C // 2 - 1

    @pl.when(pl.program_id(2) == 0)
    def _():
        st_ref[...] = jnp.zeros_like(st_ref)

    wob_ref[...] = wo_ref[...].astype(BF16)
    wgb_ref[...] = wg_ref[...].astype(BF16)

    lbp = lbp_ref[...]
    e = jnp.exp(lbp - jnp.max(lbp, axis=0, keepdims=True))
    sm = e / jnp.sum(e, axis=0, keepdims=True)
    lb = jnp.zeros((1, HEAD_DIM), F32)
    for j in range(1, layer + 1):
        lb = lb + sm[j:j + 1, :]
    lb_pos = lb > 0.0
    log1m_lb = jnp.log1p(-lb)
    one_m_lb = 1.0 - lb
    gn = gn_ref[...]

    row = lax.broadcasted_iota(jnp.int32, (C, C), 0)
    col = lax.broadcasted_iota(jnp.int32, (C, C), 1)
    causal = row >= col
    tril = jnp.where(causal, 1.0, 0.0).astype(BF16)

    def log_decay(rows):
        zf = zf_ref[rows, :]
        en = jnp.exp(-jnp.abs(zf))
        one_p = 1.0 + en
        rcp = 1.0 / one_p
        small = en * rcp
        nonneg = zf >= 0.0
        k_s[rows, :] = one_m_lb * jnp.where(nonneg, small, rcp)
        log_sig = jnp.minimum(zf, 0.0) - jnp.log(one_p)
        if layer == 0:
            return log_sig
        f = lb + one_m_lb * jnp.where(nonneg, rcp, small)
        return jnp.where(lb_pos, jnp.log(f), log1m_lb + log_sig)

    worst = jnp.zeros((1, HEAD_DIM), F32)
    for g0 in range(0, n_chunks, HGRN_GROUP):
        chunks = [pl.ds(c * C, C) for c in range(g0, min(g0 + HGRN_GROUP, n_chunks))]
        lfs = [log_decay(rows) for rows in chunks]
        his = [lf.astype(BF16) for lf in lfs]
        los = [(lf - hi.astype(F32)).astype(BF16) for lf, hi in zip(lfs, his)]
        bbs = [jnp.dot(tril, jnp.concatenate([hi, lo], axis=1), preferred_element_type=F32)
               for hi, lo in zip(his, los)]
        for rows, bb in zip(chunks, bbs):
            b = bb[:, :HEAD_DIM] + bb[:, HEAD_DIM:]
            b_s[rows, :] = b
            worst = jnp.maximum(worst, jnp.maximum(-b[mid:mid + 1, :],
                                                   b[mid:mid + 1, :] - b[C - 1:C, :]))
        for rows in chunks:
            q_s[rows, :] = _silu(zq_ref[rows, :])
    factorisable = jnp.max(worst) <= HGRN_MAX_HALF_DECAY

    def finish(o, rows):
        o = _rms(o, gn) * _silu(zg_ref[rows, :])
        o_ref[rows, :] = o.astype(o_ref.dtype)

    def state_step(st, vb, kk, b):
        b_end = b[C - 1:C, :]
        kd = (kk * jnp.exp(b_end - b)).astype(BF16)
        return st * jnp.exp(b_end) + lax.dot_general(vb, kd, TN_DIMS, preferred_element_type=F32)

    @pl.when(factorisable)
    def _():
        st = st_ref[...]
        for g0 in range(0, n_chunks, HGRN_GROUP):
            chunks = [pl.ds(c * C, C) for c in range(g0, min(g0 + HGRN_GROUP, n_chunks))]
            bs = [b_s[rows, :] for rows in chunks]
            rs = [b[mid:mid + 1, :] for b in bs]
            ts = [(b - r) * LOG2E for b, r in zip(bs, rs)]
            q_mids = [q_s[rows, :] * jnp.exp2(t) for rows, t in zip(chunks, ts)]
            k_mids = [k_s[rows, :] * jnp.exp2(-t) for rows, t in zip(chunks, ts)]
            vbs = [zi_ref[rows, :].astype(BF16) for rows in chunks]
            scores = [lax.dot_general(qm.astype(BF16), km.astype(BF16), NT_DIMS,
                                      preferred_element_type=F32)
                      for qm, km in zip(q_mids, k_mids)]
            kds = [(km * jnp.exp(b[C - 1:C, :] - r)).astype(BF16)
                   for km, b, r in zip(k_mids, bs, rs)]
            upds = [lax.dot_general(vb, kd, TN_DIMS, preferred_element_type=F32)
                    for vb, kd in zip(vbs, kds)]
            qes = [(qm * jnp.exp(r)).astype(BF16) for qm, r in zip(q_mids, rs)]
            intras = [jnp.dot(jnp.where(causal, a, 0.0).astype(BF16), vb,
                              preferred_element_type=F32) for a, vb in zip(scores, vbs)]
            outs = []
            for qe, b, upd, intra in zip(qes, bs, upds, intras):
                outs.append(intra + lax.dot_general(qe, st.astype(BF16), NT_DIMS,
                                                    preferred_element_type=F32))
                st = st * jnp.exp(b[C - 1:C, :]) + upd
            for o, rows in zip(outs, chunks):
                finish(o, rows)
        st_ref[...] = st

    @pl.when(jnp.logical_not(factorisable))
    def _():
        ones = jnp.ones((HEAD_DIM, HEAD_DIM), BF16)
        sub_t = lax.broadcasted_iota(jnp.int32, (SB, HEAD_DIM), 0)

        def chunk(c, carry):
            r0 = pl.multiple_of(c * C, C)
            rows = pl.ds(r0, C)
            q, kk, b = q_s[rows, :], k_s[rows, :], b_s[rows, :]
            v = zi_ref[rows, :]
            vb = v.astype(BF16)
            st = st_ref[...]
            o_inter = lax.dot_general((q * jnp.exp(b)).astype(BF16), st.astype(BF16), NT_DIMS,
                                      preferred_element_type=F32)
            outs = []
            for i in range(C // SB):
                lo_r = i * SB
                q_i = q[lo_r:lo_r + SB, :]
                b_i = b[lo_r:lo_r + SB, :]
                o_i = o_inter[lo_r:lo_r + SB, :]
                if i > 0:
                    bref = b[lo_r - 1:lo_r, :]
                    qs = (q_i * jnp.exp(b_i - bref)).astype(BF16)
                    ks = (kk[:lo_r, :] * jnp.exp(bref - b[:lo_r, :])).astype(BF16)
                    a = lax.dot_general(qs, ks, NT_DIMS, preferred_element_type=F32)
                    o_i = o_i + jnp.dot(a.astype(BF16), vb[:lo_r, :], preferred_element_type=F32)
                slabs = []
                for s in range(SB):
                    b_row = b_s[pl.ds(r0 + lo_r + s, 1), :]
                    k_row = k_s[pl.ds(r0 + lo_r + s, 1), :]
                    w = q_i * k_row * jnp.exp(jnp.minimum(b_i - b_row, 0.0))
                    slabs.append(jnp.where(sub_t >= s, w, 0.0).astype(BF16))
                red = jnp.dot(jnp.concatenate(slabs, axis=0), ones, preferred_element_type=F32)
                for s in range(SB):
                    o_i = o_i + red[s * SB:(s + 1) * SB, :] * v[lo_r + s:lo_r + s + 1, :]
                outs.append(o_i)
            st_ref[...] = state_step(st, vb, kk, b)
            finish(jnp.concatenate(outs, axis=0), rows)
            return carry

        lax.fori_loop(0, n_chunks, chunk, 0)


def _hgrn(z, lb_param, g_norm, layer, batch, seq, w_out_rows, w_pg_rows, tb=4096):
    nt = seq // tb
    nh = N_HEADS
    steps = batch * nh * nt
    d = w_out_rows.shape[1]
    slab = d // steps
    slab_in = lambda b, h, t: (layer * steps + (b * nh + h) * nt + t, 0)
    slab_out = lambda b, h, t: ((b * nh + h) * nt + t, 0)

    def col(stream):
        off = STREAMS.index(stream) * nh
        return pl.BlockSpec((tb, HEAD_DIM), lambda b, h, t: (b * nt + t, off + h))

    return pl.pallas_call(
        functools.partial(_hgrn_kernel, layer),
        grid=(batch, nh, nt),
        in_specs=[col("hgrn_q"), col("hgrn_f"), col("hgrn_i"), col("hgrn_g"),
                  pl.BlockSpec((lb_param.shape[0], HEAD_DIM), lambda b, h, t: (0, h)),
                  pl.BlockSpec((1, HEAD_DIM), lambda b, h, t: (0, 0)),
                  pl.BlockSpec((slab, d), slab_in), pl.BlockSpec((slab, d), slab_in)],
        out_specs=[pl.BlockSpec((tb, HEAD_DIM), lambda b, h, t: (b * nt + t, h)),
                   pl.BlockSpec((slab, d), slab_out), pl.BlockSpec((slab, d), slab_out)],
        out_shape=[jax.ShapeDtypeStruct((batch * seq, GROUP_WIDTH), BF16),
                   jax.ShapeDtypeStruct((d, d), BF16), jax.ShapeDtypeStruct((d, d), BF16)],
        scratch_shapes=[pltpu.VMEM((HEAD_DIM, HEAD_DIM), F32),
                        pltpu.VMEM((tb, HEAD_DIM), F32),
                        pltpu.VMEM((tb, HEAD_DIM), F32),
                        pltpu.VMEM((tb, HEAD_DIM), F32)],
        compiler_params=_params(("parallel", "parallel", "arbitrary")),
        name="hgrn2",
    )(z, z, z, z, lb_param, g_norm, w_out_rows, w_pg_rows)


def _attn_kernel(q_ref, k_ref, kp_ref, v_ref, vp_ref, gn_ref, w1_ref, w2_ref,
                 o_ref, w1b_ref, w2b_ref,
                 o_mid, c_mid, o_low, c_low, bias_buf, k_cache, v_cache):
    TQ, Q, G = ATTN_TILE, ATTN_BLOCK, ATTN_GROUP
    t = pl.program_id(2)
    o_bufs, c_bufs = (o_mid, o_low), (c_mid, c_low)

    w1b_ref[...] = w1_ref[...].astype(BF16)
    w2b_ref[...] = w2_ref[...].astype(BF16)

    @pl.when(t == 0)
    def _():
        k_cache[...] = jnp.zeros_like(k_cache)
        v_cache[...] = jnp.zeros_like(v_cache)

    qi = lax.broadcasted_iota(jnp.int32, (Q, 2 * Q), 0)
    kj = lax.broadcasted_iota(jnp.int32, (Q, 2 * Q), 1)
    dist = kj - qi
    band = jnp.where((dist >= 0) & (dist <= Q), 0.0, -jnp.inf)
    first_key = jnp.where(t > 0, 0, Q)
    bias_buf[0] = band
    bias_buf[1] = jnp.where(kj >= first_key, band, -jnp.inf)
    ones_v = jnp.ones((2 * Q, HEAD_DIM), BF16)
    gn = gn_ref[...]
    def run_branch(d, wider, narrower):
        nb_per_res = TQ // (Q * d)

        def contiguous(start):
            return pl.ds(start if isinstance(start, int) else pl.multiple_of(start, Q), Q)

        def rows_of(start):
            return pl.ds(start, Q, stride=d) if d > 1 else contiguous(start)

        def group(res, nb0, at_start):
            ids = []
            for j in range(G):
                if nb_per_res >= G:
                    ids.append((res, nb0 + j, at_start and j == 0))
                else:
                    ids.append((res + j // nb_per_res, j % nb_per_res, j % nb_per_res == 0))
            blocks = [(res_b + d * Q * nb_b, first) for res_b, nb_b, first in ids]

            def with_prev(cur_ref, prev_ref, cache, r0, first):
                cur = cur_ref[rows_of(r0), :].astype(BF16)
                if nb_per_res == 1:
                    prev = cache[r0]
                    fresh.append((cache, r0, cur))
                elif first:
                    prev = prev_ref[rows_of(r0 + TQ - d * Q), :].astype(BF16)
                else:
                    prev = cur_ref[rows_of(r0 - d * Q), :].astype(BF16)
                return jnp.concatenate([prev, cur], axis=0)

            fresh = []
            qs = [q_ref[rows_of(r0), :].astype(BF16) for r0, _ in blocks]
            ks = [with_prev(k_ref, kp_ref, k_cache, r0, first) for r0, first in blocks]
            ss = [lax.dot_general(q, k, NT_DIMS, preferred_element_type=F32)
                  + bias_buf[1 if first else 0] for q, k, (_, first) in zip(qs, ks, blocks)]
            ms = [jnp.max(s, axis=-1, keepdims=True) for s in ss]
            ps = [jnp.exp2(s - m).astype(BF16) for s, m in zip(ss, ms)]
            vs = [with_prev(v_ref, vp_ref, v_cache, r0, first) for r0, first in blocks]
            accs = [jnp.dot(p, jnp.concatenate([v, ones_v], axis=1), preferred_element_type=F32)
                    for p, v in zip(ps, vs)]
            for cache, r0, cur in fresh:
                cache[r0] = cur
            ls = [acc[:, HEAD_DIM:] for acc in accs]
            nums = [acc[:, :HEAD_DIM] for acc in accs]
            if wider is None:
                os_ = [num * (1.0 / l) for num, l in zip(nums, ls)]
                lses = [m + jnp.log2(l) for m, l in zip(ms, ls)]
            else:
                o_in, c_in = wider
                spans = [(res_b, contiguous(Q * nb_b)) for res_b, nb_b, _ in ids]
                c_ws = [c_in[res_b, rows, :] for res_b, rows in spans]
                deltas = [m - c_w for m, c_w in zip(ms, c_ws)]
                smalls = [jnp.exp2(-jnp.abs(delta)) for delta in deltas]
                w_own = [jnp.where(delta >= 0.0, 1.0, e) for delta, e in zip(deltas, smalls)]
                w_in = [jnp.where(delta >= 0.0, e, 1.0) for delta, e in zip(deltas, smalls)]
                dens = [wo * l + wi for wo, l, wi in zip(w_own, ls, w_in)]
                os_ = [(wo * num + wi * o_in[res_b, rows, :]) * (1.0 / den)
                       for wo, num, wi, (res_b, rows), den in zip(w_own, nums, w_in, spans, dens)]
                if narrower is not None:
                    lses = [jnp.maximum(m, c_w) + jnp.log2(den)
                            for m, c_w, den in zip(ms, c_ws, dens)]
            if narrower is not None:
                d_next, o_out, c_out = narrower
                ratio = d // d_next
                for (res_b, nb_b, _), o, lse in zip(ids, os_, lses):
                    rows = pl.ds(res_b // d_next + ratio * Q * nb_b, Q, stride=ratio)
                    o_out[res_b % d_next, rows, :] = o
                    c_out[res_b % d_next, rows, :] = lse
            else:
                for (r0, _), o in zip(blocks, os_):
                    o_ref[rows_of(r0), :] = _rms(o, gn).astype(o_ref.dtype)

        def loop(lo, hi, body):
            for it in range(lo, hi):
                body(it)

        if nb_per_res > G:
            for res in range(d):
                group(res, 0, True)
                loop(1, nb_per_res // G, lambda it: group(res, it * G, False))
        elif nb_per_res == G:
            loop(0, d, lambda res: group(res, 0, True))
        else:
            per_group = G // nb_per_res
            loop(0, d // per_group, lambda it: group(it * per_group, 0, True))

    order = sorted(DILATIONS, reverse=True)
    for k, d in enumerate(order):
        wider = (o_bufs[k - 1], c_bufs[k - 1]) if k > 0 else None
        narrower = (order[k + 1], o_bufs[k], c_bufs[k]) if k + 1 < len(order) else None
        run_branch(d, wider, narrower)


def _attn(z, g_norm, batch, seq, layer, w1_rows, w2_rows):
    TQ = ATTN_TILE
    nt = seq // TQ
    nh = N_HEADS
    _, d_mid, d_low = sorted(DILATIONS, reverse=True)
    steps = batch * nh * nt
    dff, d = w1_rows.shape[1], w2_rows.shape[1]
    slab1, slab2 = d // steps, dff // steps

    def col(stream, back=0):
        off = STREAMS.index(stream) * nh
        return pl.BlockSpec((TQ, HEAD_DIM),
                            lambda b, h, t: (b * nt + jnp.maximum(t - back, 0), off + h))

    step_of = lambda b, h, t: (b * nh + h) * nt + t
    return pl.pallas_call(
        _attn_kernel,
        grid=(batch, nh, nt),
        in_specs=[col("attn_q"), col("attn_k"), col("attn_k", 1), col("attn_v"),
                  col("attn_v", 1),
                  pl.BlockSpec((1, HEAD_DIM), lambda b, h, t: (0, 0)),
                  pl.BlockSpec((slab1, dff), lambda b, h, t: (layer * steps + step_of(b, h, t), 0)),
                  pl.BlockSpec((slab2, d), lambda b, h, t: (layer * steps + step_of(b, h, t), 0))],
        out_specs=[pl.BlockSpec((TQ, HEAD_DIM), lambda b, h, t: (b * nt + t, h)),
                   pl.BlockSpec((slab1, dff), lambda b, h, t: (step_of(b, h, t), 0)),
                   pl.BlockSpec((slab2, d), lambda b, h, t: (step_of(b, h, t), 0))],
        out_shape=[jax.ShapeDtypeStruct((batch * seq, GROUP_WIDTH), BF16),
                   jax.ShapeDtypeStruct((d, dff), BF16), jax.ShapeDtypeStruct((dff, d), BF16)],
        scratch_shapes=[pltpu.VMEM((d_mid, TQ // d_mid, HEAD_DIM), F32),
                        pltpu.VMEM((d_mid, TQ // d_mid, HEAD_DIM), F32),
                        pltpu.VMEM((d_low, TQ // d_low, HEAD_DIM), F32),
                        pltpu.VMEM((d_low, TQ // d_low, HEAD_DIM), F32),
                        pltpu.VMEM((2, ATTN_BLOCK, 2 * ATTN_BLOCK), F32),
                        pltpu.VMEM((max(DILATIONS), ATTN_BLOCK, HEAD_DIM), BF16),
                        pltpu.VMEM((max(DILATIONS), ATTN_BLOCK, HEAD_DIM), BF16)],
        compiler_params=_params(("parallel", "parallel", "arbitrary")),
        name="dilated_attn",
    )(z, z, z, z, z, g_norm, w1_rows, w2_rows)


def _out_proj_kernel(oh_ref, oa_ref, w_ref, h_ref, g_ref, h_out_ref, u_out_ref):
    half = oh_ref.shape[1]
    acc = jnp.dot(oh_ref[...], w_ref[pl.ds(0, half), :], preferred_element_type=F32)
    acc = acc + jnp.dot(oa_ref[...], w_ref[pl.ds(half, half), :], preferred_element_type=F32)
    h = h_ref[...] + acc
    h_out_ref[...] = h
    u_out_ref[...] = _rms(h, g_ref[...]).astype(u_out_ref.dtype)


def _out_proj(oh, oa, w_rows, layer, h, gain, tm=512):
    m, d = h.shape
    half = oh.shape[1]
    row = lambda i: (i, 0)
    fixed = lambda i: (0, 0)
    return pl.pallas_call(
        _out_proj_kernel,
        grid=(m // tm,),
        in_specs=[pl.BlockSpec((tm, half), row), pl.BlockSpec((tm, half), row),
                  pl.BlockSpec((2 * half, d), lambda i: (layer, 0)), pl.BlockSpec((tm, d), row),
                  pl.BlockSpec((1, d), fixed)],
        out_specs=[pl.BlockSpec((tm, d), row), pl.BlockSpec((tm, d), row)],
        out_shape=[jax.ShapeDtypeStruct((m, d), F32), jax.ShapeDtypeStruct((m, d), BF16)],
        compiler_params=_params(("parallel",)),
        name="out_proj",
    )(oh, oa, w_rows, h, gain)


def _mlp_kernel(u_ref, w1_ref, w2_ref, h_ref, o_ref):
    f = pl.program_id(1)

    def step(base_ref):
        a = jnp.maximum(jnp.dot(u_ref[...], w1_ref[...], preferred_element_type=F32), 0.0)
        o_ref[...] = base_ref[...] + jnp.dot((a * a).astype(BF16), w2_ref[...],
                                             preferred_element_type=F32)

    @pl.when(f == 0)
    def _():
        step(h_ref)

    @pl.when(f > 0)
    def _():
        step(o_ref)


MLP_TF = 512


def _mlp(u, w1_rows, w2_rows, layer, h, tm=1024, tf=MLP_TF):
    m, d = h.shape
    nf = w1_rows.shape[1] // tf
    return pl.pallas_call(
        _mlp_kernel,
        grid=(m // tm, nf),
        in_specs=[pl.BlockSpec((tm, d), lambda i, f: (i, 0)),
                  pl.BlockSpec((d, tf), lambda i, f: (layer, f)),
                  pl.BlockSpec((tf, d), lambda i, f: (layer * nf + f, 0)),
                  pl.BlockSpec((tm, d), lambda i, f: (i, 0))],
        out_specs=pl.BlockSpec((tm, d), lambda i, f: (i, 0)),
        out_shape=jax.ShapeDtypeStruct((m, d), F32),
        compiler_params=_params(("parallel", "arbitrary")),
        name="mlp",
    )(u, w1_rows, w2_rows, h)


def _ple_kernel(last, h_ref, p_ref, wg_ref, wp_ref, gp_ref, gn_ref, *out_refs):
    h = h_ref[...]
    u = _rms(h, gp_ref[...]).astype(BF16)
    zg = jnp.dot(u, wg_ref[...], preferred_element_type=F32)
    gate = 1.0 / (1.0 + jnp.exp(-zg))
    pe = jnp.dot(p_ref[...].astype(BF16), wp_ref[...], preferred_element_type=F32)
    h = h + pe * gate
    nxt = _rms(h, gn_ref[...])
    if last:
        out_refs[0][...] = nxt
    else:
        out_refs[0][...] = h
        out_refs[1][...] = nxt.astype(out_refs[1].dtype)


def _ple(h, p, layer, w_pg, wp_rows, g_ple, g_next, last, tm=512):
    m, d = h.shape
    pd = p.shape[1]
    row = lambda i: (i, 0)
    fixed = lambda i: (0, 0)
    of_layer = lambda i: (layer, 0)
    p_row = lambda i: (layer * (m // tm) + i, 0)
    if last:
        out_specs = [pl.BlockSpec((tm, d), row)]
        out_shape = [jax.ShapeDtypeStruct((m, d), F32)]
    else:
        out_specs = [pl.BlockSpec((tm, d), row), pl.BlockSpec((tm, d), row)]
        out_shape = [jax.ShapeDtypeStruct((m, d), F32), jax.ShapeDtypeStruct((m, d), BF16)]
    return pl.pallas_call(
        functools.partial(_ple_kernel, last),
        grid=(m // tm,),
        in_specs=[pl.BlockSpec((tm, d), row), pl.BlockSpec((tm, pd), p_row),
                  pl.BlockSpec((d, d), fixed), pl.BlockSpec((pd, d), of_layer),
                  pl.BlockSpec((1, d), fixed), pl.BlockSpec((1, d), fixed)],
        out_specs=out_specs,
        out_shape=out_shape,
        compiler_params=_params(("parallel",)),
        name="ple",
    )(h, p, w_pg, wp_rows, g_ple, g_next)


def kernel(x, p, positions, norm1, w_in, lb_param, hgrn_norm, attn_norm, w_out, norm2, w1, w2,
           ple_norm, w_pg, w_pp, final_norm):
    batch, seq, d = x.shape
    depth, _, in_cols = w_in.shape
    m = batch * seq
    assert seq % ATTN_TILE == 0 and d == 2 * GROUP_WIDTH

    w_in_r = w_in.reshape(depth * d, in_cols)
    w1_r, w2_r = w1.reshape(depth * d, -1), w2.reshape(-1, d)
    w_out_r, w_pg_r, w_pp_r = w_out.reshape(-1, d), w_pg.reshape(-1, d), _rows_bf16(w_pp)

    cos, sin = _rope_tables(positions)
    h = x.reshape(m, d)
    p_rows = p.reshape(depth * m, p.shape[-1])
    u = _norm(h, norm1[0].reshape(1, d), BF16)
    out = None
    for i in range(depth):
        last = i == depth - 1
        z = _in_proj(u, w_in_r, i, cos, sin)
        o_hgrn, w_out_b, w_pg_b = _hgrn(z, lb_param, hgrn_norm[i].reshape(1, HEAD_DIM), i,
                                        batch, seq, w_out_r, w_pg_r)
        o_attn, w1_b, w2_b = _attn(z, attn_norm[i].reshape(1, HEAD_DIM), batch, seq,
                                   i, w1_r, w2_r)
        h, u2 = _out_proj(o_hgrn, o_attn, w_out_b, 0, h, norm2[i].reshape(1, d))
        h = _mlp(u2, w1_b, w2_b, 0, h)
        g_next = final_norm if last else norm1[i + 1]
        res = _ple(h, p_rows, i, w_pg_b, w_pp_r, ple_norm[i].reshape(1, d),
                   g_next.reshape(1, d), last)
        if last:
            out = res[0]
        else:
            h, u = res
    return out.reshape(batch, seq, d)
```
